```python
import math
import jax, jax.numpy as jnp
from jax import lax
import numpy as np

D_MODEL = 2048
BATCH = 1
SEQ = 8192
DEPTH = 1

N_MEM = 256
D_FF = 5632
MIX_WIDTH = D_MODEL
DIFF_WIDTH = MIX_WIDTH // 2
CONV_WIDTH = MIX_WIDTH - DIFF_WIDTH
DIFF_HEADS = 8
DIFF_VDIM = DIFF_WIDTH // DIFF_HEADS
DIFF_QKDIM = DIFF_VDIM // 2
CONV_GROUPS = 8
CONV_K = 31
ROT_DIM = DIFF_QKDIM // 4
ROPE_THETA = 500000.0
XATTN_HEADS = 4
XATTN_HDIM = 128
XATTN_WIDTH = XATTN_HEADS * XATTN_HDIM
Q_BLOCK = 128
EPS = 1e-6
IN_COLS = 3 * DIFF_WIDTH + 2 * CONV_WIDTH

kernel_name = "hybrid_diffattn_conformer_conv_macaron"


def rms_norm(x, g):
    xf = x.astype(jnp.float32)
    y = xf * lax.rsqrt(jnp.mean(xf * xf, axis=-1, keepdims=True) + EPS)
    return (y * g.astype(jnp.float32)).astype(x.dtype)


def layer_norm(x, g, b):
    xf = x.astype(jnp.float32)
    mu = jnp.mean(xf, axis=-1, keepdims=True)
    var = jnp.mean(jnp.square(xf - mu), axis=-1, keepdims=True)
    y = (xf - mu) * lax.rsqrt(var + EPS)
    return (y * g.astype(jnp.float32) + b.astype(jnp.float32)).astype(x.dtype)


def swiglu(h, w_gate, w_up, w_down):
    return (jax.nn.silu(h @ w_gate) * (h @ w_up)) @ w_down


def rope_cos_sin(positions):
    inv_freq = ROPE_THETA ** (-jnp.arange(0, ROT_DIM, 2, dtype=jnp.float32) / ROT_DIM)
    ang = positions.astype(jnp.float32)[..., None] * inv_freq
    return jnp.cos(ang), jnp.sin(ang)


def apply_partial_rope(x, cos, sin):
    cos = cos[:, :, None, None, :]
    sin = sin[:, :, None, None, :]
    xr = x[..., :ROT_DIM].astype(jnp.float32)
    xp = x[..., ROT_DIM:]
    x1, x2 = xr[..., :ROT_DIM // 2], xr[..., ROT_DIM // 2:]
    rot = jnp.concatenate([x1 * cos - x2 * sin, x2 * cos + x1 * sin], axis=-1)
    return jnp.concatenate([rot.astype(x.dtype), xp], axis=-1)


def causal_diff_attention(q, k, v, lam):
    b, s, h, _, d = q.shape
    nblk = s // Q_BLOCK
    scale = 1.0 / math.sqrt(d)
    qb = q.reshape(b, nblk, Q_BLOCK, h, 2, d).transpose(1, 0, 2, 3, 4, 5)
    kpos = jnp.arange(s)

    def one_block(args):
        qi, i = args
        sc = jnp.einsum('bqhcd,bkhcd->bhcqk', qi, k).astype(jnp.float32) * scale
        qpos = i * Q_BLOCK + jnp.arange(Q_BLOCK)
        mask = kpos[None, :] <= qpos[:, None]
        sc = jnp.where(mask, sc, -jnp.inf)
        p = jax.nn.softmax(sc, axis=-1)
        a = p[:, :, 0] - lam * p[:, :, 1]
        return jnp.einsum('bhqk,bkhe->bqhe', a.astype(v.dtype), v)

    out = lax.map(one_block, (qb, jnp.arange(nblk)))
    return out.transpose(1, 0, 2, 3, 4).reshape(b, s, h, v.shape[-1])


def conformer_conv_group(val, gate, conv_w, conv_b, ln_g, ln_b):
    u = val * jax.nn.sigmoid(gate)
    y = lax.conv_general_dilated(
        u, conv_w[:, None, :],
        window_strides=(1,), padding=[(CONV_K - 1, 0)],
        dimension_numbers=('NWC', 'WIO', 'NWC'),
        feature_group_count=CONV_WIDTH)
    y = y + conv_b
    return jax.nn.silu(layer_norm(y, ln_g, ln_b))


def memory_cross_attention(h, mem_h, w_q, w_kv, q_g, k_g, w_o):
    b, s, _ = h.shape
    q = rms_norm((h @ w_q).reshape(b, s, XATTN_HEADS, XATTN_HDIM), q_g)
    kv = mem_h @ w_kv
    k = rms_norm(kv[..., :XATTN_WIDTH].reshape(b, N_MEM, XATTN_HEADS, XATTN_HDIM), k_g)
    v = kv[..., XATTN_WIDTH:].reshape(b, N_MEM, XATTN_HEADS, XATTN_HDIM)
    sc = jnp.einsum('bqhd,bkhd->bhqk', q, k).astype(jnp.float32) * (1.0 / math.sqrt(XATTN_HDIM))
    p = jax.nn.softmax(sc, axis=-1)
    o = jnp.einsum('bhqk,bkhd->bqhd', p.astype(v.dtype), v).reshape(b, s, XATTN_WIDTH)
    return o @ w_o


def setup_inputs(seed: int = 0) -> dict:
    key = jax.random.key(seed)
    ks = iter(jax.random.split(key, 40))

    def nrm(shape, scale):
        return jax.random.normal(next(ks), shape, jnp.float32) * scale

    def gain(shape):
        return 1.0 + 0.02 * jax.random.normal(next(ks), shape, jnp.float32)

    L = DEPTH
    return {
        "x": nrm((BATCH, SEQ, D_MODEL), 1.0),
        "mem": nrm((BATCH, N_MEM, D_MODEL), 1.0),
        "positions": jnp.broadcast_to(jnp.arange(SEQ, dtype=jnp.int32), (BATCH, SEQ)),
        "ffn1_norm_g": gain((L, D_MODEL)),
        "ffn1_w_gate": nrm((L, D_MODEL, D_FF), D_MODEL ** -0.5),
        "ffn1_w_up": nrm((L, D_MODEL, D_FF), D_MODEL ** -0.5),
        "ffn1_w_down": nrm((L, D_FF, D_MODEL), D_FF ** -0.5),
        "mix_norm_g": gain((L, D_MODEL)),
        "w_in": nrm((L, D_MODEL, IN_COLS), D_MODEL ** -0.5),
        "q_norm_g": gain((L, DIFF_QKDIM)),
        "k_norm_g": gain((L, DIFF_QKDIM)),
        "lambda_q1": nrm((L, DIFF_QKDIM), 0.1),
        "lambda_k1": nrm((L, DIFF_QKDIM), 0.1),
        "lambda_q2": nrm((L, DIFF_QKDIM), 0.1),
        "lambda_k2": nrm((L, DIFF_QKDIM), 0.1),
        "diff_head_norm_g": gain((L, DIFF_VDIM)),
        "conv_w": nrm((L, CONV_K, CONV_WIDTH), CONV_K ** -0.5),
        "conv_b": nrm((L, CONV_WIDTH), 0.02),
        "conv_ln_g": gain((L, CONV_WIDTH)),
        "conv_ln_b": nrm((L, CONV_WIDTH), 0.02),
        "w_out": nrm((L, MIX_WIDTH, D_MODEL), MIX_WIDTH ** -0.5),
        "xattn_norm_g": gain((L, D_MODEL)),
        "mem_norm_g": gain((L, D_MODEL)),
        "xattn_w_q": nrm((L, D_MODEL, XATTN_WIDTH), D_MODEL ** -0.5),
        "xattn_w_kv": nrm((L, D_MODEL, 2 * XATTN_WIDTH), D_MODEL ** -0.5),
        "xattn_q_norm_g": gain((L, XATTN_HDIM)),
        "xattn_k_norm_g": gain((L, XATTN_HDIM)),
        "xattn_w_o": nrm((L, XATTN_WIDTH, D_MODEL), XATTN_WIDTH ** -0.5),
        "ffn2_norm_g": gain((L, D_MODEL)),
        "ffn2_w_gate": nrm((L, D_MODEL, D_FF), D_MODEL ** -0.5),
        "ffn2_w_up": nrm((L, D_MODEL, D_FF), D_MODEL ** -0.5),
        "ffn2_w_down": nrm((L, D_FF, D_MODEL), D_FF ** -0.5),
    }


def reference(x, mem, positions,
              ffn1_norm_g, ffn1_w_gate, ffn1_w_up, ffn1_w_down,
              mix_norm_g, w_in, q_norm_g, k_norm_g,
              lambda_q1, lambda_k1, lambda_q2, lambda_k2, diff_head_norm_g,
              conv_w, conv_b, conv_ln_g, conv_ln_b, w_out,
              xattn_norm_g, mem_norm_g, xattn_w_q, xattn_w_kv,
              xattn_q_norm_g, xattn_k_norm_g, xattn_w_o,
              ffn2_norm_g, ffn2_w_gate, ffn2_w_up, ffn2_w_down):
    b, s, _ = x.shape
    cos, sin = rope_cos_sin(positions)

    for l in range(DEPTH):
        x = x + 0.5 * swiglu(rms_norm(x, ffn1_norm_g[l]), ffn1_w_gate[l], ffn1_w_up[l], ffn1_w_down[l])

        h = rms_norm(x, mix_norm_g[l])
        p = h @ w_in[l]
        q = p[..., :DIFF_WIDTH].reshape(b, s, DIFF_HEADS, 2, DIFF_QKDIM)
        k = p[..., DIFF_WIDTH:2 * DIFF_WIDTH].reshape(b, s, DIFF_HEADS, 2, DIFF_QKDIM)
        v = p[..., 2 * DIFF_WIDTH:3 * DIFF_WIDTH].reshape(b, s, DIFF_HEADS, DIFF_VDIM)
        c_val = p[..., 3 * DIFF_WIDTH:3 * DIFF_WIDTH + CONV_WIDTH]
        c_gate = p[..., 3 * DIFF_WIDTH + CONV_WIDTH:]

        q = apply_partial_rope(rms_norm(q, q_norm_g[l]), cos, sin)
        k = apply_partial_rope(rms_norm(k, k_norm_g[l]), cos, sin)
        lambda_init = 0.8 - 0.6 * math.exp(-0.3 * l)
        lam = (jnp.exp(jnp.sum(lambda_q1[l].astype(jnp.float32) * lambda_k1[l].astype(jnp.float32)))
               - jnp.exp(jnp.sum(lambda_q2[l].astype(jnp.float32) * lambda_k2[l].astype(jnp.float32)))
               + lambda_init)
        o_diff = causal_diff_attention(q, k, v, lam)
        o_diff = rms_norm(o_diff, diff_head_norm_g[l]) * (1.0 - lambda_init)
        o_diff = o_diff.reshape(b, s, DIFF_WIDTH)

        o_conv = conformer_conv_group(c_val, c_gate, conv_w[l], conv_b[l], conv_ln_g[l], conv_ln_b[l])

        x = x + jnp.concatenate([o_diff, o_conv], axis=-1) @ w_out[l]

        x = x + memory_cross_attention(rms_norm(x, xattn_norm_g[l]), rms_norm(mem, mem_norm_g[l]),
                                       xattn_w_q[l], xattn_w_kv[l],
                                       xattn_q_norm_g[l], xattn_k_norm_g[l], xattn_w_o[l])

        x = x + 0.5 * swiglu(rms_norm(x, ffn2_norm_g[l]), ffn2_w_gate[l], ffn2_w_up[l], ffn2_w_down[l])
    return x
```

```python
import functools
import math

import jax
import jax.numpy as jnp
from jax import lax
from jax.experimental import pallas as pl
from jax.experimental.pallas import tpu as pltpu

F32 = jnp.float32
BF16 = jnp.bfloat16

EPS = 1e-6
ROPE_THETA = 500000.0
DIFF_HEADS = 8
DIFF_VDIM = 128
DIFF_QKDIM = 64
ROT_DIM = 16
CONV_K = 31
XATTN_HEADS = 4
XATTN_HDIM = 128
LAMBDA_INIT = 0.8 - 0.6 * math.exp(0.0)

LANES = 128
MXU_DIM = 256
VMEM_LIMIT = 56 * 1024 * 1024

NEG_BIG = -1e30


def _rms_rows(x, g):
    ms = jnp.mean(x * x, axis=-1, keepdims=True)
    return x * lax.rsqrt(ms + EPS) * g


def _ffn_kernel(x_ref, g_ref, wg_ref, wu_ref, wd_ref, o_ref, h_ref, *, n_ff_steps, n_chunk):
    j = pl.program_id(1)

    @pl.when(j == 0)
    def _():
        h_ref[...] = _rms_rows(x_ref[...], g_ref[...]).astype(BF16)

    h = h_ref[...]
    a = jnp.dot(h, wg_ref[...], preferred_element_type=F32)
    b = jnp.dot(h, wu_ref[...], preferred_element_type=F32)
    act = (a * jax.nn.sigmoid(a) * b).astype(BF16)
    d_model = o_ref.shape[1]
    for c in range(d_model // n_chunk):
        sl = slice(c * n_chunk, (c + 1) * n_chunk)
        d = jnp.dot(act, wd_ref[:, sl], preferred_element_type=F32)

        @pl.when(j == 0)
        def _():
            o_ref[:, sl] = d

        @pl.when(j > 0)
        def _():
            o_ref[:, sl] += d

    @pl.when(j == n_ff_steps - 1)
    def _():
        o_ref[...] = x_ref[...] + 0.5 * o_ref[...]


def _ffn(x, g, wg, wu, wd, *, tm=512, tf=512):
    s, d = x.shape
    f = wg.shape[1]
    n_ff_steps = f // tf
    kern = functools.partial(_ffn_kernel, n_ff_steps=n_ff_steps, n_chunk=512)
    return pl.pallas_call(
        kern,
        grid=(s // tm, n_ff_steps),
        in_specs=[
            pl.BlockSpec((tm, d), lambda i, j: (i, 0)),
            pl.BlockSpec((1, d), lambda i, j: (0, 0)),
            pl.BlockSpec((d, tf), lambda i, j: (0, j)),
            pl.BlockSpec((d, tf), lambda i, j: (0, j)),
            pl.BlockSpec((tf, d), lambda i, j: (j, 0)),
        ],
        out_specs=pl.BlockSpec((tm, d), lambda i, j: (i, 0)),
        out_shape=jax.ShapeDtypeStruct((s, d), F32),
        scratch_shapes=[pltpu.VMEM((tm, d), BF16)],
        compiler_params=pltpu.CompilerParams(
            dimension_semantics=("arbitrary", "arbitrary"), vmem_limit_bytes=VMEM_LIMIT),
        name="ffn",
    )(x, g, wg, wu, wd)


def _group_sumsq(p, bd_ref):
    p2 = p * p
    hi = p2.astype(BF16)
    lo = (p2 - hi.astype(F32)).astype(BF16)
    outs = []
    for c in range(p.shape[1] // MXU_DIM):
        sl = slice(c * MXU_DIM, (c + 1) * MXU_DIM)
        outs.append(jnp.dot(hi[:, sl], bd_ref[...], preferred_element_type=F32)
                    + jnp.dot(lo[:, sl], bd_ref[...], preferred_element_type=F32))
    return jnp.concatenate(outs, axis=1)


def _qk_norm_rope(p, gain, bd_ref, rc_ref, ra_ref, rb_ref, scale):
    ss = _group_sumsq(p, bd_ref)
    xn = p * lax.rsqrt(ss * (1.0 / DIFF_QKDIM) + EPS) * gain
    rc, ra, rb = rc_ref[...], ra_ref[...], rb_ref[...]
    outs = []
    for c in range(p.shape[1] // LANES):
        xc = xn[:, c * LANES:(c + 1) * LANES]
        up = pltpu.roll(xc, LANES - ROT_DIM // 2, axis=1)
        dn = pltpu.roll(xc, ROT_DIM // 2, axis=1)
        outs.append(xc * rc + up * ra + dn * rb)
    out = jnp.concatenate(outs, axis=1)
    if scale != 1.0:
        out = out * scale
    return out.astype(BF16)


def _proj_kernel(x_ref, g_ref, w_ref, qg_ref, kg_ref, bd_ref, rc_ref, ra_ref, rb_ref,
                 qk_ref, vt_ref, u_ref, h_ref, val_ref):
    j = pl.program_id(1)

    @pl.when(j == 0)
    def _():
        h_ref[...] = _rms_rows(x_ref[...], g_ref[...]).astype(BF16)

    p = jnp.dot(h_ref[...], w_ref[...], preferred_element_type=F32)

    @pl.when(j == 0)
    def _():
        qk_ref[...] = _qk_norm_rope(p, qg_ref[...], bd_ref, rc_ref, ra_ref, rb_ref,
                                    1.0 / math.sqrt(DIFF_QKDIM))

    @pl.when(j == 1)
    def _():
        qk_ref[...] = _qk_norm_rope(p, kg_ref[...], bd_ref, rc_ref, ra_ref, rb_ref, 1.0)

    @pl.when(j == 2)
    def _():
        vt_ref[...] = p.T.astype(BF16)

    @pl.when(j == 3)
    def _():
        val_ref[...] = p

    @pl.when(j == 4)
    def _():
        u_ref[...] = val_ref[...] * jax.nn.sigmoid(p)


def _proj(x, g, w_in, qg, kg, bd, rc, ra, rb, *, tm=512):
    s, d = x.shape
    w = 1024
    return pl.pallas_call(
        _proj_kernel,
        grid=(s // tm, 5),
        in_specs=[
            pl.BlockSpec((tm, d), lambda i, j: (i, 0)),
            pl.BlockSpec((1, d), lambda i, j: (0, 0)),
            pl.BlockSpec((d, w), lambda i, j: (0, j)),
            pl.BlockSpec((1, w), lambda i, j: (0, 0)),
            pl.BlockSpec((1, w), lambda i, j: (0, 0)),
            pl.BlockSpec((MXU_DIM, MXU_DIM), lambda i, j: (0, 0)),
            pl.BlockSpec((tm, LANES), lambda i, j: (i, 0)),
            pl.BlockSpec((tm, LANES), lambda i, j: (i, 0)),
            pl.BlockSpec((tm, LANES), lambda i, j: (i, 0)),
        ],
        out_specs=[
            pl.BlockSpec((None, tm, w), lambda i, j: (jnp.minimum(j, 1), i, 0)),
            pl.BlockSpec((w, tm), lambda i, j: (0, i)),
            pl.BlockSpec((tm, w), lambda i, j: (i, 0)),
        ],
        out_shape=[
            jax.ShapeDtypeStruct((2, s, w), BF16),
            jax.ShapeDtypeStruct((w, s), BF16),
            jax.ShapeDtypeStruct((s, w), F32),
        ],
        scratch_shapes=[pltpu.VMEM((tm, d), BF16), pltpu.VMEM((tm, w), F32)],
        compiler_params=pltpu.CompilerParams(
            dimension_semantics=("arbitrary", "arbitrary"), vmem_limit_bytes=VMEM_LIMIT),
        name="proj",
    )(x, g, w_in, qg, kg, bd, rc, ra, rb)


_NT = (((1,), (1,)), ((), ()))


def _attn_kernel(q_ref, k_ref, vt_ref, lq1_ref, lk1_ref, lq2_ref, lk2_ref, hg_ref, o_ref,
                 acc1, acc2, m1, l1, m2, l2, *, t):
    i = pl.program_id(1)
    q = q_ref[...]
    lane = lax.broadcasted_iota(jnp.int32, q.shape, 1)
    zero = jnp.zeros_like(q)
    qz = (jnp.where(lane < DIFF_QKDIM, q, zero), jnp.where(lane >= DIFF_QKDIM, q, zero))
    state = ((acc1, m1, l1), (acc2, m2, l2))

    def block(j, first):
        start = pl.multiple_of(j * t, t)
        kb = k_ref[pl.ds(start, t), :]
        vb = vt_ref[:, pl.ds(start, t)]
        for c in range(2):
            acc, m, l = state[c]
            s = lax.dot_general(kb, qz[c], _NT, preferred_element_type=F32)
            if first:
                krow = lax.broadcasted_iota(jnp.int32, s.shape, 0)
                qcol = lax.broadcasted_iota(jnp.int32, s.shape, 1)
                s = jnp.where(krow <= qcol, s, NEG_BIG)
                m_new = jnp.max(s, axis=0, keepdims=True)
                p = jnp.exp(s - m_new)
                l[...] = jnp.sum(p, axis=0, keepdims=True)
                acc[...] = jnp.dot(vb, p.astype(BF16), preferred_element_type=F32)
            else:
                m_old = m[...]
                m_new = jnp.maximum(m_old, jnp.max(s, axis=0, keepdims=True))
                alpha = jnp.exp(m_old - m_new)
                p = jnp.exp(s - m_new)
                l[...] = alpha * l[...] + jnp.sum(p, axis=0, keepdims=True)
                acc[...] = alpha * acc[...] + jnp.dot(vb, p.astype(BF16), preferred_element_type=F32)
            m[...] = m_new

    block(i, True)

    def body(j, carry):
        block(j, False)
        return carry

    lax.fori_loop(0, i, body, 0)

    lam = (jnp.exp(jnp.sum(lq1_ref[...] * lk1_ref[...], axis=-1, keepdims=True))
           - jnp.exp(jnp.sum(lq2_ref[...] * lk2_ref[...], axis=-1, keepdims=True))
           + LAMBDA_INIT)
    ot = acc1[...] / l1[...] - lam * (acc2[...] / l2[...])
    ms = jnp.mean(ot * ot, axis=0, keepdims=True)
    on = ot * lax.rsqrt(ms + EPS)
    o_ref[...] = (on.T * hg_ref[...] * (1.0 - LAMBDA_INIT)).astype(BF16)


def _attn(qk, vt, lq1, lk1, lq2, lk2, hg, *, t=512):
    _, s, w = qk.shape
    nh = w // DIFF_VDIM
    vec = lambda n: pl.BlockSpec((1, n), lambda h, i: (0, 0))
    return pl.pallas_call(
        functools.partial(_attn_kernel, t=t),
        grid=(nh, s // t),
        in_specs=[
            pl.BlockSpec((None, t, DIFF_VDIM), lambda h, i: (0, i, h)),
            pl.BlockSpec((None, s, DIFF_VDIM), lambda h, i: (1, 0, h)),
            pl.BlockSpec((DIFF_VDIM, s), lambda h, i: (h, 0)),
            vec(DIFF_QKDIM), vec(DIFF_QKDIM), vec(DIFF_QKDIM), vec(DIFF_QKDIM),
            vec(DIFF_VDIM),
        ],
        out_specs=pl.BlockSpec((t, DIFF_VDIM), lambda h, i: (i, h)),
        out_shape=jax.ShapeDtypeStruct((s, w), BF16),
        scratch_shapes=[
            pltpu.VMEM((DIFF_VDIM, t), F32), pltpu.VMEM((DIFF_VDIM, t), F32),
            pltpu.VMEM((1, t), F32), pltpu.VMEM((1, t), F32),
            pltpu.VMEM((1, t), F32), pltpu.VMEM((1, t), F32),
        ],
        compiler_params=pltpu.CompilerParams(
            dimension_semantics=("arbitrary", "arbitrary"), vmem_limit_bytes=VMEM_LIMIT),
        name="attn",
    )(qk, qk, vt, lq1, lk1, lq2, lk2, hg)


_HALO = 32
_SUB = 8
_ROWS = 16


def _conv_kernel(cur_ref, halo_ref, w_ref, b_ref, g_ref, beta_ref, o_ref, e_ref, y_ref, *, tc):
    i = pl.program_id(0)
    e_ref[0, 0:_HALO, :] = jnp.where(i > 0, halo_ref[...], 0.0)
    e_ref[0, _HALO:_HALO + tc, :] = cur_ref[...]
    span = tc + _HALO - _SUB
    for r in range(1, _SUB):
        e_ref[r, 0:span, :] = e_ref[0, r:r + span, :]

    first = _HALO - (CONV_K - 1)

    def chunk(c, carry):
        t0 = pl.multiple_of(c * _ROWS, _ROWS)
        acc = jnp.zeros((_ROWS, cur_ref.shape[1]), F32)
        for k in range(CONV_K):
            off = first + k
            a, r = off // _SUB, off % _SUB
            acc = acc + w_ref[k:k + 1, :] * e_ref[r, pl.ds(t0 + a * _SUB, _ROWS), :]
        y_ref[pl.ds(t0, _ROWS), :] = acc
        return carry

    lax.fori_loop(0, tc // _ROWS, chunk, 0)

    y = y_ref[...] + b_ref[...]
    mu = jnp.mean(y, axis=-1, keepdims=True)
    yc = y - mu
    var = jnp.mean(yc * yc, axis=-1, keepdims=True)
    yn = yc * lax.rsqrt(var + EPS) * g_ref[...] + beta_ref[...]
    o_ref[...] = (yn * jax.nn.sigmoid(yn)).astype(BF16)


def _conv(u, w, b, g, beta, *, tc=512):
    s, c = u.shape
    vec = pl.BlockSpec((1, c), lambda i: (0, 0))
    halo_blocks = tc // _HALO
    return pl.pallas_call(
        functools.partial(_conv_kernel, tc=tc),
        grid=(s // tc,),
        in_specs=[
            pl.BlockSpec((tc, c), lambda i: (i, 0)),
            pl.BlockSpec((_HALO, c), lambda i: (jnp.maximum(i * halo_blocks - 1, 0), 0)),
            pl.BlockSpec((CONV_K, c), lambda i: (0, 0)),
            vec, vec, vec,
        ],
        out_specs=pl.BlockSpec((tc, c), lambda i: (i, 0)),
        out_shape=jax.ShapeDtypeStruct((s, c), BF16),
        scratch_shapes=[pltpu.VMEM((_SUB, tc + _HALO, c), F32), pltpu.VMEM((tc, c), F32)],
        compiler_params=pltpu.CompilerParams(
            dimension_semantics=("arbitrary",), vmem_limit_bytes=VMEM_LIMIT),
        name="conv",
    )(u, u, w, b, g, beta)


def _memkv_kernel(mem_ref, g_ref, w_ref, kg_ref, k_ref, v_ref):
    h = _rms_rows(mem_ref[...], g_ref[...]).astype(BF16)
    kv = jnp.dot(h, w_ref[...], preferred_element_type=F32)
    width = k_ref.shape[1]
    for hd in range(XATTN_HEADS):
        sl = slice(hd * XATTN_HDIM, (hd + 1) * XATTN_HDIM)
        k_ref[:, sl] = _rms_rows(kv[:, sl], kg_ref[...]).astype(BF16)
    v_ref[...] = kv[:, width:].astype(BF16)


def _memkv(mem, g, w_kv, kg):
    n, _ = mem.shape
    width = w_kv.shape[1] // 2
    return pl.pallas_call(
        _memkv_kernel,
        out_shape=[jax.ShapeDtypeStruct((n, width), BF16), jax.ShapeDtypeStruct((n, width), BF16)],
        compiler_params=pltpu.CompilerParams(vmem_limit_bytes=VMEM_LIMIT),
        name="memkv",
    )(mem, g, w_kv, kg)


def _post_kernel(x_ref, od_ref, oc_ref, wo_ref, g_ref, wq_ref, qg_ref, km_ref, vm_ref, wxo_ref, o_ref):
    half = od_ref.shape[1]
    x2 = (x_ref[...]
          + jnp.dot(od_ref[...], wo_ref[0:half, :], preferred_element_type=F32)
          + jnp.dot(oc_ref[...], wo_ref[half:, :], preferred_element_type=F32))
    h = _rms_rows(x2, g_ref[...]).astype(BF16)
    q = jnp.dot(h, wq_ref[...], preferred_element_type=F32)
    scale = 1.0 / math.sqrt(XATTN_HDIM)
    outs = []
    for hd in range(XATTN_HEADS):
        sl = slice(hd * XATTN_HDIM, (hd + 1) * XATTN_HDIM)
        qn = _rms_rows(q[:, sl], qg_ref[...]).astype(BF16)
        sc = lax.dot_general(qn, km_ref[:, sl], _NT, preferred_element_type=F32) * scale
        sc = sc - jnp.max(sc, axis=-1, keepdims=True)
        p = jnp.exp(sc)
        p = p / jnp.sum(p, axis=-1, keepdims=True)
        outs.append(jnp.dot(p.astype(BF16), vm_ref[:, sl], preferred_element_type=F32))
    o = jnp.concatenate(outs, axis=1).astype(BF16)
    o_ref[...] = x2 + jnp.dot(o, wxo_ref[...], preferred_element_type=F32)


def _post(x, od, oc, w_out, g, wq, qg, km, vm, wxo, *, tm=512):
    s, d = x.shape
    half = od.shape[1]
    xw = wq.shape[1]
    n_mem = km.shape[0]
    const = lambda shape: pl.BlockSpec(shape, lambda i: (0, 0))
    return pl.pallas_call(
        _post_kernel,
        grid=(s // tm,),
        in_specs=[
            pl.BlockSpec((tm, d), lambda i: (i, 0)),
            pl.BlockSpec((tm, half), lambda i: (i, 0)),
            pl.BlockSpec((tm, half), lambda i: (i, 0)),
            const((2 * half, d)), const((1, d)), const((d, xw)), const((1, XATTN_HDIM)),
            const((n_mem, xw)), const((n_mem, xw)), const((xw, d)),
        ],
        out_specs=pl.BlockSpec((tm, d), lambda i: (i, 0)),
        out_shape=jax.ShapeDtypeStruct((s, d), F32),
        compiler_params=pltpu.CompilerParams(
            dimension_semantics=("arbitrary",), vmem_limit_bytes=VMEM_LIMIT),
        name="post",
    )(x, od, oc, w_out, g, wq, qg, km, vm, wxo)


def _rope_tables(positions):
    half = ROT_DIM // 2
    inv_freq = ROPE_THETA ** (-jnp.arange(0, ROT_DIM, 2, dtype=F32) / ROT_DIM)
    ang = positions.astype(F32)[:, None] * inv_freq
    cos, sin = jnp.cos(ang), jnp.sin(ang)
    n = positions.shape[0]
    rest = DIFF_QKDIM - ROT_DIM
    rc = jnp.concatenate([cos, cos, jnp.ones((n, rest), F32)], axis=1)
    ra = jnp.concatenate([-sin, jnp.zeros((n, half + rest), F32)], axis=1)
    rb = jnp.concatenate([jnp.zeros((n, half), F32), sin, jnp.zeros((n, rest), F32)], axis=1)
    rep = LANES // DIFF_QKDIM
    return jnp.tile(rc, (1, rep)), jnp.tile(ra, (1, rep)), jnp.tile(rb, (1, rep))


def kernel(x, mem, positions, ffn1_norm_g, ffn1_w_gate, ffn1_w_up, ffn1_w_down, mix_norm_g, w_in, q_norm_g, k_norm_g, lambda_q1, lambda_k1, lambda_q2, lambda_k2, diff_head_norm_g, conv_w, conv_b, conv_ln_g, conv_ln_b, w_out, xattn_norm_g, mem_norm_g, xattn_w_q, xattn_w_kv, xattn_q_norm_g, xattn_k_norm_g, xattn_w_o, ffn2_norm_g, ffn2_w_gate, ffn2_w_up, ffn2_w_down):
    b, s, d = x.shape
    assert b == 1 and ffn1_norm_g.shape[0] == 1
    bf = lambda w: w[0].astype(BF16)
    row = lambda v: v[0][None, :]

    rc, ra, rb = _rope_tables(positions[0])
    width = DIFF_HEADS * DIFF_VDIM
    qg = jnp.tile(q_norm_g[0], width // DIFF_QKDIM)[None, :]
    kg = jnp.tile(k_norm_g[0], width // DIFF_QKDIM)[None, :]
    blk = jnp.arange(MXU_DIM) // DIFF_QKDIM
    bd = (blk[:, None] == blk[None, :]).astype(BF16)

    x1 = _ffn(x[0], row(ffn1_norm_g), bf(ffn1_w_gate), bf(ffn1_w_up), bf(ffn1_w_down))

    qk, vt, u = _proj(x1, row(mix_norm_g), bf(w_in), qg, kg, bd, rc, ra, rb)
    o_diff = _attn(qk, vt, row(lambda_q1), row(lambda_k1), row(lambda_q2), row(lambda_k2),
                   row(diff_head_norm_g))
    o_conv = _conv(u, conv_w[0], row(conv_b), row(conv_ln_g), row(conv_ln_b))

    km, vm = _memkv(mem[0], row(mem_norm_g), bf(xattn_w_kv), row(xattn_k_norm_g))
    x3 = _post(x1, o_diff, o_conv, bf(w_out), row(xattn_norm_g), bf(xattn_w_q), row(xattn_q_norm_g),
               km, vm, bf(xattn_w_o))

    out = _ffn(x3, row(ffn2_norm_g), bf(ffn2_w_gate), bf(ffn2_w_up), bf(ffn2_w_down))
    return out[None]
```

```python
import functools
import math

import jax
import jax.numpy as jnp
from jax import lax
from jax.experimental import pallas as pl
from jax.experimental.pallas import tpu as pltpu

F32 = jnp.float32
BF16 = jnp.bfloat16

EPS = 1e-6
ROPE_THETA = 500000.0
DIFF_HEADS = 8
DIFF_VDIM = 128
DIFF_QKDIM = 64
ROT_DIM = 16
CONV_K = 31
XATTN_HEADS = 4
XATTN_HDIM = 128
LAMBDA_INIT = 0.8 - 0.6 * math.exp(0.0)

LANES = 128
MXU_DIM = 256
VMEM_LIMIT = 56 * 1024 * 1024

NEG_BIG = -1e30


def _rms_rows(x, g):
    ms = jnp.mean(x * x, axis=-1, keepdims=True)
    return x * lax.rsqrt(ms + EPS) * g


_FFN_ROWS = 256


def _ffn_kernel(x_ref, g_ref, wg_ref, wu_ref, wd_ref, o_ref, h_ref, *, n_ff_steps):
    j = pl.program_id(1)

    @pl.when(j == 0)
    def _():
        h_ref[...] = _rms_rows(x_ref[...], g_ref[...]).astype(BF16)
        o_ref[...] = jnp.zeros_like(o_ref)

    for r in range(o_ref.shape[0] // _FFN_ROWS):
        rows = slice(r * _FFN_ROWS, (r + 1) * _FFN_ROWS)
        h = h_ref[rows, :]
        a = jnp.dot(h, wg_ref[...], preferred_element_type=F32)
        b = jnp.dot(h, wu_ref[...], preferred_element_type=F32)
        act = (a * jax.nn.sigmoid(a) * b).astype(BF16)
        o_ref[rows, :] += jnp.dot(act, wd_ref[...], preferred_element_type=F32)

    @pl.when(j == n_ff_steps - 1)
    def _():
        o_ref[...] = x_ref[...] + 0.5 * o_ref[...]


def _ffn(x, g, wg, wu, wd, *, tm=1024, tf=512):
    s, d = x.shape
    f = wg.shape[1]
    n_ff_steps = f // tf
    kern = functools.partial(_ffn_kernel, n_ff_steps=n_ff_steps)
    return pl.pallas_call(
        kern,
        grid=(s // tm, n_ff_steps),
        in_specs=[
            pl.BlockSpec((tm, d), lambda i, j: (i, 0)),
            pl.BlockSpec((1, d), lambda i, j: (0, 0)),
            pl.BlockSpec((d, tf), lambda i, j: (0, j)),
            pl.BlockSpec((d, tf), lambda i, j: (0, j)),
            pl.BlockSpec((tf, d), lambda i, j: (j, 0)),
        ],
        out_specs=pl.BlockSpec((tm, d), lambda i, j: (i, 0)),
        out_shape=jax.ShapeDtypeStruct((s, d), F32),
        scratch_shapes=[pltpu.VMEM((tm, d), BF16)],
        compiler_params=pltpu.CompilerParams(
            dimension_semantics=("arbitrary", "arbitrary"), vmem_limit_bytes=VMEM_LIMIT),
        name="ffn",
    )(x, g, wg, wu, wd)


def _group_sumsq(p, bd_ref):
    p2 = p * p
    hi = p2.astype(BF16)
    lo = (p2 - hi.astype(F32)).astype(BF16)
    outs = []
    for c in range(p.shape[1] // MXU_DIM):
        sl = slice(c * MXU_DIM, (c + 1) * MXU_DIM)
        outs.append(jnp.dot(hi[:, sl], bd_ref[...], preferred_element_type=F32)
                    + jnp.dot(lo[:, sl], bd_ref[...], preferred_element_type=F32))
    return jnp.concatenate(outs, axis=1)


def _qk_norm_rope(p, gain, bd_ref, rc_ref, ra_ref, rb_ref, scale):
    ss = _group_sumsq(p, bd_ref)
    xn = p * lax.rsqrt(ss * (1.0 / DIFF_QKDIM) + EPS) * gain
    rc, ra, rb = rc_ref[...], ra_ref[...], rb_ref[...]
    outs = []
    for c in range(p.shape[1] // LANES):
        xc = xn[:, c * LANES:(c + 1) * LANES]
        up = pltpu.roll(xc, LANES - ROT_DIM // 2, axis=1)
        dn = pltpu.roll(xc, ROT_DIM // 2, axis=1)
        outs.append(xc * rc + up * ra + dn * rb)
    out = jnp.concatenate(outs, axis=1)
    if scale != 1.0:
        out = out * scale
    return out.astype(BF16)


def _proj_kernel(x_ref, g_ref, w_ref, qg_ref, kg_ref, bd_ref, rc_ref, ra_ref, rb_ref,
                 qk_ref, vt_ref, u_ref, h_ref, val_ref):
    j = pl.program_id(1)

    @pl.when(j == 0)
    def _():
        h_ref[...] = _rms_rows(x_ref[...], g_ref[...]).astype(BF16)

    p = jnp.dot(h_ref[...], w_ref[...], preferred_element_type=F32)

    @pl.when(j == 0)
    def _():
        qk_ref[...] = _qk_norm_rope(p, qg_ref[...], bd_ref, rc_ref, ra_ref, rb_ref,
                                    math.log2(math.e) / math.sqrt(DIFF_QKDIM))

    @pl.when(j == 1)
    def _():
        qk_ref[...] = _qk_norm_rope(p, kg_ref[...], bd_ref, rc_ref, ra_ref, rb_ref, 1.0)

    @pl.when(j == 2)
    def _():
        vt_ref[...] = p.T.astype(BF16)

    @pl.when(j == 3)
    def _():
        val_ref[...] = p

    @pl.when(j == 4)
    def _():
        u_ref[...] = val_ref[...] * jax.nn.sigmoid(p)


def _proj(x, g, w_in, qg, kg, bd, rc, ra, rb, *, tm=512):
    s, d = x.shape
    w = 1024
    return pl.pallas_call(
        _proj_kernel,
        grid=(s // tm, 5),
        in_specs=[
            pl.BlockSpec((tm, d), lambda i, j: (i, 0)),
            pl.BlockSpec((1, d), lambda i, j: (0, 0)),
            pl.BlockSpec((d, w), lambda i, j: (0, j)),
            pl.BlockSpec((1, w), lambda i, j: (0, 0)),
            pl.BlockSpec((1, w), lambda i, j: (0, 0)),
            pl.BlockSpec((MXU_DIM, MXU_DIM), lambda i, j: (0, 0)),
            pl.BlockSpec((tm, LANES), lambda i, j: (i, 0)),
            pl.BlockSpec((tm, LANES), lambda i, j: (i, 0)),
            pl.BlockSpec((tm, LANES), lambda i, j: (i, 0)),
        ],
        out_specs=[
            pl.BlockSpec((None, tm, w), lambda i, j: (jnp.minimum(j, 1), i, 0)),
            pl.BlockSpec((w, tm), lambda i, j: (0, i)),
            pl.BlockSpec((tm, w), lambda i, j: (i, 0)),
        ],
        out_shape=[
            jax.ShapeDtypeStruct((2, s, w), BF16),
            jax.ShapeDtypeStruct((w, s), BF16),
            jax.ShapeDtypeStruct((s, w), F32),
        ],
        scratch_shapes=[pltpu.VMEM((tm, d), BF16), pltpu.VMEM((tm, w), F32)],
        compiler_params=pltpu.CompilerParams(
            dimension_semantics=("arbitrary", "arbitrary"), vmem_limit_bytes=VMEM_LIMIT),
        name="proj",
    )(x, g, w_in, qg, kg, bd, rc, ra, rb)


_NT = (((1,), (1,)), ((), ()))


_ONES_ROWS = 16


def _attn_kernel(q_ref, k_ref, vt_ref, lq1_ref, lk1_ref, lq2_ref, lk2_ref, hg_ref, o_ref,
                 acc1, acc2, m1, m2, *, t, ts):
    i = pl.program_id(1)
    q = q_ref[...]
    lane = lax.broadcasted_iota(jnp.int32, q.shape, 1)
    zero = jnp.zeros_like(q)
    qz = (jnp.where(lane < DIFF_QKDIM, q, zero), jnp.where(lane >= DIFF_QKDIM, q, zero))
    state = ((acc1, m1), (acc2, m2))
    ones = jnp.ones((_ONES_ROWS, t), BF16)

    def block(j, first):
        start = pl.multiple_of(j * t, t)
        chains = [(st, c) for st in range(t // ts) for c in range(2)]
        scores = []
        for st, c in chains:
            nk = (st + 1) * ts if first else t
            kb = k_ref[pl.ds(start, nk), :]
            s = lax.dot_general(kb, qz[c][st * ts:(st + 1) * ts, :], _NT,
                                preferred_element_type=F32)
            if first:
                krow = lax.broadcasted_iota(jnp.int32, s.shape, 0)
                qcol = lax.broadcasted_iota(jnp.int32, s.shape, 1) + st * ts
                s = jnp.where(krow <= qcol, s, NEG_BIG)
            scores.append(s)
        probs = []
        for (st, c), s in zip(chains, scores):
            acc, m = state[c]
            cols = slice(st * ts, (st + 1) * ts)
            if first:
                m_new = jnp.max(s, axis=0, keepdims=True)
                alpha = None
            else:
                m_old = m[:, cols]
                m_new = jnp.maximum(m_old, jnp.max(s, axis=0, keepdims=True))
                alpha = jnp.exp2(m_old - m_new)
            m[:, cols] = m_new
            probs.append((jnp.exp2(s - m_new).astype(BF16), alpha))
        for (st, c), (p, alpha) in zip(chains, probs):
            acc, m = state[c]
            cols = slice(st * ts, (st + 1) * ts)
            nk = p.shape[0]
            vb = jnp.concatenate([vt_ref[:, pl.ds(start, nk)], ones[:, :nk]], axis=0)
            pv = jnp.dot(vb, p, preferred_element_type=F32)
            acc[:, cols] = pv if first else alpha * acc[:, cols] + pv

    block(i, True)

    def body(j, carry):
        block(j, False)
        return carry

    lax.fori_loop(0, i, body, 0)

    lam = (jnp.exp(jnp.sum(lq1_ref[...] * lk1_ref[...], axis=-1, keepdims=True))
           - jnp.exp(jnp.sum(lq2_ref[...] * lk2_ref[...], axis=-1, keepdims=True))
           + LAMBDA_INIT)
    nv = DIFF_VDIM
    ot = (acc1[0:nv, :] / acc1[nv:nv + 1, :] - lam * (acc2[0:nv, :] / acc2[nv:nv + 1, :]))
    ms = jnp.mean(ot * ot, axis=0, keepdims=True)
    on = ot * lax.rsqrt(ms + EPS)
    o_ref[...] = (on.T * hg_ref[...] * (1.0 - LAMBDA_INIT)).astype(BF16)


def _attn(qk, vt, lq1, lk1, lq2, lk2, hg, *, t=1024, ts=256):
    _, s, w = qk.shape
    nh = w // DIFF_VDIM
    vec = lambda n: pl.BlockSpec((1, n), lambda h, i: (0, 0))
    return pl.pallas_call(
        functools.partial(_attn_kernel, t=t, ts=ts),
        grid=(nh, s // t),
        in_specs=[
            pl.BlockSpec((None, t, DIFF_VDIM), lambda h, i: (0, i, h)),
            pl.BlockSpec((None, s, DIFF_VDIM), lambda h, i: (1, 0, h)),
            pl.BlockSpec((DIFF_VDIM, s), lambda h, i: (h, 0)),
            vec(DIFF_QKDIM), vec(DIFF_QKDIM), vec(DIFF_QKDIM), vec(DIFF_QKDIM),
            vec(DIFF_VDIM),
        ],
        out_specs=pl.BlockSpec((t, DIFF_VDIM), lambda h, i: (i, h)),
        out_shape=jax.ShapeDtypeStruct((s, w), BF16),
        scratch_shapes=[
            pltpu.VMEM((DIFF_VDIM + _ONES_ROWS, t), F32), pltpu.VMEM((DIFF_VDIM + _ONES_ROWS, t), F32),
            pltpu.VMEM((1, t), F32), pltpu.VMEM((1, t), F32),
        ],
        compiler_params=pltpu.CompilerParams(
            dimension_semantics=("arbitrary", "arbitrary"), vmem_limit_bytes=VMEM_LIMIT),
        name="attn",
    )(qk, qk, vt, lq1, lk1, lq2, lk2, hg)


_HALO = 32
_SUB = 8
_ROWS = 16


def _conv_kernel(cur_ref, halo_ref, w_ref, b_ref, g_ref, beta_ref, o_ref, e_ref, y_ref, *, tc):
    i = pl.program_id(0)
    e_ref[0, 0:_HALO, :] = jnp.where(i > 0, halo_ref[...], 0.0)
    e_ref[0, _HALO:_HALO + tc, :] = cur_ref[...]
    span = tc + _HALO - _SUB
    for r in range(1, _SUB):
        e_ref[r, 0:span, :] = e_ref[0, r:r + span, :]

    first = _HALO - (CONV_K - 1)

    def chunk(c, carry):
        t0 = pl.multiple_of(c * _ROWS, _ROWS)
        acc = jnp.zeros((_ROWS, cur_ref.shape[1]), F32)
        for k in range(CONV_K):
            off = first + k
            a, r = off // _SUB, off % _SUB
            acc = acc + w_ref[k:k + 1, :] * e_ref[r, pl.ds(t0 + a * _SUB, _ROWS), :]
        y_ref[pl.ds(t0, _ROWS), :] = acc
        return carry

    lax.fori_loop(0, tc // _ROWS, chunk, 0)

    y = y_ref[...] + b_ref[...]
    mu = jnp.mean(y, axis=-1, keepdims=True)
    yc = y - mu
    var = jnp.mean(yc * yc, axis=-1, keepdims=True)
    yn = yc * lax.rsqrt(var + EPS) * g_ref[...] + beta_ref[...]
    o_ref[...] = (yn * jax.nn.sigmoid(yn)).astype(BF16)


def _conv(u, w, b, g, beta, *, tc=512):
    s, c = u.shape
    vec = pl.BlockSpec((1, c), lambda i: (0, 0))
    halo_blocks = tc // _HALO
    return pl.pallas_call(
        functools.partial(_conv_kernel, tc=tc),
        grid=(s // tc,),
        in_specs=[
            pl.BlockSpec((tc, c), lambda i: (i, 0)),
            pl.BlockSpec((_HALO, c), lambda i: (jnp.maximum(i * halo_blocks - 1, 0), 0)),
            pl.BlockSpec((CONV_K, c), lambda i: (0, 0)),
            vec, vec, vec,
        ],
        out_specs=pl.BlockSpec((tc, c), lambda i: (i, 0)),
        out_shape=jax.ShapeDtypeStruct((s, c), BF16),
        scratch_shapes=[pltpu.VMEM((_SUB, tc + _HALO, c), F32), pltpu.VMEM((tc, c), F32)],
        compiler_params=pltpu.CompilerParams(
            dimension_semantics=("arbitrary",), vmem_limit_bytes=VMEM_LIMIT),
        name="conv",
    )(u, u, w, b, g, beta)


def _memkv_kernel(mem_ref, g_ref, w_ref, kg_ref, k_ref, v_ref):
    h = _rms_rows(mem_ref[...], g_ref[...]).astype(BF16)
    kv = jnp.dot(h, w_ref[...], preferred_element_type=F32)
    width = k_ref.shape[1]
    for hd in range(XATTN_HEADS):
        sl = slice(hd * XATTN_HDIM, (hd + 1) * XATTN_HDIM)
        k_ref[:, sl] = _rms_rows(kv[:, sl], kg_ref[...]).astype(BF16)
    v_ref[...] = kv[:, width:].astype(BF16)


def _memkv(mem, g, w_kv, kg):
    n, _ = mem.shape
    width = w_kv.shape[1] // 2
    return pl.pallas_call(
        _memkv_kernel,
        out_shape=[jax.ShapeDtypeStruct((n, width), BF16), jax.ShapeDtypeStruct((n, width), BF16)],
        compiler_params=pltpu.CompilerParams(vmem_limit_bytes=VMEM_LIMIT),
        name="memkv",
    )(mem, g, w_kv, kg)


def _post_kernel(x_ref, od_ref, oc_ref, wo_ref, g_ref, wq_ref, qg_ref, km_ref, vm_ref, wxo_ref, o_ref):
    half = od_ref.shape[1]
    x2 = (x_ref[...]
          + jnp.dot(od_ref[...], wo_ref[0:half, :], preferred_element_type=F32)
          + jnp.dot(oc_ref[...], wo_ref[half:, :], preferred_element_type=F32))
    h = _rms_rows(x2, g_ref[...]).astype(BF16)
    q = jnp.dot(h, wq_ref[...], preferred_element_type=F32)
    scale = 1.0 / math.sqrt(XATTN_HDIM)
    outs = []
    for hd in range(XATTN_HEADS):
        sl = slice(hd * XATTN_HDIM, (hd + 1) * XATTN_HDIM)
        qn = _rms_rows(q[:, sl], qg_ref[...]).astype(BF16)
        sc = lax.dot_general(qn, km_ref[:, sl], _NT, preferred_element_type=F32) * scale
        sc = sc - jnp.max(sc, axis=-1, keepdims=True)
        p = jnp.exp(sc)
        p = p / jnp.sum(p, axis=-1, keepdims=True)
        outs.append(jnp.dot(p.astype(BF16), vm_ref[:, sl], preferred_element_type=F32))
    o = jnp.concatenate(outs, axis=1).astype(BF16)
    o_ref[...] = x2 + jnp.dot(o, wxo_ref[...], preferred_element_type=F32)


def _post(x, od, oc, w_out, g, wq, qg, km, vm, wxo, *, tm=512):
    s, d = x.shape
    half = od.shape[1]
    xw = wq.shape[1]
    n_mem = km.shape[0]
    const = lambda shape: pl.BlockSpec(shape, lambda i: (0, 0))
    return pl.pallas_call(
        _post_kernel,
        grid=(s // tm,),
        in_specs=[
            pl.BlockSpec((tm, d), lambda i: (i, 0)),
            pl.BlockSpec((tm, half), lambda i: (i, 0)),
            pl.BlockSpec((tm, half), lambda i: (i, 0)),
            const((2 * half, d)), const((1, d)), const((d, xw)), const((1, XATTN_HDIM)),
            const((n_mem, xw)), const((n_mem, xw)), const((xw, d)),
        ],
        out_specs=pl.BlockSpec((tm, d), lambda i: (i, 0)),
        out_shape=jax.ShapeDtypeStruct((s, d), F32),
        compiler_params=pltpu.CompilerParams(
            dimension_semantics=("arbitrary",), vmem_limit_bytes=VMEM_LIMIT),
        name="post",
    )(x, od, oc, w_out, g, wq, qg, km, vm, wxo)


def _rope_tables(positions):
    half = ROT_DIM // 2
    inv_freq = ROPE_THETA ** (-jnp.arange(0, ROT_DIM, 2, dtype=F32) / ROT_DIM)
    ang = positions.astype(F32)[:, None] * inv_freq
    cos, sin = jnp.cos(ang), jnp.sin(ang)
    n = positions.shape[0]
    rest = DIFF_QKDIM - ROT_DIM
    rc = jnp.concatenate([cos, cos, jnp.ones((n, rest), F32)], axis=1)
    ra = jnp.concatenate([-sin, jnp.zeros((n, half + rest), F32)], axis=1)
    rb = jnp.concatenate([jnp.zeros((n, half), F32), sin, jnp.zeros((n, rest), F32)], axis=1)
    rep = LANES // DIFF_QKDIM
    return jnp.tile(rc, (1, rep)), jnp.tile(ra, (1, rep)), jnp.tile(rb, (1, rep))


def kernel(x, mem, positions, ffn1_norm_g, ffn1_w_gate, ffn1_w_up, ffn1_w_down, mix_norm_g, w_in, q_norm_g, k_norm_g, lambda_q1, lambda_k1, lambda_q2, lambda_k2, diff_head_norm_g, conv_w, conv_b, conv_ln_g, conv_ln_b, w_out, xattn_norm_g, mem_norm_g, xattn_w_q, xattn_w_kv, xattn_q_norm_g, xattn_k_norm_g, xattn_w_o, ffn2_norm_g, ffn2_w_gate, ffn2_w_up, ffn2_w_down):
    b, s, d = x.shape
    assert b == 1 and ffn1_norm_g.shape[0] == 1
    bf = lambda w: w[0].astype(BF16)
    row = lambda v: v[0][None, :]

    rc, ra, rb = _rope_tables(positions[0])
    width = DIFF_HEADS * DIFF_VDIM
    qg = jnp.tile(q_norm_g[0], width // DIFF_QKDIM)[None, :]
    kg = jnp.tile(k_norm_g[0], width // DIFF_QKDIM)[None, :]
    blk = jnp.arange(MXU_DIM) // DIFF_QKDIM
    bd = (blk[:, None] == blk[None, :]).astype(BF16)

    x1 = _ffn(x[0], row(ffn1_norm_g), bf(ffn1_w_gate), bf(ffn1_w_up), bf(ffn1_w_down))

    qk, vt, u = _proj(x1, row(mix_norm_g), bf(w_in), qg, kg, bd, rc, ra, rb)
    o_diff = _attn(qk, vt, row(lambda_q1), row(lambda_k1), row(lambda_q2), row(lambda_k2),
                   row(diff_head_norm_g))
    o_conv = _conv(u, conv_w[0], row(conv_b), row(conv_ln_g), row(conv_ln_b))

    km, vm = _memkv(mem[0], row(mem_norm_g), bf(xattn_w_kv), row(xattn_k_norm_g))
    x3 = _post(x1, o_diff, o_conv, bf(w_out), row(xattn_norm_g), bf(xattn_w_q), row(xattn_q_norm_g),
               km, vm, bf(xattn_w_o))

    out = _ffn(x3, row(ffn2_norm_g), bf(ffn2_w_gate), bf(ffn2_w_up), bf(ffn2_w_down))
    return out[None]
```

```python
import functools
import math

import jax
import jax.numpy as jnp
from jax import lax
from jax.experimental import pallas as pl
from jax.experimental.pallas import tpu as pltpu

F32 = jnp.float32
BF16 = jnp.bfloat16

EPS = 1e-6
ROPE_THETA = 500000.0
DIFF_HEADS = 8
DIFF_VDIM = 128
DIFF_QKDIM = 64
ROT_DIM = 16
CONV_K = 31
XATTN_HEADS = 4
XATTN_HDIM = 128
LAMBDA_INIT = 0.8 - 0.6 * math.exp(0.0)

LANES = 128
MXU_DIM = 256
VMEM_LIMIT = 56 * 1024 * 1024

NEG_BIG = -1e30


def _rms_rows(x, g):
    ms = jnp.mean(x * x, axis=-1, keepdims=True)
    return x * lax.rsqrt(ms + EPS) * g


_FFN_ROWS = 256


def _ffn_kernel(x_ref, g_ref, wg_ref, wu_ref, wd_ref, o_ref, h_ref, *, n_ff_steps):
    j = pl.program_id(1)

    @pl.when(j == 0)
    def _():
        h_ref[...] = _rms_rows(x_ref[...], g_ref[...]).astype(BF16)
        o_ref[...] = jnp.zeros_like(o_ref)

    for r in range(o_ref.shape[0] // _FFN_ROWS):
        rows = slice(r * _FFN_ROWS, (r + 1) * _FFN_ROWS)
        h = h_ref[rows, :]
        a = jnp.dot(h, wg_ref[...], preferred_element_type=F32)
        b = jnp.dot(h, wu_ref[...], preferred_element_type=F32)
        act = (a * jax.nn.sigmoid(a) * b).astype(BF16)
        o_ref[rows, :] += jnp.dot(act, wd_ref[...], preferred_element_type=F32)

    @pl.when(j == n_ff_steps - 1)
    def _():
        o_ref[...] = x_ref[...] + 0.5 * o_ref[...]


def _ffn(x, g, wg, wu, wd, *, tm=1024, tf=512):
    s, d = x.shape
    f = wg.shape[1]
    n_ff_steps = f // tf
    kern = functools.partial(_ffn_kernel, n_ff_steps=n_ff_steps)
    return pl.pallas_call(
        kern,
        grid=(s // tm, n_ff_steps),
        in_specs=[
            pl.BlockSpec((tm, d), lambda i, j: (i, 0)),
            pl.BlockSpec((1, d), lambda i, j: (0, 0)),
            pl.BlockSpec((d, tf), lambda i, j: (0, j)),
            pl.BlockSpec((d, tf), lambda i, j: (0, j)),
            pl.BlockSpec((tf, d), lambda i, j: (j, 0)),
        ],
        out_specs=pl.BlockSpec((tm, d), lambda i, j: (i, 0)),
        out_shape=jax.ShapeDtypeStruct((s, d), F32),
        scratch_shapes=[pltpu.VMEM((tm, d), BF16)],
        compiler_params=pltpu.CompilerParams(
            dimension_semantics=("arbitrary", "arbitrary"), vmem_limit_bytes=VMEM_LIMIT),
        name="ffn",
    )(x, g, wg, wu, wd)


def _group_sumsq(p, bd_ref):
    p2 = p * p
    hi = p2.astype(BF16)
    lo = (p2 - hi.astype(F32)).astype(BF16)
    outs = []
    for c in range(p.shape[1] // MXU_DIM):
        sl = slice(c * MXU_DIM, (c + 1) * MXU_DIM)
        outs.append(jnp.dot(hi[:, sl], bd_ref[...], preferred_element_type=F32)
                    + jnp.dot(lo[:, sl], bd_ref[...], preferred_element_type=F32))
    return jnp.concatenate(outs, axis=1)


def _qk_norm_rope(p, gain, bd_ref, rc_ref, ra_ref, rb_ref, scale):
    ss = _group_sumsq(p, bd_ref)
    xn = p * lax.rsqrt(ss * (1.0 / DIFF_QKDIM) + EPS) * gain
    rc, ra, rb = rc_ref[...], ra_ref[...], rb_ref[...]
    outs = []
    for c in range(p.shape[1] // LANES):
        xc = xn[:, c * LANES:(c + 1) * LANES]
        up = pltpu.roll(xc, LANES - ROT_DIM // 2, axis=1)
        dn = pltpu.roll(xc, ROT_DIM // 2, axis=1)
        outs.append(xc * rc + up * ra + dn * rb)
    out = jnp.concatenate(outs, axis=1)
    if scale != 1.0:
        out = out * scale
    return out.astype(BF16)


_PROJ_ROWS = 256


def _proj_kernel(x_ref, g_ref, w_ref, qg_ref, kg_ref, bd_ref, rc_ref, ra_ref, rb_ref,
                 qk_ref, vt_ref, u_ref):
    w = u_ref.shape[1]
    q_scale = math.log2(math.e) / math.sqrt(DIFF_QKDIM)
    for r in range(x_ref.shape[0] // _PROJ_ROWS):
        rows = slice(r * _PROJ_ROWS, (r + 1) * _PROJ_ROWS)
        h = _rms_rows(x_ref[rows, :], g_ref[...]).astype(BF16)
        rope = (rc_ref.at[rows, :], ra_ref.at[rows, :], rb_ref.at[rows, :])
        col = lambda n: jnp.dot(h, w_ref[:, n * w:(n + 1) * w], preferred_element_type=F32)
        qk_ref[0, rows, :] = _qk_norm_rope(col(0), qg_ref[...], bd_ref, *rope, q_scale)
        qk_ref[1, rows, :] = _qk_norm_rope(col(1), kg_ref[...], bd_ref, *rope, 1.0)
        vt_ref[:, rows] = col(2).T.astype(BF16)
        u_ref[rows, :] = col(3) * jax.nn.sigmoid(col(4))


def _proj(x, g, w_in, qg, kg, bd, rc, ra, rb, *, tm=512):
    s, d = x.shape
    n_cols = w_in.shape[1]
    w = n_cols // 5
    once = pl.Buffered(1)
    return pl.pallas_call(
        _proj_kernel,
        grid=(s // tm,),
        in_specs=[
            pl.BlockSpec((tm, d), lambda i: (i, 0)),
            pl.BlockSpec((1, d), lambda i: (0, 0)),
            pl.BlockSpec((d, n_cols), lambda i: (0, 0), pipeline_mode=once),
            pl.BlockSpec((1, w), lambda i: (0, 0)),
            pl.BlockSpec((1, w), lambda i: (0, 0)),
            pl.BlockSpec((MXU_DIM, MXU_DIM), lambda i: (0, 0)),
            pl.BlockSpec((tm, LANES), lambda i: (i, 0)),
            pl.BlockSpec((tm, LANES), lambda i: (i, 0)),
            pl.BlockSpec((tm, LANES), lambda i: (i, 0)),
        ],
        out_specs=[
            pl.BlockSpec((2, tm, w), lambda i: (0, i, 0)),
            pl.BlockSpec((w, tm), lambda i: (0, i)),
            pl.BlockSpec((tm, w), lambda i: (i, 0)),
        ],
        out_shape=[
            jax.ShapeDtypeStruct((2, s, w), BF16),
            jax.ShapeDtypeStruct((w, s), BF16),
            jax.ShapeDtypeStruct((s, w), F32),
        ],
        compiler_params=pltpu.CompilerParams(
            dimension_semantics=("arbitrary",), vmem_limit_bytes=VMEM_LIMIT),
        name="proj",
    )(x, g, w_in, qg, kg, bd, rc, ra, rb)


_NT = (((1,), (1,)), ((), ()))


_ONES_ROWS = 16


def _attn_kernel(q_ref, k_ref, vt_ref, lq1_ref, lk1_ref, lq2_ref, lk2_ref, hg_ref, o_ref,
                 acc1, acc2, m1, m2, *, t, ts):
    i = pl.program_id(1)
    q = q_ref[...]
    lane = lax.broadcasted_iota(jnp.int32, q.shape, 1)
    zero = jnp.zeros_like(q)
    qz = (jnp.where(lane < DIFF_QKDIM, q, zero), jnp.where(lane >= DIFF_QKDIM, q, zero))
    state = ((acc1, m1), (acc2, m2))
    ones = jnp.ones((_ONES_ROWS, t), BF16)

    def block(j, first):
        start = pl.multiple_of(j * t, t)
        chains = [(st, c) for st in range(t // ts) for c in range(2)]
        scores = []
        for st, c in chains:
            nk = (st + 1) * ts if first else t
            kb = k_ref[pl.ds(start, nk), :]
            s = lax.dot_general(kb, qz[c][st * ts:(st + 1) * ts, :], _NT,
                                preferred_element_type=F32)
            if first:
                krow = lax.broadcasted_iota(jnp.int32, s.shape, 0)
                qcol = lax.broadcasted_iota(jnp.int32, s.shape, 1) + st * ts
                s = jnp.where(krow <= qcol, s, NEG_BIG)
            scores.append(s)
        probs = []
        for (st, c), s in zip(chains, scores):
            acc, m = state[c]
            cols = slice(st * ts, (st + 1) * ts)
            if first:
                m_new = jnp.max(s, axis=0, keepdims=True)
                alpha = None
            else:
                m_old = m[:, cols]
                m_new = jnp.maximum(m_old, jnp.max(s, axis=0, keepdims=True))
                alpha = jnp.exp2(m_old - m_new)
            m[:, cols] = m_new
            probs.append((jnp.exp2(s - m_new).astype(BF16), alpha))
        for (st, c), (p, alpha) in zip(chains, probs):
            acc, m = state[c]
            cols = slice(st * ts, (st + 1) * ts)
            nk = p.shape[0]
            vb = jnp.concatenate([vt_ref[:, pl.ds(start, nk)], ones[:, :nk]], axis=0)
            pv = jnp.dot(vb, p, preferred_element_type=F32)
            acc[:, cols] = pv if first else alpha * acc[:, cols] + pv

    block(i, True)

    def body(j, carry):
        block(j, False)
        return carry

    lax.fori_loop(0, i, body, 0)

    lam = (jnp.exp(jnp.sum(lq1_ref[...] * lk1_ref[...], axis=-1, keepdims=True))
           - jnp.exp(jnp.sum(lq2_ref[...] * lk2_ref[...], axis=-1, keepdims=True))
           + LAMBDA_INIT)
    nv = DIFF_VDIM
    ot = (acc1[0:nv, :] / acc1[nv:nv + 1, :] - lam * (acc2[0:nv, :] / acc2[nv:nv + 1, :]))
    ms = jnp.mean(ot * ot, axis=0, keepdims=True)
    on = ot * lax.rsqrt(ms + EPS)
    o_ref[...] = (on.T * hg_ref[...] * (1.0 - LAMBDA_INIT)).astype(BF16)


def _attn(qk, vt, lq1, lk1, lq2, lk2, hg, *, t=1024, ts=256):
    _, s, w = qk.shape
    nh = w // DIFF_VDIM
    vec = lambda n: pl.BlockSpec((1, n), lambda h, i: (0, 0))
    return pl.pallas_call(
        functools.partial(_attn_kernel, t=t, ts=ts),
        grid=(nh, s // t),
        in_specs=[
            pl.BlockSpec((None, t, DIFF_VDIM), lambda h, i: (0, i, h)),
            pl.BlockSpec((None, s, DIFF_VDIM), lambda h, i: (1, 0, h)),
            pl.BlockSpec((DIFF_VDIM, s), lambda h, i: (h, 0)),
            vec(DIFF_QKDIM), vec(DIFF_QKDIM), vec(DIFF_QKDIM), vec(DIFF_QKDIM),
            vec(DIFF_VDIM),
        ],
        out_specs=pl.BlockSpec((t, DIFF_VDIM), lambda h, i: (i, h)),
        out_shape=jax.ShapeDtypeStruct((s, w), BF16),
        scratch_shapes=[
            pltpu.VMEM((DIFF_VDIM + _ONES_ROWS, t), F32), pltpu.VMEM((DIFF_VDIM + _ONES_ROWS, t), F32),
            pltpu.VMEM((1, t), F32), pltpu.VMEM((1, t), F32),
        ],
        compiler_params=pltpu.CompilerParams(
            dimension_semantics=("arbitrary", "arbitrary"), vmem_limit_bytes=VMEM_LIMIT),
        name="attn",
    )(qk, qk, vt, lq1, lk1, lq2, lk2, hg)


_HALO = 32
_SUB = 8
_ROWS = 32


def _conv_kernel(cur_ref, halo_ref, w_ref, b_ref, g_ref, beta_ref, o_ref, e_ref, y_ref, *, tc):
    i = pl.program_id(0)
    e_ref[0, 0:_HALO, :] = jnp.where(i > 0, halo_ref[...], 0.0)
    e_ref[0, _HALO:_HALO + tc, :] = cur_ref[...]
    span = tc + _HALO - _SUB
    for r in range(1, _SUB):
        e_ref[r, 0:span, :] = e_ref[0, r:r + span, :]

    first = _HALO - (CONV_K - 1)

    def chunk(c, carry):
        t0 = pl.multiple_of(c * _ROWS, _ROWS)
        n_sub = _ROWS // _SUB
        acc = [jnp.zeros((_SUB, cur_ref.shape[1]), F32) for _ in range(n_sub)]
        for k in range(CONV_K):
            off = first + k
            a, r = off // _SUB, off % _SUB
            wk = w_ref[k]
            for q in range(n_sub):
                acc[q] = acc[q] + wk * e_ref[r, pl.ds(t0 + (a + q) * _SUB, _SUB), :]
        for q in range(n_sub):
            y_ref[pl.ds(t0 + q * _SUB, _SUB), :] = acc[q]
        return carry

    lax.fori_loop(0, tc // _ROWS, chunk, 0)

    y = y_ref[...] + b_ref[...]
    mu = jnp.mean(y, axis=-1, keepdims=True)
    yc = y - mu
    var = jnp.mean(yc * yc, axis=-1, keepdims=True)
    yn = yc * lax.rsqrt(var + EPS) * g_ref[...] + beta_ref[...]
    o_ref[...] = (yn * jax.nn.sigmoid(yn)).astype(BF16)


def _conv(u, w, b, g, beta, *, tc=512):
    s, c = u.shape
    vec = pl.BlockSpec((1, c), lambda i: (0, 0))
    halo_blocks = tc // _HALO
    return pl.pallas_call(
        functools.partial(_conv_kernel, tc=tc),
        grid=(s // tc,),
        in_specs=[
            pl.BlockSpec((tc, c), lambda i: (i, 0)),
            pl.BlockSpec((_HALO, c), lambda i: (jnp.maximum(i * halo_blocks - 1, 0), 0)),
            pl.BlockSpec((CONV_K, _SUB, c), lambda i: (0, 0, 0)),
            vec, vec, vec,
        ],
        out_specs=pl.BlockSpec((tc, c), lambda i: (i, 0)),
        out_shape=jax.ShapeDtypeStruct((s, c), BF16),
        scratch_shapes=[pltpu.VMEM((_SUB, tc + _HALO, c), F32), pltpu.VMEM((tc, c), F32)],
        compiler_params=pltpu.CompilerParams(
            dimension_semantics=("arbitrary",), vmem_limit_bytes=VMEM_LIMIT),
        name="conv",
    )(u, u, w, b, g, beta)


def _memkv_kernel(mem_ref, g_ref, w_ref, kg_ref, k_ref, v_ref):
    h = _rms_rows(mem_ref[...], g_ref[...]).astype(BF16)
    kv = jnp.dot(h, w_ref[...], preferred_element_type=F32)
    width = k_ref.shape[1]
    for hd in range(XATTN_HEADS):
        sl = slice(hd * XATTN_HDIM, (hd + 1) * XATTN_HDIM)
        k_ref[:, sl] = _rms_rows(kv[:, sl], kg_ref[...]).astype(BF16)
    v_ref[...] = kv[:, width:].astype(BF16)


def _memkv(mem, g, w_kv, kg):
    n, _ = mem.shape
    width = w_kv.shape[1] // 2
    return pl.pallas_call(
        _memkv_kernel,
        out_shape=[jax.ShapeDtypeStruct((n, width), BF16), jax.ShapeDtypeStruct((n, width), BF16)],
        compiler_params=pltpu.CompilerParams(vmem_limit_bytes=VMEM_LIMIT),
        name="memkv",
    )(mem, g, w_kv, kg)


def _post_kernel(x_ref, od_ref, oc_ref, wo_ref, g_ref, wq_ref, qg_ref, km_ref, vm_ref, wxo_ref, o_ref):
    half = od_ref.shape[1]
    x2 = (x_ref[...]
          + jnp.dot(od_ref[...], wo_ref[0:half, :], preferred_element_type=F32)
          + jnp.dot(oc_ref[...], wo_ref[half:, :], preferred_element_type=F32))
    h = _rms_rows(x2, g_ref[...]).astype(BF16)
    q = jnp.dot(h, wq_ref[...], preferred_element_type=F32)
    scale = 1.0 / math.sqrt(XATTN_HDIM)
    outs = []
    for hd in range(XATTN_HEADS):
        sl = slice(hd * XATTN_HDIM, (hd + 1) * XATTN_HDIM)
        qn = _rms_rows(q[:, sl], qg_ref[...]).astype(BF16)
        sc = lax.dot_general(qn, km_ref[:, sl], _NT, preferred_element_type=F32) * scale
        sc = sc - jnp.max(sc, axis=-1, keepdims=True)
        p = jnp.exp(sc)
        p = p / jnp.sum(p, axis=-1, keepdims=True)
        outs.append(jnp.dot(p.astype(BF16), vm_ref[:, sl], preferred_element_type=F32))
    o = jnp.concatenate(outs, axis=1).astype(BF16)
    o_ref[...] = x2 + jnp.dot(o, wxo_ref[...], preferred_element_type=F32)


def _post(x, od, oc, w_out, g, wq, qg, km, vm, wxo, *, tm=512):
    s, d = x.shape
    half = od.shape[1]
    xw = wq.shape[1]
    n_mem = km.shape[0]
    const = lambda shape: pl.BlockSpec(shape, lambda i: (0, 0))
    return pl.pallas_call(
        _post_kernel,
        grid=(s // tm,),
        in_specs=[
            pl.BlockSpec((tm, d), lambda i: (i, 0)),
            pl.BlockSpec((tm, half), lambda i: (i, 0)),
            pl.BlockSpec((tm, half), lambda i: (i, 0)),
            const((2 * half, d)), const((1, d)), const((d, xw)), const((1, XATTN_HDIM)),
            const((n_mem, xw)), const((n_mem, xw)), const((xw, d)),
        ],
        out_specs=pl.BlockSpec((tm, d), lambda i: (i, 0)),
        out_shape=jax.ShapeDtypeStruct((s, d), F32),
        compiler_params=pltpu.CompilerParams(
            dimension_semantics=("arbitrary",), vmem_limit_bytes=VMEM_LIMIT),
        name="post",
    )(x, od, oc, w_out, g, wq, qg, km, vm, wxo)


def _rope_tables(positions):
    half = ROT_DIM // 2
    inv_freq = ROPE_THETA ** (-jnp.arange(0, ROT_DIM, 2, dtype=F32) / ROT_DIM)
    ang = positions.astype(F32)[:, None] * inv_freq
    cos, sin = jnp.cos(ang), jnp.sin(ang)
    n = positions.shape[0]
    rest = DIFF_QKDIM - ROT_DIM
    rc = jnp.concatenate([cos, cos, jnp.ones((n, rest), F32)], axis=1)
    ra = jnp.concatenate([-sin, jnp.zeros((n, half + rest), F32)], axis=1)
    rb = jnp.concatenate([jnp.zeros((n, half), F32), sin, jnp.zeros((n, rest), F32)], axis=1)
    rep = LANES // DIFF_QKDIM
    return jnp.tile(rc, (1, rep)), jnp.tile(ra, (1, rep)), jnp.tile(rb, (1, rep))


def kernel(x, mem, positions, ffn1_norm_g, ffn1_w_gate, ffn1_w_up, ffn1_w_down, mix_norm_g, w_in, q_norm_g, k_norm_g, lambda_q1, lambda_k1, lambda_q2, lambda_k2, diff_head_norm_g, conv_w, conv_b, conv_ln_g, conv_ln_b, w_out, xattn_norm_g, mem_norm_g, xattn_w_q, xattn_w_kv, xattn_q_norm_g, xattn_k_norm_g, xattn_w_o, ffn2_norm_g, ffn2_w_gate, ffn2_w_up, ffn2_w_down):
    b, s, d = x.shape
    assert b == 1 and ffn1_norm_g.shape[0] == 1
    bf = lambda w: w[0].astype(BF16)
    row = lambda v: v[0][None, :]

    rc, ra, rb = _rope_tables(positions[0])
    width = DIFF_HEADS * DIFF_VDIM
    qg = jnp.tile(q_norm_g[0], width // DIFF_QKDIM)[None, :]
    kg = jnp.tile(k_norm_g[0], width // DIFF_QKDIM)[None, :]
    blk = jnp.arange(MXU_DIM) // DIFF_QKDIM
    bd = (blk[:, None] == blk[None, :]).astype(BF16)

    x1 = _ffn(x[0], row(ffn1_norm_g), bf(ffn1_w_gate), bf(ffn1_w_up), bf(ffn1_w_down))

    qk, vt, u = _proj(x1, row(mix_norm_g), bf(w_in), qg, kg, bd, rc, ra, rb)
    o_diff = _attn(qk, vt, row(lambda_q1), row(lambda_k1), row(lambda_q2), row(lambda_k2),
                   row(diff_head_norm_g))
    conv_taps = jnp.broadcast_to(conv_w[0][:, None, :], (CONV_K, _SUB, conv_w.shape[-1]))
    o_conv = _conv(u, conv_taps, row(conv_b), row(conv_ln_g), row(conv_ln_b))

    km, vm = _memkv(mem[0], row(mem_norm_g), bf(xattn_w_kv), row(xattn_k_norm_g))
    x3 = _post(x1, o_diff, o_conv, bf(w_out), row(xattn_norm_g), bf(xattn_w_q), row(xattn_q_norm_g),
               km, vm, bf(xattn_w_o))

    out = _ffn(x3, row(ffn2_norm_g), bf(ffn2_w_gate), bf(ffn2_w_up), bf(ffn2_w_down))
    return out[None]
```

```python
import functools
import math
from typing import Callable, NamedTuple

import jax
import jax.numpy as jnp
from jax import lax
from jax.experimental import pallas as pl
from jax.experimental.pallas import tpu as pltpu

F32 = jnp.float32
BF16 = jnp.bfloat16

EPS = 1e-6
ROPE_THETA = 500000.0
DIFF_HEADS = 8
DIFF_VDIM = 128
DIFF_QKDIM = 64
ROT_DIM = 16
CONV_K = 31
XATTN_HEADS = 4
XATTN_HDIM = 128
LAMBDA_INIT = 0.8 - 0.6 * math.exp(0.0)

LANES = 128
MXU_DIM = 256
VMEM_LIMIT = 56 * 1024 * 1024

NEG_BIG = -1e30


def _rms_rows(x, g):
    ms = jnp.mean(x * x, axis=-1, keepdims=True)
    return x * lax.rsqrt(ms + EPS) * g


class _HostedCast(NamedTuple):
    weight: jax.Array
    block: tuple
    index_map: Callable
    scale: float = 1.0


def _cast_blocks(srcs, dsts, scales):
    for src, dst, scale in zip(srcs, dsts, scales):
        v = src[...]
        dst[...] = (v if scale == 1.0 else v * scale).astype(BF16)


FFN_TM = 1024
FFN_TF = 512
_FFN_ROWS = 256


def _ffn_kernel(*refs, n_casts, cast_scales):
    x_ref, g_ref, wg_ref, wu_ref, wd_ref = refs[:5]
    cast_in = refs[5:5 + n_casts]
    o_ref = refs[5 + n_casts]
    cast_out = refs[6 + n_casts:6 + 2 * n_casts]
    h_ref = refs[6 + 2 * n_casts]
    j = pl.program_id(1)

    @pl.when(j == 0)
    def _():
        x = x_ref[...]
        h_ref[...] = _rms_rows(x, g_ref[...]).astype(BF16)
        o_ref[...] = x

    for r in range(o_ref.shape[0] // _FFN_ROWS):
        rows = slice(r * _FFN_ROWS, (r + 1) * _FFN_ROWS)
        h = h_ref[rows, :]
        a = jnp.dot(h, wg_ref[...], preferred_element_type=F32)
        b = jnp.dot(h, wu_ref[...], preferred_element_type=F32)
        act = (a * jax.nn.sigmoid(a) * b).astype(BF16)
        o_ref[rows, :] += jnp.dot(act, wd_ref[...], preferred_element_type=F32)

    _cast_blocks(cast_in, cast_out, cast_scales)


def _ffn(x, g, wg, wu, wd_half, casts=(), *, tm=FFN_TM, tf=FFN_TF):
    s, d = x.shape
    f = wg.shape[1]
    cast_specs = [pl.BlockSpec(c.block, c.index_map) for c in casts]
    outs = pl.pallas_call(
        functools.partial(_ffn_kernel, n_casts=len(casts), cast_scales=tuple(c.scale for c in casts)),
        grid=(s // tm, f // tf),
        in_specs=[
            pl.BlockSpec((tm, d), lambda i, j: (i, 0)),
            pl.BlockSpec((1, d), lambda i, j: (0, 0)),
            pl.BlockSpec((d, tf), lambda i, j: (0, j)),
            pl.BlockSpec((d, tf), lambda i, j: (0, j)),
            pl.BlockSpec((tf, d), lambda i, j: (j, 0)),
        ] + cast_specs,
        out_specs=[pl.BlockSpec((tm, d), lambda i, j: (i, 0))] + cast_specs,
        out_shape=[jax.ShapeDtypeStruct((s, d), F32)]
        + [jax.ShapeDtypeStruct(c.weight.shape, BF16) for c in casts],
        scratch_shapes=[pltpu.VMEM((tm, d), BF16)],
        compiler_params=pltpu.CompilerParams(
            dimension_semantics=("arbitrary", "arbitrary"), vmem_limit_bytes=VMEM_LIMIT),
        name="ffn",
    )(x, g, wg, wu, wd_half, *[c.weight for c in casts])
    return outs


def _group_sumsq(p, bd_ref):
    p2 = p * p
    hi = p2.astype(BF16)
    lo = (p2 - hi.astype(F32)).astype(BF16)
    outs = []
    for c in range(p.shape[1] // MXU_DIM):
        sl = slice(c * MXU_DIM, (c + 1) * MXU_DIM)
        outs.append(jnp.dot(hi[:, sl], bd_ref[...], preferred_element_type=F32)
                    + jnp.dot(lo[:, sl], bd_ref[...], preferred_element_type=F32))
    return jnp.concatenate(outs, axis=1)


def _qk_norm_rope(p, gain, bd_ref, rc_ref, ra_ref, rb_ref, scale):
    ss = _group_sumsq(p, bd_ref)
    xn = p * lax.rsqrt(ss * (1.0 / DIFF_QKDIM) + EPS) * gain
    rc, ra, rb = rc_ref[...], ra_ref[...], rb_ref[...]
    outs = []
    for c in range(p.shape[1] // LANES):
        xc = xn[:, c * LANES:(c + 1) * LANES]
        up = pltpu.roll(xc, LANES - ROT_DIM // 2, axis=1)
        dn = pltpu.roll(xc, ROT_DIM // 2, axis=1)
        outs.append(xc * rc + up * ra + dn * rb)
    out = jnp.concatenate(outs, axis=1)
    if scale != 1.0:
        out = out * scale
    return out.astype(BF16)


PROJ_TM = 512
_PROJ_ROWS = 256


def _proj_kernel(*refs, n_casts, cast_scales):
    x_ref, g_ref, w_ref, qg_ref, kg_ref, bd_ref, rc_ref, ra_ref, rb_ref = refs[:9]
    cast_in = refs[9:9 + n_casts]
    qk_ref, vt_ref, u_ref = refs[9 + n_casts:12 + n_casts]
    cast_out = refs[12 + n_casts:]
    _cast_blocks(cast_in, cast_out, cast_scales)
    w = u_ref.shape[1]
    q_scale = math.log2(math.e) / math.sqrt(DIFF_QKDIM)
    for r in range(x_ref.shape[0] // _PROJ_ROWS):
        rows = slice(r * _PROJ_ROWS, (r + 1) * _PROJ_ROWS)
        h = _rms_rows(x_ref[rows, :], g_ref[...]).astype(BF16)
        rope = (rc_ref.at[rows, :], ra_ref.at[rows, :], rb_ref.at[rows, :])
        col = lambda n: jnp.dot(h, w_ref[:, n * w:(n + 1) * w], preferred_element_type=F32)
        qk_ref[0, rows, :] = _qk_norm_rope(col(0), qg_ref[...], bd_ref, *rope, q_scale)
        qk_ref[1, rows, :] = _qk_norm_rope(col(1), kg_ref[...], bd_ref, *rope, 1.0)
        vt_ref[:, rows] = col(2).T.astype(BF16)
        u_ref[rows, :] = col(3) * jax.nn.sigmoid(col(4))


def _proj(x, g, w_in, qg, kg, bd, rc, ra, rb, casts=(), *, tm=PROJ_TM):
    s, d = x.shape
    n_cols = w_in.shape[1]
    w = n_cols // 5
    once = pl.Buffered(1)
    cast_specs = [pl.BlockSpec(c.block, c.index_map) for c in casts]
    return pl.pallas_call(
        functools.partial(_proj_kernel, n_casts=len(casts), cast_scales=tuple(c.scale for c in casts)),
        grid=(s // tm,),
        in_specs=[
            pl.BlockSpec((tm, d), lambda i: (i, 0)),
            pl.BlockSpec((1, d), lambda i: (0, 0)),
            pl.BlockSpec((d, n_cols), lambda i: (0, 0), pipeline_mode=once),
            pl.BlockSpec((1, w), lambda i: (0, 0)),
            pl.BlockSpec((1, w), lambda i: (0, 0)),
            pl.BlockSpec((MXU_DIM, MXU_DIM), lambda i: (0, 0)),
            pl.BlockSpec((tm, LANES), lambda i: (i, 0)),
            pl.BlockSpec((tm, LANES), lambda i: (i, 0)),
            pl.BlockSpec((tm, LANES), lambda i: (i, 0)),
        ] + cast_specs,
        out_specs=[
            pl.BlockSpec((2, tm, w), lambda i: (0, i, 0)),
            pl.BlockSpec((w, tm), lambda i: (0, i)),
            pl.BlockSpec((tm, w), lambda i: (i, 0)),
        ] + cast_specs,
        out_shape=[
            jax.ShapeDtypeStruct((2, s, w), BF16),
            jax.ShapeDtypeStruct((w, s), BF16),
            jax.ShapeDtypeStruct((s, w), F32),
        ] + [jax.ShapeDtypeStruct(c.weight.shape, BF16) for c in casts],
        compiler_params=pltpu.CompilerParams(
            dimension_semantics=("arbitrary",), vmem_limit_bytes=VMEM_LIMIT),
        name="proj",
    )(x, g, w_in, qg, kg, bd, rc, ra, rb, *[c.weight for c in casts])


_NT = (((1,), (1,)), ((), ()))


_ONES_ROWS = 16


def _attn_kernel(q_ref, k_ref, vt_ref, lq1_ref, lk1_ref, lq2_ref, lk2_ref, hg_ref, o_ref,
                 acc1, acc2, m1, m2, *, t, ts):
    i = pl.program_id(1)
    q = q_ref[...]
    lane = lax.broadcasted_iota(jnp.int32, q.shape, 1)
    zero = jnp.zeros_like(q)
    qz = (jnp.where(lane < DIFF_QKDIM, q, zero), jnp.where(lane >= DIFF_QKDIM, q, zero))
    state = ((acc1, m1), (acc2, m2))
    ones = jnp.ones((_ONES_ROWS, t), BF16)

    def block(j, first):
        start = pl.multiple_of(j * t, t)
        chains = [(st, c) for st in range(t // ts) for c in range(2)]
        scores = []
        for st, c in chains:
            nk = (st + 1) * ts if first else t
            kb = k_ref[pl.ds(start, nk), :]
            s = lax.dot_general(kb, qz[c][st * ts:(st + 1) * ts, :], _NT,
                                preferred_element_type=F32)
            if first:
                krow = lax.broadcasted_iota(jnp.int32, s.shape, 0)
                qcol = lax.broadcasted_iota(jnp.int32, s.shape, 1) + st * ts
                s = jnp.where(krow <= qcol, s, NEG_BIG)
            scores.append(s)
        probs = []
        for (st, c), s in zip(chains, scores):
            acc, m = state[c]
            cols = slice(st * ts, (st + 1) * ts)
            if first:
                m_new = jnp.max(s, axis=0, keepdims=True)
                alpha = None
            else:
                m_old = m[:, cols]
                m_new = jnp.maximum(m_old, jnp.max(s, axis=0, keepdims=True))
                alpha = jnp.exp2(m_old - m_new)
            m[:, cols] = m_new
            probs.append((jnp.exp2(s - m_new).astype(BF16), alpha))
        for (st, c), (p, alpha) in zip(chains, probs):
            acc, m = state[c]
            cols = slice(st * ts, (st + 1) * ts)
            nk = p.shape[0]
            vb = jnp.concatenate([vt_ref[:, pl.ds(start, nk)], ones[:, :nk]], axis=0)
            pv = jnp.dot(vb, p, preferred_element_type=F32)
            acc[:, cols] = pv if first else alpha * acc[:, cols] + pv

    block(i, True)

    def body(j, carry):
        block(j, False)
        return carry

    lax.fori_loop(0, i, body, 0)

    lam = (jnp.exp(jnp.sum(lq1_ref[...] * lk1_ref[...], axis=-1, keepdims=True))
           - jnp.exp(jnp.sum(lq2_ref[...] * lk2_ref[...], axis=-1, keepdims=True))
           + LAMBDA_INIT)
    nv = DIFF_VDIM
    ot = (acc1[0:nv, :] / acc1[nv:nv + 1, :] - lam * (acc2[0:nv, :] / acc2[nv:nv + 1, :]))
    ms = jnp.mean(ot * ot, axis=0, keepdims=True)
    on = ot * lax.rsqrt(ms + EPS)
    o_ref[...] = (on.T * hg_ref[...] * (1.0 - LAMBDA_INIT)).astype(BF16)


def _attn(qk, vt, lq1, lk1, lq2, lk2, hg, *, t=1024, ts=256):
    _, s, w = qk.shape
    nh = w // DIFF_VDIM
    vec = lambda n: pl.BlockSpec((1, n), lambda h, i: (0, 0))
    return pl.pallas_call(
        functools.partial(_attn_kernel, t=t, ts=ts),
        grid=(nh, s // t),
        in_specs=[
            pl.BlockSpec((None, t, DIFF_VDIM), lambda h, i: (0, i, h)),
            pl.BlockSpec((None, s, DIFF_VDIM), lambda h, i: (1, 0, h)),
            pl.BlockSpec((DIFF_VDIM, s), lambda h, i: (h, 0)),
            vec(DIFF_QKDIM), vec(DIFF_QKDIM), vec(DIFF_QKDIM), vec(DIFF_QKDIM),
            vec(DIFF_VDIM),
        ],
        out_specs=pl.BlockSpec((t, DIFF_VDIM), lambda h, i: (i, h)),
        out_shape=jax.ShapeDtypeStruct((s, w), BF16),
        scratch_shapes=[
            pltpu.VMEM((DIFF_VDIM + _ONES_ROWS, t), F32), pltpu.VMEM((DIFF_VDIM + _ONES_ROWS, t), F32),
            pltpu.VMEM((1, t), F32), pltpu.VMEM((1, t), F32),
        ],
        compiler_params=pltpu.CompilerParams(
            dimension_semantics=("arbitrary", "arbitrary"), vmem_limit_bytes=VMEM_LIMIT),
        name="attn",
    )(qk, qk, vt, lq1, lk1, lq2, lk2, hg)


_HALO = 32
_SUB = 8
_ROWS = 32


def _conv_kernel(cur_ref, halo_ref, w_ref, b_ref, g_ref, beta_ref, o_ref, e_ref, y_ref, *, tc):
    i = pl.program_id(0)
    e_ref[0, 0:_HALO, :] = jnp.where(i > 0, halo_ref[...], 0.0)
    e_ref[0, _HALO:_HALO + tc, :] = cur_ref[...]
    span = tc + _HALO - _SUB
    for r in range(1, _SUB):
        e_ref[r, 0:span, :] = e_ref[0, r:r + span, :]

    first = _HALO - (CONV_K - 1)

    def chunk(c, carry):
        t0 = pl.multiple_of(c * _ROWS, _ROWS)
        n_sub = _ROWS // _SUB
        acc = [jnp.zeros((_SUB, cur_ref.shape[1]), F32) for _ in range(n_sub)]
        for k in range(CONV_K):
            off = first + k
            a, r = off // _SUB, off % _SUB
            wk = w_ref[k]
            for q in range(n_sub):
                acc[q] = acc[q] + wk * e_ref[r, pl.ds(t0 + (a + q) * _SUB, _SUB), :]
        for q in range(n_sub):
            y_ref[pl.ds(t0 + q * _SUB, _SUB), :] = acc[q]
        return carry

    lax.fori_loop(0, tc // _ROWS, chunk, 0)

    y = y_ref[...] + b_ref[...]
    mu = jnp.mean(y, axis=-1, keepdims=True)
    yc = y - mu
    var = jnp.mean(yc * yc, axis=-1, keepdims=True)
    yn = yc * lax.rsqrt(var + EPS) * g_ref[...] + beta_ref[...]
    o_ref[...] = (yn * jax.nn.sigmoid(yn)).astype(BF16)


def _conv(u, w, b, g, beta, *, tc=512):
    s, c = u.shape
    vec = pl.BlockSpec((1, c), lambda i: (0, 0))
    halo_blocks = tc // _HALO
    return pl.pallas_call(
        functools.partial(_conv_kernel, tc=tc),
        grid=(s // tc,),
        in_specs=[
            pl.BlockSpec((tc, c), lambda i: (i, 0)),
            pl.BlockSpec((_HALO, c), lambda i: (jnp.maximum(i * halo_blocks - 1, 0), 0)),
            pl.BlockSpec((CONV_K, _SUB, c), lambda i: (0, 0, 0)),
            vec, vec, vec,
        ],
        out_specs=pl.BlockSpec((tc, c), lambda i: (i, 0)),
        out_shape=jax.ShapeDtypeStruct((s, c), BF16),
        scratch_shapes=[pltpu.VMEM((_SUB, tc + _HALO, c), F32), pltpu.VMEM((tc, c), F32)],
        compiler_params=pltpu.CompilerParams(
            dimension_semantics=("arbitrary",), vmem_limit_bytes=VMEM_LIMIT),
        name="conv",
    )(u, u, w, b, g, beta)


def _memkv_kernel(mem_ref, g_ref, w_ref, kg_ref, k_ref, v_ref):
    h = _rms_rows(mem_ref[...], g_ref[...]).astype(BF16)
    kv = jnp.dot(h, w_ref[...], preferred_element_type=F32)
    width = k_ref.shape[1]
    for hd in range(XATTN_HEADS):
        sl = slice(hd * XATTN_HDIM, (hd + 1) * XATTN_HDIM)
        k_ref[:, sl] = _rms_rows(kv[:, sl], kg_ref[...]).astype(BF16)
    v_ref[...] = kv[:, width:].astype(BF16)


def _memkv(mem, g, w_kv, kg):
    n, _ = mem.shape
    width = w_kv.shape[1] // 2
    return pl.pallas_call(
        _memkv_kernel,
        out_shape=[jax.ShapeDtypeStruct((n, width), BF16), jax.ShapeDtypeStruct((n, width), BF16)],
        compiler_params=pltpu.CompilerParams(vmem_limit_bytes=VMEM_LIMIT),
        name="memkv",
    )(mem, g, w_kv, kg)


def _post_kernel(x_ref, od_ref, oc_ref, wo_ref, g_ref, wq_ref, qg_ref, km_ref, vm_ref, wxo_ref, o_ref):
    half = od_ref.shape[1]
    x2 = (x_ref[...]
          + jnp.dot(od_ref[...], wo_ref[0:half, :], preferred_element_type=F32)
          + jnp.dot(oc_ref[...], wo_ref[half:, :], preferred_element_type=F32))
    h = _rms_rows(x2, g_ref[...]).astype(BF16)
    q = jnp.dot(h, wq_ref[...], preferred_element_type=F32)
    scale = 1.0 / math.sqrt(XATTN_HDIM)
    outs = []
    for hd in range(XATTN_HEADS):
        sl = slice(hd * XATTN_HDIM, (hd + 1) * XATTN_HDIM)
        qn = _rms_rows(q[:, sl], qg_ref[...]).astype(BF16)
        sc = lax.dot_general(qn, km_ref[:, sl], _NT, preferred_element_type=F32) * scale
        sc = sc - jnp.max(sc, axis=-1, keepdims=True)
        p = jnp.exp(sc)
        p = p / jnp.sum(p, axis=-1, keepdims=True)
        outs.append(jnp.dot(p.astype(BF16), vm_ref[:, sl], preferred_element_type=F32))
    o = jnp.concatenate(outs, axis=1).astype(BF16)
    o_ref[...] = x2 + jnp.dot(o, wxo_ref[...], preferred_element_type=F32)


def _post(x, od, oc, w_out, g, wq, qg, km, vm, wxo, *, tm=512):
    s, d = x.shape
    half = od.shape[1]
    xw = wq.shape[1]
    n_mem = km.shape[0]
    const = lambda shape: pl.BlockSpec(shape, lambda i: (0, 0))
    return pl.pallas_call(
        _post_kernel,
        grid=(s // tm,),
        in_specs=[
            pl.BlockSpec((tm, d), lambda i: (i, 0)),
            pl.BlockSpec((tm, half), lambda i: (i, 0)),
            pl.BlockSpec((tm, half), lambda i: (i, 0)),
            const((2 * half, d)), const((1, d)), const((d, xw)), const((1, XATTN_HDIM)),
            const((n_mem, xw)), const((n_mem, xw)), const((xw, d)),
        ],
        out_specs=pl.BlockSpec((tm, d), lambda i: (i, 0)),
        out_shape=jax.ShapeDtypeStruct((s, d), F32),
        compiler_params=pltpu.CompilerParams(
            dimension_semantics=("arbitrary",), vmem_limit_bytes=VMEM_LIMIT),
        name="post",
    )(x, od, oc, w_out, g, wq, qg, km, vm, wxo)


def _rope_tables(positions):
    half = ROT_DIM // 2
    inv_freq = ROPE_THETA ** (-jnp.arange(0, ROT_DIM, 2, dtype=F32) / ROT_DIM)
    ang = positions.astype(F32)[:, None] * inv_freq
    cos, sin = jnp.cos(ang), jnp.sin(ang)
    n = positions.shape[0]
    rest = DIFF_QKDIM - ROT_DIM
    rc = jnp.concatenate([cos, cos, jnp.ones((n, rest), F32)], axis=1)
    ra = jnp.concatenate([-sin, jnp.zeros((n, half + rest), F32)], axis=1)
    rb = jnp.concatenate([jnp.zeros((n, half), F32), sin, jnp.zeros((n, rest), F32)], axis=1)
    rep = LANES // DIFF_QKDIM
    return jnp.tile(rc, (1, rep)), jnp.tile(ra, (1, rep)), jnp.tile(rb, (1, rep))


def kernel(x, mem, positions, ffn1_norm_g, ffn1_w_gate, ffn1_w_up, ffn1_w_down, mix_norm_g, w_in, q_norm_g, k_norm_g, lambda_q1, lambda_k1, lambda_q2, lambda_k2, diff_head_norm_g, conv_w, conv_b, conv_ln_g, conv_ln_b, w_out, xattn_norm_g, mem_norm_g, xattn_w_q, xattn_w_kv, xattn_q_norm_g, xattn_k_norm_g, xattn_w_o, ffn2_norm_g, ffn2_w_gate, ffn2_w_up, ffn2_w_down):
    b, s, d = x.shape
    assert b == 1 and ffn1_norm_g.shape[0] == 1
    bf = lambda w: w[0].astype(BF16)
    row = lambda v: v[0][None, :]

    rc, ra, rb = _rope_tables(positions[0])
    width = DIFF_HEADS * DIFF_VDIM
    qg = jnp.tile(q_norm_g[0], width // DIFF_QKDIM)[None, :]
    kg = jnp.tile(k_norm_g[0], width // DIFF_QKDIM)[None, :]
    blk = jnp.arange(MXU_DIM) // DIFF_QKDIM
    bd = (blk[:, None] == blk[None, :]).astype(BF16)

    n_i, n_j = s // FFN_TM, ffn1_w_gate.shape[-1] // FFN_TF

    def by_rows(w, scale=1.0):
        r, c = w.shape
        last = c // FFN_TF - 1
        return _HostedCast(w, (r // n_i, FFN_TF), lambda i, j: (i, jnp.minimum(j, last)), scale)

    def by_cols(w, scale=1.0):
        r, c = w.shape
        return _HostedCast(w, (r // n_j, c // n_i), lambda i, j: (j, i), scale)

    x1, wg2, wu2, wd2_half, w_in_bf = _ffn(
        x[0], row(ffn1_norm_g), bf(ffn1_w_gate), bf(ffn1_w_up), (0.5 * ffn1_w_down[0]).astype(BF16),
        [by_rows(ffn2_w_gate[0]), by_rows(ffn2_w_up[0]), by_cols(ffn2_w_down[0], 0.5), by_rows(w_in[0])])

    n_p = s // PROJ_TM
    row_block = lambda w: _HostedCast(w, (w.shape[0] // n_p, w.shape[1]), lambda i: (i, 0))
    qk, vt, u, w_out_bf, wq_bf, wkv_bf, wo_bf = _proj(
        x1, row(mix_norm_g), w_in_bf, qg, kg, bd, rc, ra, rb,
        [row_block(w_out[0]), row_block(xattn_w_q[0]), row_block(xattn_w_kv[0]), row_block(xattn_w_o[0])])
    o_diff = _attn(qk, vt, row(lambda_q1), row(lambda_k1), row(lambda_q2), row(lambda_k2),
                   row(diff_head_norm_g))
    conv_taps = jnp.broadcast_to(conv_w[0][:, None, :], (CONV_K, _SUB, conv_w.shape[-1]))
    o_conv = _conv(u, conv_taps, row(conv_b), row(conv_ln_g), row(conv_ln_b))

    km, vm = _memkv(mem[0], row(mem_norm_g), wkv_bf, row(xattn_k_norm_g))
    x3 = _post(x1, o_diff, o_conv, w_out_bf, row(xattn_norm_g), wq_bf, row(xattn_q_norm_g), km, vm, wo_bf)

    out = _ffn(x3, row(ffn2_norm_g), wg2, wu2, wd2_half)[0]
    return out[None]
```

```python
import functools
import math
from typing import Callable, NamedTuple

import jax
import jax.numpy as jnp
from jax import lax
from jax.experimental import pallas as pl
from jax.experimental.pallas import tpu as pltpu

F32 = jnp.float32
BF16 = jnp.bfloat16

EPS = 1e-6
ROPE_THETA = 500000.0
DIFF_HEADS = 8
DIFF_VDIM = 128
DIFF_QKDIM = 64
ROT_DIM = 16
CONV_K = 31
XATTN_HEADS = 4
XATTN_HDIM = 128
LAMBDA_INIT = 0.8 - 0.6 * math.exp(0.0)

LANES = 128
MXU_DIM = 256
VMEM_LIMIT = 56 * 1024 * 1024

NEG_BIG = -1e30


def _rms_rows(x, g):
    ms = jnp.mean(x * x, axis=-1, keepdims=True)
    return x * lax.rsqrt(ms + EPS) * g


class _HostedCast(NamedTuple):
    weight: jax.Array
    block: tuple
    index_map: Callable
    scale: float = 1.0


def _cast_blocks(srcs, dsts, scales):
    for src, dst, scale in zip(srcs, dsts, scales):
        v = src[...]
        dst[...] = (v if scale == 1.0 else v * scale).astype(BF16)


FFN_TM = 1024
FFN_TF = 512
_FFN_ROWS = 256


def _ffn_kernel(*refs, n_casts, cast_scales):
    x_ref, g_ref, wg_ref, wu_ref, wd_ref = refs[:5]
    cast_in = refs[5:5 + n_casts]
    o_ref = refs[5 + n_casts]
    cast_out = refs[6 + n_casts:6 + 2 * n_casts]
    h_ref = refs[6 + 2 * n_casts]
    j = pl.program_id(1)

    @pl.when(j == 0)
    def _():
        x = x_ref[...]
        h_ref[...] = _rms_rows(x, g_ref[...]).astype(BF16)
        o_ref[...] = x

    for r in range(o_ref.shape[0] // _FFN_ROWS):
        rows = slice(r * _FFN_ROWS, (r + 1) * _FFN_ROWS)
        h = h_ref[rows, :]
        a = jnp.dot(h, wg_ref[...], preferred_element_type=F32)
        b = jnp.dot(h, wu_ref[...], preferred_element_type=F32)
        act = (a * jax.nn.sigmoid(a) * b).astype(BF16)
        o_ref[rows, :] += jnp.dot(act, wd_ref[...], preferred_element_type=F32)

    _cast_blocks(cast_in, cast_out, cast_scales)


def _ffn(x, g, wg, wu, wd_half, casts=(), *, tm=FFN_TM, tf=FFN_TF):
    s, d = x.shape
    f = wg.shape[1]
    cast_specs = [pl.BlockSpec(c.block, c.index_map) for c in casts]
    outs = pl.pallas_call(
        functools.partial(_ffn_kernel, n_casts=len(casts), cast_scales=tuple(c.scale for c in casts)),
        grid=(s // tm, f // tf),
        in_specs=[
            pl.BlockSpec((tm, d), lambda i, j: (i, 0)),
            pl.BlockSpec((1, d), lambda i, j: (0, 0)),
            pl.BlockSpec((d, tf), lambda i, j: (0, j)),
            pl.BlockSpec((d, tf), lambda i, j: (0, j)),
            pl.BlockSpec((tf, d), lambda i, j: (j, 0)),
        ] + cast_specs,
        out_specs=[pl.BlockSpec((tm, d), lambda i, j: (i, 0))] + cast_specs,
        out_shape=[jax.ShapeDtypeStruct((s, d), F32)]
        + [jax.ShapeDtypeStruct(c.weight.shape, BF16) for c in casts],
        scratch_shapes=[pltpu.VMEM((tm, d), BF16)],
        compiler_params=pltpu.CompilerParams(
            dimension_semantics=("arbitrary", "arbitrary"), vmem_limit_bytes=VMEM_LIMIT),
        name="ffn",
    )(x, g, wg, wu, wd_half, *[c.weight for c in casts])
    return outs


def _group_sumsq(p, bd_ref):
    p2 = p * p
    hi = p2.astype(BF16)
    lo = (p2 - hi.astype(F32)).astype(BF16)
    outs = []
    for c in range(p.shape[1] // MXU_DIM):
        sl = slice(c * MXU_DIM, (c + 1) * MXU_DIM)
        outs.append(jnp.dot(hi[:, sl], bd_ref[...], preferred_element_type=F32)
                    + jnp.dot(lo[:, sl], bd_ref[...], preferred_element_type=F32))
    return jnp.concatenate(outs, axis=1)


def _qk_norm_rope(p, gain, bd_ref, rc_ref, ra_ref, rb_ref, scale):
    ss = _group_sumsq(p, bd_ref)
    xn = p * lax.rsqrt(ss * (1.0 / DIFF_QKDIM) + EPS) * gain
    rc, ra, rb = rc_ref[...], ra_ref[...], rb_ref[...]
    outs = []
    for c in range(p.shape[1] // LANES):
        xc = xn[:, c * LANES:(c + 1) * LANES]
        up = pltpu.roll(xc, LANES - ROT_DIM // 2, axis=1)
        dn = pltpu.roll(xc, ROT_DIM // 2, axis=1)
        outs.append(xc * rc + up * ra + dn * rb)
    out = jnp.concatenate(outs, axis=1)
    if scale != 1.0:
        out = out * scale
    return out.astype(BF16)


PROJ_TM = 512
_PROJ_ROWS = 256


def _proj_kernel(*refs, n_casts, cast_scales):
    x_ref, g_ref, w_ref, qg_ref, kg_ref, bd_ref, rc_ref, ra_ref, rb_ref = refs[:9]
    cast_in = refs[9:9 + n_casts]
    qk_ref, vt_ref, u_ref = refs[9 + n_casts:12 + n_casts]
    cast_out = refs[12 + n_casts:]
    _cast_blocks(cast_in, cast_out, cast_scales)
    w = u_ref.shape[1]
    q_scale = math.log2(math.e) / math.sqrt(DIFF_QKDIM)
    for r in range(x_ref.shape[0] // _PROJ_ROWS):
        rows = slice(r * _PROJ_ROWS, (r + 1) * _PROJ_ROWS)
        h = _rms_rows(x_ref[rows, :], g_ref[...]).astype(BF16)
        rope = (rc_ref.at[rows, :], ra_ref.at[rows, :], rb_ref.at[rows, :])
        col = lambda n: jnp.dot(h, w_ref[:, n * w:(n + 1) * w], preferred_element_type=F32)
        qk_ref[0, rows, :] = _qk_norm_rope(col(0), qg_ref[...], bd_ref, *rope, q_scale)
        qk_ref[1, rows, :] = _qk_norm_rope(col(1), kg_ref[...], bd_ref, *rope, 1.0)
        vt_ref[:, rows] = col(2).T.astype(BF16)
        u_ref[rows, :] = col(3) * jax.nn.sigmoid(col(4))


def _proj(x, g, w_in, qg, kg, bd, rc, ra, rb, casts=(), *, tm=PROJ_TM):
    s, d = x.shape
    n_cols = w_in.shape[1]
    w = n_cols // 5
    once = pl.Buffered(1)
    cast_specs = [pl.BlockSpec(c.block, c.index_map) for c in casts]
    return pl.pallas_call(
        functools.partial(_proj_kernel, n_casts=len(casts), cast_scales=tuple(c.scale for c in casts)),
        grid=(s // tm,),
        in_specs=[
            pl.BlockSpec((tm, d), lambda i: (i, 0)),
            pl.BlockSpec((1, d), lambda i: (0, 0)),
            pl.BlockSpec((d, n_cols), lambda i: (0, 0), pipeline_mode=once),
            pl.BlockSpec((1, w), lambda i: (0, 0)),
            pl.BlockSpec((1, w), lambda i: (0, 0)),
            pl.BlockSpec((MXU_DIM, MXU_DIM), lambda i: (0, 0)),
            pl.BlockSpec((tm, LANES), lambda i: (i, 0)),
            pl.BlockSpec((tm, LANES), lambda i: (i, 0)),
            pl.BlockSpec((tm, LANES), lambda i: (i, 0)),
        ] + cast_specs,
        out_specs=[
            pl.BlockSpec((2, tm, w), lambda i: (0, i, 0)),
            pl.BlockSpec((w, tm), lambda i: (0, i)),
            pl.BlockSpec((tm, w), lambda i: (i, 0)),
        ] + cast_specs,
        out_shape=[
            jax.ShapeDtypeStruct((2, s, w), BF16),
            jax.ShapeDtypeStruct((w, s), BF16),
            jax.ShapeDtypeStruct((s, w), F32),
        ] + [jax.ShapeDtypeStruct(c.weight.shape, BF16) for c in casts],
        compiler_params=pltpu.CompilerParams(
            dimension_semantics=("arbitrary",), vmem_limit_bytes=VMEM_LIMIT),
        name="proj",
    )(x, g, w_in, qg, kg, bd, rc, ra, rb, *[c.weight for c in casts])


_NT = (((1,), (1,)), ((), ()))


_ONES_ROWS = 16
_CHAIN_GROUP = 2


def _attn_kernel(q_ref, k_ref, vt_ref, lq1_ref, lk1_ref, lq2_ref, lk2_ref, hg_ref, o_ref,
                 acc1, acc2, m1, m2, s_ref, *, t, ts):
    i = pl.program_id(1)
    q = q_ref[...]
    lane = lax.broadcasted_iota(jnp.int32, q.shape, 1)
    zero = jnp.zeros_like(q)
    qz = (jnp.where(lane < DIFF_QKDIM, q, zero), jnp.where(lane >= DIFF_QKDIM, q, zero))
    accs, ms = (acc1, acc2), (m1, m2)
    ones = jnp.ones((_ONES_ROWS, t), BF16)
    chains = [(st, c) for st in range(t // ts) for c in range(2)]

    def score(j, chain, diagonal):
        st, c = chain
        start = pl.multiple_of(j * t, t)
        nk = (st + 1) * ts if diagonal else t
        kb = k_ref[pl.ds(start, nk), :]
        s = lax.dot_general(kb, qz[c][st * ts:(st + 1) * ts, :], _NT, preferred_element_type=F32)
        if diagonal:
            krow = lax.broadcasted_iota(jnp.int32, s.shape, 0)
            qcol = lax.broadcasted_iota(jnp.int32, s.shape, 1) + st * ts
            s = jnp.where(krow <= qcol, s, NEG_BIG)
        s_ref[c, 0:nk, st * ts:(st + 1) * ts] = s
        return jnp.max(s, axis=0, keepdims=True), nk

    def softmax(s_max, chain, diagonal):
        st, c = chain
        cols = slice(st * ts, (st + 1) * ts)
        m_blk, nk = s_max
        if diagonal:
            m_new, alpha = m_blk, None
        else:
            m_old = ms[c][:, cols]
            m_new = jnp.maximum(m_old, m_blk)
            alpha = jnp.exp2(m_old - m_new)
        ms[c][:, cols] = m_new
        return jnp.exp2(s_ref[c, 0:nk, cols] - m_new).astype(BF16), alpha

    def pv(j, chain, prob):
        st, c = chain
        p, alpha = prob
        nk = p.shape[0]
        start = pl.multiple_of(j * t, t)
        cols = slice(st * ts, (st + 1) * ts)
        vb = jnp.concatenate([vt_ref[:, pl.ds(start, nk)], ones[:, :nk]], axis=0)
        acc_new = jnp.dot(vb, p, preferred_element_type=F32)
        accs[c][:, cols] = acc_new if alpha is None else alpha * accs[c][:, cols] + acc_new

    def block(j, diagonal):
        groups = [chains[g:g + _CHAIN_GROUP] for g in range(0, len(chains), _CHAIN_GROUP)]
        maxima = [score(j, chain, diagonal) for chain in groups[0]]
        for g, group in enumerate(groups):
            nxt = [score(j, chain, diagonal) for chain in groups[g + 1]] if g + 1 < len(groups) else []
            probs = [softmax(mx, chain, diagonal) for chain, mx in zip(group, maxima)]
            for chain, prob in zip(group, probs):
                pv(j, chain, prob)
            maxima = nxt

    block(i, True)

    def body(j, carry):
        block(j, False)
        return carry

    lax.fori_loop(0, i, body, 0)

    lam = (jnp.exp(jnp.sum(lq1_ref[...] * lk1_ref[...], axis=-1, keepdims=True))
           - jnp.exp(jnp.sum(lq2_ref[...] * lk2_ref[...], axis=-1, keepdims=True))
           + LAMBDA_INIT)
    nv = DIFF_VDIM
    ot = (acc1[0:nv, :] / acc1[nv:nv + 1, :] - lam * (acc2[0:nv, :] / acc2[nv:nv + 1, :]))
    mean_sq = jnp.mean(ot * ot, axis=0, keepdims=True)
    on = ot * lax.rsqrt(mean_sq + EPS)
    o_ref[...] = (on.T * hg_ref[...] * (1.0 - LAMBDA_INIT)).astype(BF16)


def _attn(qk, vt, lq1, lk1, lq2, lk2, hg, *, t=1024, ts=256):
    _, s, w = qk.shape
    nh = w // DIFF_VDIM
    vec = lambda n: pl.BlockSpec((1, n), lambda h, i: (0, 0))
    return pl.pallas_call(
        functools.partial(_attn_kernel, t=t, ts=ts),
        grid=(nh, s // t),
        in_specs=[
            pl.BlockSpec((None, t, DIFF_VDIM), lambda h, i: (0, i, h)),
            pl.BlockSpec((None, s, DIFF_VDIM), lambda h, i: (1, 0, h)),
            pl.BlockSpec((DIFF_VDIM, s), lambda h, i: (h, 0)),
            vec(DIFF_QKDIM), vec(DIFF_QKDIM), vec(DIFF_QKDIM), vec(DIFF_QKDIM),
            vec(DIFF_VDIM),
        ],
        out_specs=pl.BlockSpec((t, DIFF_VDIM), lambda h, i: (i, h)),
        out_shape=jax.ShapeDtypeStruct((s, w), BF16),
        scratch_shapes=[
            pltpu.VMEM((DIFF_VDIM + _ONES_ROWS, t), F32), pltpu.VMEM((DIFF_VDIM + _ONES_ROWS, t), F32),
            pltpu.VMEM((1, t), F32), pltpu.VMEM((1, t), F32),
            pltpu.VMEM((2, t, t), F32),
        ],
        compiler_params=pltpu.CompilerParams(
            dimension_semantics=("arbitrary", "arbitrary"), vmem_limit_bytes=VMEM_LIMIT),
        name="attn",
    )(qk, qk, vt, lq1, lk1, lq2, lk2, hg)


_HALO = 32
_SUB = 8
_ROWS = 32


def _conv_kernel(cur_ref, halo_ref, w_ref, b_ref, g_ref, beta_ref, o_ref, e_ref, y_ref, *, tc):
    i = pl.program_id(0)
    e_ref[0, 0:_HALO, :] = jnp.where(i > 0, halo_ref[...], 0.0)
    e_ref[0, _HALO:_HALO + tc, :] = cur_ref[...]
    span = tc + _HALO - _SUB
    for r in range(1, _SUB):
        e_ref[r, 0:span, :] = e_ref[0, r:r + span, :]

    first = _HALO - (CONV_K - 1)

    def chunk(c, carry):
        t0 = pl.multiple_of(c * _ROWS, _ROWS)
        n_sub = _ROWS // _SUB
        acc = [jnp.zeros((_SUB, cur_ref.shape[1]), F32) for _ in range(n_sub)]
        for k in range(CONV_K):
            off = first + k
            a, r = off // _SUB, off % _SUB
            wk = w_ref[k]
            for q in range(n_sub):
                acc[q] = acc[q] + wk * e_ref[r, pl.ds(t0 + (a + q) * _SUB, _SUB), :]
        for q in range(n_sub):
            y_ref[pl.ds(t0 + q * _SUB, _SUB), :] = acc[q]
        return carry

    lax.fori_loop(0, tc // _ROWS, chunk, 0)

    y = y_ref[...] + b_ref[...]
    mu = jnp.mean(y, axis=-1, keepdims=True)
    yc = y - mu
    var = jnp.mean(yc * yc, axis=-1, keepdims=True)
    yn = yc * lax.rsqrt(var + EPS) * g_ref[...] + beta_ref[...]
    o_ref[...] = (yn * jax.nn.sigmoid(yn)).astype(BF16)


def _conv(u, w, b, g, beta, *, tc=512):
    s, c = u.shape
    vec = pl.BlockSpec((1, c), lambda i: (0, 0))
    halo_blocks = tc // _HALO
    return pl.pallas_call(
        functools.partial(_conv_kernel, tc=tc),
        grid=(s // tc,),
        in_specs=[
            pl.BlockSpec((tc, c), lambda i: (i, 0)),
            pl.BlockSpec((_HALO, c), lambda i: (jnp.maximum(i * halo_blocks - 1, 0), 0)),
            pl.BlockSpec((CONV_K, _SUB, c), lambda i: (0, 0, 0)),
            vec, vec, vec,
        ],
        out_specs=pl.BlockSpec((tc, c), lambda i: (i, 0)),
        out_shape=jax.ShapeDtypeStruct((s, c), BF16),
        scratch_shapes=[pltpu.VMEM((_SUB, tc + _HALO, c), F32), pltpu.VMEM((tc, c), F32)],
        compiler_params=pltpu.CompilerParams(
            dimension_semantics=("arbitrary",), vmem_limit_bytes=VMEM_LIMIT),
        name="conv",
    )(u, u, w, b, g, beta)


def _memkv_kernel(mem_ref, g_ref, w_ref, kg_ref, k_ref, v_ref):
    h = _rms_rows(mem_ref[...], g_ref[...]).astype(BF16)
    kv = jnp.dot(h, w_ref[...], preferred_element_type=F32)
    width = k_ref.shape[1]
    for hd in range(XATTN_HEADS):
        sl = slice(hd * XATTN_HDIM, (hd + 1) * XATTN_HDIM)
        k_ref[:, sl] = _rms_rows(kv[:, sl], kg_ref[...]).astype(BF16)
    v_ref[...] = kv[:, width:].astype(BF16)


def _memkv(mem, g, w_kv, kg):
    n, _ = mem.shape
    width = w_kv.shape[1] // 2
    return pl.pallas_call(
        _memkv_kernel,
        out_shape=[jax.ShapeDtypeStruct((n, width), BF16), jax.ShapeDtypeStruct((n, width), BF16)],
        compiler_params=pltpu.CompilerParams(vmem_limit_bytes=VMEM_LIMIT),
        name="memkv",
    )(mem, g, w_kv, kg)


def _post_kernel(x_ref, od_ref, oc_ref, wo_ref, g_ref, wq_ref, qg_ref, km_ref, vm_ref, wxo_ref, o_ref):
    half = od_ref.shape[1]
    x2 = (x_ref[...]
          + jnp.dot(od_ref[...], wo_ref[0:half, :], preferred_element_type=F32)
          + jnp.dot(oc_ref[...], wo_ref[half:, :], preferred_element_type=F32))
    h = _rms_rows(x2, g_ref[...]).astype(BF16)
    q = jnp.dot(h, wq_ref[...], preferred_element_type=F32)
    scale = 1.0 / math.sqrt(XATTN_HDIM)
    outs = []
    for hd in range(XATTN_HEADS):
        sl = slice(hd * XATTN_HDIM, (hd + 1) * XATTN_HDIM)
        qn = _rms_rows(q[:, sl], qg_ref[...]).astype(BF16)
        sc = lax.dot_general(qn, km_ref[:, sl], _NT, preferred_element_type=F32) * scale
        sc = sc - jnp.max(sc, axis=-1, keepdims=True)
        p = jnp.exp(sc)
        p = p / jnp.sum(p, axis=-1, keepdims=True)
        outs.append(jnp.dot(p.astype(BF16), vm_ref[:, sl], preferred_element_type=F32))
    o = jnp.concatenate(outs, axis=1).astype(BF16)
    o_ref[...] = x2 + jnp.dot(o, wxo_ref[...], preferred_element_type=F32)


def _post(x, od, oc, w_out, g, wq, qg, km, vm, wxo, *, tm=512):
    s, d = x.shape
    half = od.shape[1]
    xw = wq.shape[1]
    n_mem = km.shape[0]
    const = lambda shape: pl.BlockSpec(shape, lambda i: (0, 0))
    return pl.pallas_call(
        _post_kernel,
        grid=(s // tm,),
        in_specs=[
            pl.BlockSpec((tm, d), lambda i: (i, 0)),
            pl.BlockSpec((tm, half), lambda i: (i, 0)),
            pl.BlockSpec((tm, half), lambda i: (i, 0)),
            const((2 * half, d)), const((1, d)), const((d, xw)), const((1, XATTN_HDIM)),
            const((n_mem, xw)), const((n_mem, xw)), const((xw, d)),
        ],
        out_specs=pl.BlockSpec((tm, d), lambda i: (i, 0)),
        out_shape=jax.ShapeDtypeStruct((s, d), F32),
        compiler_params=pltpu.CompilerParams(
            dimension_semantics=("arbitrary",), vmem_limit_bytes=VMEM_LIMIT),
        name="post",
    )(x, od, oc, w_out, g, wq, qg, km, vm, wxo)


def _rope_tables(positions):
    half = ROT_DIM // 2
    inv_freq = ROPE_THETA ** (-jnp.arange(0, ROT_DIM, 2, dtype=F32) / ROT_DIM)
    ang = positions.astype(F32)[:, None] * inv_freq
    cos, sin = jnp.cos(ang), jnp.sin(ang)
    n = positions.shape[0]
    rest = DIFF_QKDIM - ROT_DIM
    rc = jnp.concatenate([cos, cos, jnp.ones((n, rest), F32)], axis=1)
    ra = jnp.concatenate([-sin, jnp.zeros((n, half + rest), F32)], axis=1)
    rb = jnp.concatenate([jnp.zeros((n, half), F32), sin, jnp.zeros((n, rest), F32)], axis=1)
    rep = LANES // DIFF_QKDIM
    return jnp.tile(rc, (1, rep)), jnp.tile(ra, (1, rep)), jnp.tile(rb, (1, rep))


def kernel(x, mem, positions, ffn1_norm_g, ffn1_w_gate, ffn1_w_up, ffn1_w_down, mix_norm_g, w_in, q_norm_g, k_norm_g, lambda_q1, lambda_k1, lambda_q2, lambda_k2, diff_head_norm_g, conv_w, conv_b, conv_ln_g, conv_ln_b, w_out, xattn_norm_g, mem_norm_g, xattn_w_q, xattn_w_kv, xattn_q_norm_g, xattn_k_norm_g, xattn_w_o, ffn2_norm_g, ffn2_w_gate, ffn2_w_up, ffn2_w_down):
    b, s, d = x.shape
    assert b == 1 and ffn1_norm_g.shape[0] == 1
    bf = lambda w: w[0].astype(BF16)
    row = lambda v: v[0][None, :]

    rc, ra, rb = _rope_tables(positions[0])
    width = DIFF_HEADS * DIFF_VDIM
    qg = jnp.tile(q_norm_g[0], width // DIFF_QKDIM)[None, :]
    kg = jnp.tile(k_norm_g[0], width // DIFF_QKDIM)[None, :]
    blk = jnp.arange(MXU_DIM) // DIFF_QKDIM
    bd = (blk[:, None] == blk[None, :]).astype(BF16)

    n_i, n_j = s // FFN_TM, ffn1_w_gate.shape[-1] // FFN_TF

    def by_rows(w, scale=1.0):
        r, c = w.shape
        last = c // FFN_TF - 1
        return _HostedCast(w, (r // n_i, FFN_TF), lambda i, j: (i, jnp.minimum(j, last)), scale)

    def by_cols(w, scale=1.0):
        r, c = w.shape
        return _HostedCast(w, (r // n_j, c // n_i), lambda i, j: (j, i), scale)

    x1, wg2, wu2, wd2_half, w_in_bf = _ffn(
        x[0], row(ffn1_norm_g), bf(ffn1_w_gate), bf(ffn1_w_up), (0.5 * ffn1_w_down[0]).astype(BF16),
        [by_rows(ffn2_w_gate[0]), by_rows(ffn2_w_up[0]), by_cols(ffn2_w_down[0], 0.5), by_rows(w_in[0])])

    n_p = s // PROJ_TM
    row_block = lambda w: _HostedCast(w, (w.shape[0] // n_p, w.shape[1]), lambda i: (i, 0))
    qk, vt, u, w_out_bf, wq_bf, wkv_bf, wo_bf = _proj(
        x1, row(mix_norm_g), w_in_bf, qg, kg, bd, rc, ra, rb,
        [row_block(w_out[0]), row_block(xattn_w_q[0]), row_block(xattn_w_kv[0]), row_block(xattn_w_o[0])])
    o_diff = _attn(qk, vt, row(lambda_q1), row(lambda_k1), row(lambda_q2), row(lambda_k2),
                   row(diff_head_norm_g))
    conv_taps = jnp.broadcast_to(conv_w[0][:, None, :], (CONV_K, _SUB, conv_w.shape[-1]))
    o_conv = _conv(u, conv_taps, row(conv_b), row(conv_ln_g), row(conv_ln_b))

    km, vm = _memkv(mem[0], row(mem_norm_g), wkv_bf, row(xattn_k_norm_g))
    x3 = _post(x1, o_diff, o_conv, w_out_bf, row(xattn_norm_g), wq_bf, row(xattn_q_norm_g), km, vm, wo_bf)

    out = _ffn(x3, row(ffn2_norm_g), wg2, wu2, wd2_half)[0]
    return out[None]
```

```python
import functools
import math
from typing import Callable, NamedTuple

import jax
import jax.numpy as jnp
from jax import lax
from jax.experimental import pallas as pl
from jax.experimental.pallas import tpu as pltpu

F32 = jnp.float32
BF16 = jnp.bfloat16

EPS = 1e-6
ROPE_THETA = 500000.0
DIFF_HEADS = 8
DIFF_VDIM = 128
DIFF_QKDIM = 64
ROT_DIM = 16
CONV_K = 31
XATTN_HEADS = 4
XATTN_HDIM = 128
LAMBDA_INIT = 0.8 - 0.6 * math.exp(0.0)

LANES = 128
MXU_DIM = 256
VMEM_LIMIT = 56 * 1024 * 1024

NEG_BIG = -1e30


def _rms_rows(x, g):
    ms = jnp.mean(x * x, axis=-1, keepdims=True)
    return x * lax.rsqrt(ms + EPS) * g


class _HostedCast(NamedTuple):
    weight: jax.Array
    block: tuple
    index_map: Callable
    scale: float = 1.0


def _cast_blocks(srcs, dsts, scales):
    for src, dst, scale in zip(srcs, dsts, scales):
        v = src[...]
        dst[...] = (v if scale == 1.0 else v * scale).astype(BF16)


FFN_TM = 1024
FFN_TF = 512
_FFN_ROWS = 256


def _ffn_kernel(*refs, n_casts, cast_scales):
    x_ref, g_ref, wg_ref, wu_ref, wd_ref = refs[:5]
    cast_in = refs[5:5 + n_casts]
    o_ref = refs[5 + n_casts]
    cast_out = refs[6 + n_casts:6 + 2 * n_casts]
    h_ref = refs[6 + 2 * n_casts]
    j = pl.program_id(1)

    n_chunks = o_ref.shape[0] // _FFN_ROWS

    def body(first):
        def gate_up(r):
            rows = slice(r * _FFN_ROWS, (r + 1) * _FFN_ROWS)
            if first:
                h = _rms_rows(x_ref[rows, :], g_ref[...]).astype(BF16)
                h_ref[rows, :] = h
            else:
                h = h_ref[rows, :]
            a = jnp.dot(h, wg_ref[...], preferred_element_type=F32)
            b = jnp.dot(h, wu_ref[...], preferred_element_type=F32)
            return a, b

        ab = gate_up(0)
        for r in range(n_chunks):
            a, b = ab
            if r + 1 < n_chunks:
                ab = gate_up(r + 1)
            act = (a * jax.nn.sigmoid(a) * b).astype(BF16)
            rows = slice(r * _FFN_ROWS, (r + 1) * _FFN_ROWS)
            base = x_ref[rows, :] if first else o_ref[rows, :]
            o_ref[rows, :] = base + jnp.dot(act, wd_ref[...], preferred_element_type=F32)
        _cast_blocks(cast_in, cast_out, cast_scales)

    pl.when(j == 0)(lambda: body(True))
    pl.when(j > 0)(lambda: body(False))


def _ffn(x, g, wg, wu, wd_half, casts=(), *, tm=FFN_TM, tf=FFN_TF):
    s, d = x.shape
    f = wg.shape[1]
    cast_specs = [pl.BlockSpec(c.block, c.index_map) for c in casts]
    outs = pl.pallas_call(
        functools.partial(_ffn_kernel, n_casts=len(casts), cast_scales=tuple(c.scale for c in casts)),
        grid=(s // tm, f // tf),
        in_specs=[
            pl.BlockSpec((tm, d), lambda i, j: (i, 0)),
            pl.BlockSpec((1, d), lambda i, j: (0, 0)),
            pl.BlockSpec((d, tf), lambda i, j: (0, j)),
            pl.BlockSpec((d, tf), lambda i, j: (0, j)),
            pl.BlockSpec((tf, d), lambda i, j: (j, 0)),
        ] + cast_specs,
        out_specs=[pl.BlockSpec((tm, d), lambda i, j: (i, 0))] + cast_specs,
        out_shape=[jax.ShapeDtypeStruct((s, d), F32)]
        + [jax.ShapeDtypeStruct(c.weight.shape, BF16) for c in casts],
        scratch_shapes=[pltpu.VMEM((tm, d), BF16)],
        compiler_params=pltpu.CompilerParams(
            dimension_semantics=("arbitrary", "arbitrary"), vmem_limit_bytes=VMEM_LIMIT),
        name="ffn",
    )(x, g, wg, wu, wd_half, *[c.weight for c in casts])
    return outs


def _group_sumsq(p, bd_ref):
    p2 = p * p
    hi = p2.astype(BF16)
    lo = (p2 - hi.astype(F32)).astype(BF16)
    outs = []
    for c in range(p.shape[1] // MXU_DIM):
        sl = slice(c * MXU_DIM, (c + 1) * MXU_DIM)
        outs.append(jnp.dot(hi[:, sl], bd_ref[...], preferred_element_type=F32)
                    + jnp.dot(lo[:, sl], bd_ref[...], preferred_element_type=F32))
    return jnp.concatenate(outs, axis=1)


def _qk_norm_rope(p, gain, bd_ref, rc_ref, ra_ref, rb_ref, scale):
    ss = _group_sumsq(p, bd_ref)
    xn = p * lax.rsqrt(ss * (1.0 / DIFF_QKDIM) + EPS) * gain
    rc, ra, rb = rc_ref[...], ra_ref[...], rb_ref[...]
    outs = []
    for c in range(p.shape[1] // LANES):
        xc = xn[:, c * LANES:(c + 1) * LANES]
        up = pltpu.roll(xc, LANES - ROT_DIM // 2, axis=1)
        dn = pltpu.roll(xc, ROT_DIM // 2, axis=1)
        outs.append(xc * rc + up * ra + dn * rb)
    out = jnp.concatenate(outs, axis=1)
    if scale != 1.0:
        out = out * scale
    return out.astype(BF16)


PROJ_TM = 512
_PROJ_ROWS = 256


def _proj_kernel(*refs, n_casts, cast_scales):
    x_ref, g_ref, w_ref, qg_ref, kg_ref, bd_ref, rc_ref, ra_ref, rb_ref = refs[:9]
    cast_in = refs[9:9 + n_casts]
    qk_ref, vt_ref, u_ref = refs[9 + n_casts:12 + n_casts]
    cast_out = refs[12 + n_casts:]
    _cast_blocks(cast_in, cast_out, cast_scales)
    w = u_ref.shape[1]
    q_scale = math.log2(math.e) / math.sqrt(DIFF_QKDIM)
    for r in range(x_ref.shape[0] // _PROJ_ROWS):
        rows = slice(r * _PROJ_ROWS, (r + 1) * _PROJ_ROWS)
        h = _rms_rows(x_ref[rows, :], g_ref[...]).astype(BF16)
        rope = (rc_ref.at[rows, :], ra_ref.at[rows, :], rb_ref.at[rows, :])
        col = lambda n: jnp.dot(h, w_ref[:, n * w:(n + 1) * w], preferred_element_type=F32)
        qk_ref[0, rows, :] = _qk_norm_rope(col(0), qg_ref[...], bd_ref, *rope, q_scale)
        qk_ref[1, rows, :] = _qk_norm_rope(col(1), kg_ref[...], bd_ref, *rope, 1.0)
        vt_ref[:, rows] = col(2).T.astype(BF16)
        u_ref[rows, :] = col(3) * jax.nn.sigmoid(col(4))


def _proj(x, g, w_in, qg, kg, bd, rc, ra, rb, casts=(), *, tm=PROJ_TM):
    s, d = x.shape
    n_cols = w_in.shape[1]
    w = n_cols // 5
    once = pl.Buffered(1)
    cast_specs = [pl.BlockSpec(c.block, c.index_map) for c in casts]
    return pl.pallas_call(
        functools.partial(_proj_kernel, n_casts=len(casts), cast_scales=tuple(c.scale for c in casts)),
        grid=(s // tm,),
        in_specs=[
            pl.BlockSpec((tm, d), lambda i: (i, 0)),
            pl.BlockSpec((1, d), lambda i: (0, 0)),
            pl.BlockSpec((d, n_cols), lambda i: (0, 0), pipeline_mode=once),
            pl.BlockSpec((1, w), lambda i: (0, 0)),
            pl.BlockSpec((1, w), lambda i: (0, 0)),
            pl.BlockSpec((MXU_DIM, MXU_DIM), lambda i: (0, 0)),
            pl.BlockSpec((tm, LANES), lambda i: (i, 0)),
            pl.BlockSpec((tm, LANES), lambda i: (i, 0)),
            pl.BlockSpec((tm, LANES), lambda i: (i, 0)),
        ] + cast_specs,
        out_specs=[
            pl.BlockSpec((2, tm, w), lambda i: (0, i, 0)),
            pl.BlockSpec((w, tm), lambda i: (0, i)),
            pl.BlockSpec((tm, w), lambda i: (i, 0)),
        ] + cast_specs,
        out_shape=[
            jax.ShapeDtypeStruct((2, s, w), BF16),
            jax.ShapeDtypeStruct((w, s), BF16),
            jax.ShapeDtypeStruct((s, w), F32),
        ] + [jax.ShapeDtypeStruct(c.weight.shape, BF16) for c in casts],
        compiler_params=pltpu.CompilerParams(
            dimension_semantics=("arbitrary",), vmem_limit_bytes=VMEM_LIMIT),
        name="proj",
    )(x, g, w_in, qg, kg, bd, rc, ra, rb, *[c.weight for c in casts])


_NT = (((1,), (1,)), ((), ()))


_ONES_ROWS = 16
_CHAIN_GROUP = 2


def _attn_kernel(q_ref, k_ref, vt_ref, lq1_ref, lk1_ref, lq2_ref, lk2_ref, hg_ref, o_ref,
                 acc1, acc2, m1, m2, s_ref, *, t, ts):
    i = pl.program_id(1)
    q = q_ref[...]
    lane = lax.broadcasted_iota(jnp.int32, q.shape, 1)
    zero = jnp.zeros_like(q)
    qz = (jnp.where(lane < DIFF_QKDIM, q, zero), jnp.where(lane >= DIFF_QKDIM, q, zero))
    accs, ms = (acc1, acc2), (m1, m2)
    ones = jnp.ones((_ONES_ROWS, t), BF16)
    chains = [(st, c) for st in range(t // ts) for c in range(2)]

    def score(j, chain, diagonal):
        st, c = chain
        start = pl.multiple_of(j * t, t)
        nk = (st + 1) * ts if diagonal else t
        kb = k_ref[pl.ds(start, nk), :]
        s = lax.dot_general(kb, qz[c][st * ts:(st + 1) * ts, :], _NT, preferred_element_type=F32)
        if diagonal:
            krow = lax.broadcasted_iota(jnp.int32, s.shape, 0)
            qcol = lax.broadcasted_iota(jnp.int32, s.shape, 1) + st * ts
            s = jnp.where(krow <= qcol, s, NEG_BIG)
        s_ref[c, 0:nk, st * ts:(st + 1) * ts] = s
        return jnp.max(s, axis=0, keepdims=True), nk

    def softmax(s_max, chain, diagonal):
        st, c = chain
        cols = slice(st * ts, (st + 1) * ts)
        m_blk, nk = s_max
        if diagonal:
            m_new, alpha = m_blk, None
        else:
            m_old = ms[c][:, cols]
            m_new = jnp.maximum(m_old, m_blk)
            alpha = jnp.exp2(m_old - m_new)
        ms[c][:, cols] = m_new
        return jnp.exp2(s_ref[c, 0:nk, cols] - m_new).astype(BF16), alpha

    def pv(j, chain, prob):
        st, c = chain
        p, alpha = prob
        nk = p.shape[0]
        start = pl.multiple_of(j * t, t)
        cols = slice(st * ts, (st + 1) * ts)
        vb = jnp.concatenate([vt_ref[:, pl.ds(start, nk)], ones[:, :nk]], axis=0)
        acc_new = jnp.dot(vb, p, preferred_element_type=F32)
        accs[c][:, cols] = acc_new if alpha is None else alpha * accs[c][:, cols] + acc_new

    def blocks(js, diagonal):
        groups = [(j, chains[g:g + _CHAIN_GROUP]) for j in js for g in range(0, len(chains), _CHAIN_GROUP)]
        j0, first = groups[0]
        maxima = [score(j0, chain, diagonal) for chain in first]
        for g, (j, group) in enumerate(groups):
            nxt = []
            if g + 1 < len(groups):
                j_nxt, group_nxt = groups[g + 1]
                nxt = [score(j_nxt, chain, diagonal) for chain in group_nxt]
            probs = [softmax(mx, chain, diagonal) for chain, mx in zip(group, maxima)]
            for chain, prob in zip(group, probs):
                pv(j, chain, prob)
            maxima = nxt

    blocks([i], True)

    def body(n, carry):
        blocks([2 * n, 2 * n + 1], False)
        return carry

    lax.fori_loop(0, i // 2, body, 0)
    pl.when(i % 2 == 1)(lambda: blocks([i - 1], False))

    lam = (jnp.exp(jnp.sum(lq1_ref[...] * lk1_ref[...], axis=-1, keepdims=True))
           - jnp.exp(jnp.sum(lq2_ref[...] * lk2_ref[...], axis=-1, keepdims=True))
           + LAMBDA_INIT)
    nv = DIFF_VDIM
    ot = (acc1[0:nv, :] / acc1[nv:nv + 1, :] - lam * (acc2[0:nv, :] / acc2[nv:nv + 1, :]))
    mean_sq = jnp.mean(ot * ot, axis=0, keepdims=True)
    on = ot * lax.rsqrt(mean_sq + EPS)
    o_ref[...] = (on.T * hg_ref[...] * (1.0 - LAMBDA_INIT)).astype(BF16)


def _attn(qk, vt, lq1, lk1, lq2, lk2, hg, *, t=1024, ts=256):
    _, s, w = qk.shape
    nh = w // DIFF_VDIM
    vec = lambda n: pl.BlockSpec((1, n), lambda h, i: (0, 0))
    return pl.pallas_call(
        functools.partial(_attn_kernel, t=t, ts=ts),
        grid=(nh, s // t),
        in_specs=[
            pl.BlockSpec((None, t, DIFF_VDIM), lambda h, i: (0, i, h)),
            pl.BlockSpec((None, s, DIFF_VDIM), lambda h, i: (1, 0, h)),
            pl.BlockSpec((DIFF_VDIM, s), lambda h, i: (h, 0)),
            vec(DIFF_QKDIM), vec(DIFF_QKDIM), vec(DIFF_QKDIM), vec(DIFF_QKDIM),
            vec(DIFF_VDIM),
        ],
        out_specs=pl.BlockSpec((t, DIFF_VDIM), lambda h, i: (i, h)),
        out_shape=jax.ShapeDtypeStruct((s, w), BF16),
        scratch_shapes=[
            pltpu.VMEM((DIFF_VDIM + _ONES_ROWS, t), F32), pltpu.VMEM((DIFF_VDIM + _ONES_ROWS, t), F32),
            pltpu.VMEM((1, t), F32), pltpu.VMEM((1, t), F32),
            pltpu.VMEM((2, t, t), F32),
        ],
        compiler_params=pltpu.CompilerParams(
            dimension_semantics=("arbitrary", "arbitrary"), vmem_limit_bytes=VMEM_LIMIT),
        name="attn",
    )(qk, qk, vt, lq1, lk1, lq2, lk2, hg)


_HALO = 32
_SUB = 8
_ROWS = 32


def _conv_kernel(cur_ref, halo_ref, w_ref, b_ref, y_ref, e_ref, *, tc):
    i = pl.program_id(0)
    e_ref[0, 0:_HALO, :] = jnp.where(i > 0, halo_ref[...], 0.0)
    e_ref[0, _HALO:_HALO + tc, :] = cur_ref[...]
    span = tc + _HALO - _SUB
    for r in range(1, _SUB):
        e_ref[r, 0:span, :] = e_ref[0, r:r + span, :]

    first = _HALO - (CONV_K - 1)
    n_sub = _ROWS // _SUB

    def chunk(c, carry):
        t0 = pl.multiple_of(c * _ROWS, _ROWS)
        acc = [b_ref[...] for _ in range(n_sub)]
        for r in range(_SUB):
            taps_r = [(k, (first + k) // _SUB) for k in range(CONV_K) if (first + k) % _SUB == r]
            weights = {k: w_ref[k] for k, _ in taps_r}
            for idx in range(max(a for _, a in taps_r) + n_sub):
                uses = [(k, idx - a) for k, a in taps_r if 0 <= idx - a < n_sub]
                if not uses:
                    continue
                tile = e_ref[r, pl.ds(t0 + idx * _SUB, _SUB), :]
                for k, q in uses:
                    acc[q] = acc[q] + weights[k] * tile
        for q in range(n_sub):
            y_ref[pl.ds(t0 + q * _SUB, _SUB), :] = acc[q]
        return carry

    lax.fori_loop(0, tc // _ROWS, chunk, 0)


def _conv(u, taps, bias, *, tc=512):
    s, c = u.shape
    halo_blocks = tc // _HALO
    return pl.pallas_call(
        functools.partial(_conv_kernel, tc=tc),
        grid=(s // tc,),
        in_specs=[
            pl.BlockSpec((tc, c), lambda i: (i, 0)),
            pl.BlockSpec((_HALO, c), lambda i: (jnp.maximum(i * halo_blocks - 1, 0), 0)),
            pl.BlockSpec((CONV_K, _SUB, c), lambda i: (0, 0, 0)),
            pl.BlockSpec((_SUB, c), lambda i: (0, 0)),
        ],
        out_specs=pl.BlockSpec((tc, c), lambda i: (i, 0)),
        out_shape=jax.ShapeDtypeStruct((s, c), F32),
        scratch_shapes=[pltpu.VMEM((_SUB, tc + _HALO, c), F32)],
        compiler_params=pltpu.CompilerParams(
            dimension_semantics=("arbitrary",), vmem_limit_bytes=VMEM_LIMIT),
        name="conv",
    )(u, u, taps, bias)


def _memkv_kernel(mem_ref, g_ref, w_ref, kg_ref, k_ref, v_ref):
    h = _rms_rows(mem_ref[...], g_ref[...]).astype(BF16)
    kv = jnp.dot(h, w_ref[...], preferred_element_type=F32)
    width = k_ref.shape[1]
    for hd in range(XATTN_HEADS):
        sl = slice(hd * XATTN_HDIM, (hd + 1) * XATTN_HDIM)
        k_ref[:, sl] = _rms_rows(kv[:, sl], kg_ref[...]).astype(BF16)
    v_ref[...] = kv[:, width:].astype(BF16)


def _memkv(mem, g, w_kv, kg):
    n, _ = mem.shape
    width = w_kv.shape[1] // 2
    return pl.pallas_call(
        _memkv_kernel,
        out_shape=[jax.ShapeDtypeStruct((n, width), BF16), jax.ShapeDtypeStruct((n, width), BF16)],
        compiler_params=pltpu.CompilerParams(vmem_limit_bytes=VMEM_LIMIT),
        name="memkv",
    )(mem, g, w_kv, kg)


def _post_kernel(x_ref, od_ref, y_ref, lng_ref, lnb_ref, wo_ref, g_ref, wq_ref, qg_ref, km_ref, vm_ref,
                 wxo_ref, o_ref):
    half = od_ref.shape[1]
    x2 = x_ref[...] + jnp.dot(od_ref[...], wo_ref[0:half, :], preferred_element_type=F32)
    y = y_ref[...]
    yc = y - jnp.mean(y, axis=-1, keepdims=True)
    var = jnp.mean(yc * yc, axis=-1, keepdims=True)
    yn = yc * lax.rsqrt(var + EPS) * lng_ref[...] + lnb_ref[...]
    oc = (yn * jax.nn.sigmoid(yn)).astype(BF16)
    x2 = x2 + jnp.dot(oc, wo_ref[half:, :], preferred_element_type=F32)
    h = _rms_rows(x2, g_ref[...]).astype(BF16)
    q = jnp.dot(h, wq_ref[...], preferred_element_type=F32)
    scale = 1.0 / math.sqrt(XATTN_HDIM)
    outs = []
    for hd in range(XATTN_HEADS):
        sl = slice(hd * XATTN_HDIM, (hd + 1) * XATTN_HDIM)
        qn = _rms_rows(q[:, sl], qg_ref[...]).astype(BF16)
        sc = lax.dot_general(qn, km_ref[:, sl], _NT, preferred_element_type=F32) * scale
        sc = sc - jnp.max(sc, axis=-1, keepdims=True)
        p = jnp.exp(sc)
        p = p / jnp.sum(p, axis=-1, keepdims=True)
        outs.append(jnp.dot(p.astype(BF16), vm_ref[:, sl], preferred_element_type=F32))
    o = jnp.concatenate(outs, axis=1).astype(BF16)
    o_ref[...] = x2 + jnp.dot(o, wxo_ref[...], preferred_element_type=F32)


def _post(x, od, y_conv, ln_g, ln_b, w_out, g, wq, qg, km, vm, wxo, *, tm=512):
    s, d = x.shape
    half = od.shape[1]
    xw = wq.shape[1]
    n_mem = km.shape[0]
    const = lambda shape: pl.BlockSpec(shape, lambda i: (0, 0))
    return pl.pallas_call(
        _post_kernel,
        grid=(s // tm,),
        in_specs=[
            pl.BlockSpec((tm, d), lambda i: (i, 0)),
            pl.BlockSpec((tm, half), lambda i: (i, 0)),
            pl.BlockSpec((tm, half), lambda i: (i, 0)),
            const((1, half)), const((1, half)),
            const((2 * half, d)), const((1, d)), const((d, xw)), const((1, XATTN_HDIM)),
            const((n_mem, xw)), const((n_mem, xw)), const((xw, d)),
        ],
        out_specs=pl.BlockSpec((tm, d), lambda i: (i, 0)),
        out_shape=jax.ShapeDtypeStruct((s, d), F32),
        compiler_params=pltpu.CompilerParams(
            dimension_semantics=("arbitrary",), vmem_limit_bytes=VMEM_LIMIT),
        name="post",
    )(x, od, y_conv, ln_g, ln_b, w_out, g, wq, qg, km, vm, wxo)


def _rope_tables(positions):
    half = ROT_DIM // 2
    inv_freq = ROPE_THETA ** (-jnp.arange(0, ROT_DIM, 2, dtype=F32) / ROT_DIM)
    ang = positions.astype(F32)[:, None] * inv_freq
    cos, sin = jnp.cos(ang), jnp.sin(ang)
    n = positions.shape[0]
    rest = DIFF_QKDIM - ROT_DIM
    rc = jnp.concatenate([cos, cos, jnp.ones((n, rest), F32)], axis=1)
    ra = jnp.concatenate([-sin, jnp.zeros((n, half + rest), F32)], axis=1)
    rb = jnp.concatenate([jnp.zeros((n, half), F32), sin, jnp.zeros((n, rest), F32)], axis=1)
    rep = LANES // DIFF_QKDIM
    return jnp.tile(rc, (1, rep)), jnp.tile(ra, (1, rep)), jnp.tile(rb, (1, rep))


def kernel(x, mem, positions, ffn1_norm_g, ffn1_w_gate, ffn1_w_up, ffn1_w_down, mix_norm_g, w_in, q_norm_g, k_norm_g, lambda_q1, lambda_k1, lambda_q2, lambda_k2, diff_head_norm_g, conv_w, conv_b, conv_ln_g, conv_ln_b, w_out, xattn_norm_g, mem_norm_g, xattn_w_q, xattn_w_kv, xattn_q_norm_g, xattn_k_norm_g, xattn_w_o, ffn2_norm_g, ffn2_w_gate, ffn2_w_up, ffn2_w_down):
    b, s, d = x.shape
    assert b == 1 and ffn1_norm_g.shape[0] == 1
    bf = lambda w: w[0].astype(BF16)
    row = lambda v: v[0][None, :]

    rc, ra, rb = _rope_tables(positions[0])
    width = DIFF_HEADS * DIFF_VDIM
    qg = jnp.tile(q_norm_g[0], width // DIFF_QKDIM)[None, :]
    kg = jnp.tile(k_norm_g[0], width // DIFF_QKDIM)[None, :]
    blk = jnp.arange(MXU_DIM) // DIFF_QKDIM
    bd = (blk[:, None] == blk[None, :]).astype(BF16)

    n_i, n_j = s // FFN_TM, ffn1_w_gate.shape[-1] // FFN_TF

    def by_rows(w, scale=1.0):
        r, c = w.shape
        last = c // FFN_TF - 1
        return _HostedCast(w, (r // n_i, FFN_TF), lambda i, j: (i, jnp.minimum(j, last)), scale)

    def by_cols(w, scale=1.0):
        r, c = w.shape
        return _HostedCast(w, (r // n_j, c // n_i), lambda i, j: (j, i), scale)

    x1, wg2, wu2, wd2_half, w_in_bf = _ffn(
        x[0], row(ffn1_norm_g), bf(ffn1_w_gate), bf(ffn1_w_up), (0.5 * ffn1_w_down[0]).astype(BF16),
        [by_rows(ffn2_w_gate[0]), by_rows(ffn2_w_up[0]), by_cols(ffn2_w_down[0], 0.5), by_rows(w_in[0])])

    n_p = s // PROJ_TM
    row_block = lambda w: _HostedCast(w, (w.shape[0] // n_p, w.shape[1]), lambda i: (i, 0))
    qk, vt, u, w_out_bf, wq_bf, wkv_bf, wo_bf = _proj(
        x1, row(mix_norm_g), w_in_bf, qg, kg, bd, rc, ra, rb,
        [row_block(w_out[0]), row_block(xattn_w_q[0]), row_block(xattn_w_kv[0]), row_block(xattn_w_o[0])])
    o_diff = _attn(qk, vt, row(lambda_q1), row(lambda_k1), row(lambda_q2), row(lambda_k2),
                   row(diff_head_norm_g))
    conv_taps = jnp.broadcast_to(conv_w[0][:, None, :], (CONV_K, _SUB, conv_w.shape[-1]))
    conv_bias = jnp.broadcast_to(conv_b[0][None, :], (_SUB, conv_b.shape[-1]))
    y_conv = _conv(u, conv_taps, conv_bias)

    km, vm = _memkv(mem[0], row(mem_norm_g), wkv_bf, row(xattn_k_norm_g))
    x3 = _post(x1, o_diff, y_conv, row(conv_ln_g), row(conv_ln_b), w_out_bf, row(xattn_norm_g), wq_bf,
               row(xattn_q_norm_g), km, vm, wo_bf)

    out = _ffn(x3, row(ffn2_norm_g), wg2, wu2, wd2_half)[0]
    return out[None]
```

```python
import functools
import math
from typing import Callable, NamedTuple

import jax
import jax.numpy as jnp
from jax import lax
from jax.experimental import pallas as pl
from jax.experimental.pallas import tpu as pltpu

F32 = jnp.float32
BF16 = jnp.bfloat16

EPS = 1e-6
ROPE_THETA = 500000.0
DIFF_HEADS = 8
DIFF_VDIM = 128
DIFF_QKDIM = 64
ROT_DIM = 16
CONV_K = 31
XATTN_HEADS = 4
XATTN_HDIM = 128
LAMBDA_INIT = 0.8 - 0.6 * math.exp(0.0)

LANES = 128
MXU_DIM = 256
VMEM_LIMIT = 56 * 1024 * 1024

NEG_BIG = -1e30


def _rms_rows(x, g):
    ms = jnp.mean(x * x, axis=-1, keepdims=True)
    return x * lax.rsqrt(ms + EPS) * g


class _HostedCast(NamedTuple):
    weight: jax.Array
    block: tuple
    index_map: Callable
    scale: float = 1.0


def _cast_blocks(srcs, dsts, scales):
    for src, dst, scale in zip(srcs, dsts, scales):
        v = src[...]
        dst[...] = (v if scale == 1.0 else v * scale).astype(BF16)


FFN_TM = 1024
FFN_TF = 512
_FFN_ROWS = 256


def _ffn_kernel(*refs, n_casts, cast_scales):
    x_ref, g_ref, wg_ref, wu_ref, wd_ref = refs[:5]
    cast_in = refs[5:5 + n_casts]
    o_ref = refs[5 + n_casts]
    cast_out = refs[6 + n_casts:6 + 2 * n_casts]
    h_ref = refs[6 + 2 * n_casts]
    j = pl.program_id(1)

    n_chunks = o_ref.shape[0] // _FFN_ROWS

    def body(first):
        def gate_up(r):
            rows = slice(r * _FFN_ROWS, (r + 1) * _FFN_ROWS)
            if first:
                h = _rms_rows(x_ref[rows, :], g_ref[...]).astype(BF16)
                h_ref[rows, :] = h
            else:
                h = h_ref[rows, :]
            a = jnp.dot(h, wg_ref[...], preferred_element_type=F32)
            b = jnp.dot(h, wu_ref[...], preferred_element_type=F32)
            return a, b

        ab = gate_up(0)
        for r in range(n_chunks):
            a, b = ab
            if r + 1 < n_chunks:
                ab = gate_up(r + 1)
            act = (a * jax.nn.sigmoid(a) * b).astype(BF16)
            rows = slice(r * _FFN_ROWS, (r + 1) * _FFN_ROWS)
            base = x_ref[rows, :] if first else o_ref[rows, :]
            o_ref[rows, :] = base + jnp.dot(act, wd_ref[...], preferred_element_type=F32)
        _cast_blocks(cast_in, cast_out, cast_scales)

    pl.when(j == 0)(lambda: body(True))
    pl.when(j > 0)(lambda: body(False))


def _ffn(x, g, wg, wu, wd_half, casts=(), *, tm=FFN_TM, tf=FFN_TF):
    s, d = x.shape
    f = wg.shape[1]
    cast_specs = [pl.BlockSpec(c.block, c.index_map) for c in casts]
    outs = pl.pallas_call(
        functools.partial(_ffn_kernel, n_casts=len(casts), cast_scales=tuple(c.scale for c in casts)),
        grid=(s // tm, f // tf),
        in_specs=[
            pl.BlockSpec((tm, d), lambda i, j: (i, 0)),
            pl.BlockSpec((1, d), lambda i, j: (0, 0)),
            pl.BlockSpec((d, tf), lambda i, j: (0, j)),
            pl.BlockSpec((d, tf), lambda i, j: (0, j)),
            pl.BlockSpec((tf, d), lambda i, j: (j, 0)),
        ] + cast_specs,
        out_specs=[pl.BlockSpec((tm, d), lambda i, j: (i, 0))] + cast_specs,
        out_shape=[jax.ShapeDtypeStruct((s, d), F32)]
        + [jax.ShapeDtypeStruct(c.weight.shape, BF16) for c in casts],
        scratch_shapes=[pltpu.VMEM((tm, d), BF16)],
        compiler_params=pltpu.CompilerParams(
            dimension_semantics=("arbitrary", "arbitrary"), vmem_limit_bytes=VMEM_LIMIT),
        name="ffn",
    )(x, g, wg, wu, wd_half, *[c.weight for c in casts])
    return outs


def _group_sumsq(p, bd_ref):
    p2 = p * p
    hi = p2.astype(BF16)
    lo = (p2 - hi.astype(F32)).astype(BF16)
    outs = []
    for c in range(p.shape[1] // MXU_DIM):
        sl = slice(c * MXU_DIM, (c + 1) * MXU_DIM)
        outs.append(jnp.dot(hi[:, sl], bd_ref[...], preferred_element_type=F32)
                    + jnp.dot(lo[:, sl], bd_ref[...], preferred_element_type=F32))
    return jnp.concatenate(outs, axis=1)


def _qk_norm_rope(p, gain, bd_ref, rc_ref, ra_ref, rb_ref, scale):
    ss = _group_sumsq(p, bd_ref)
    xn = p * lax.rsqrt(ss * (1.0 / DIFF_QKDIM) + EPS) * gain
    rc, ra, rb = rc_ref[...], ra_ref[...], rb_ref[...]
    outs = []
    for c in range(p.shape[1] // LANES):
        xc = xn[:, c * LANES:(c + 1) * LANES]
        up = pltpu.roll(xc, LANES - ROT_DIM // 2, axis=1)
        dn = pltpu.roll(xc, ROT_DIM // 2, axis=1)
        outs.append(xc * rc + up * ra + dn * rb)
    out = jnp.concatenate(outs, axis=1)
    if scale != 1.0:
        out = out * scale
    return out.astype(BF16)


PROJ_TM = 512
_PROJ_ROWS = 256


def _proj_kernel(*refs, n_casts, cast_scales):
    x_ref, g_ref, w_ref, qg_ref, kg_ref, bd_ref, rc_ref, ra_ref, rb_ref = refs[:9]
    cast_in = refs[9:9 + n_casts]
    qk_ref, vt_ref, u_ref = refs[9 + n_casts:12 + n_casts]
    cast_out = refs[12 + n_casts:]
    _cast_blocks(cast_in, cast_out, cast_scales)
    w = u_ref.shape[1]
    q_scale = math.log2(math.e) / math.sqrt(DIFF_QKDIM)
    for r in range(x_ref.shape[0] // _PROJ_ROWS):
        rows = slice(r * _PROJ_ROWS, (r + 1) * _PROJ_ROWS)
        h = _rms_rows(x_ref[rows, :], g_ref[...]).astype(BF16)
        rope = (rc_ref.at[rows, :], ra_ref.at[rows, :], rb_ref.at[rows, :])
        col = lambda n: jnp.dot(h, w_ref[:, n * w:(n + 1) * w], preferred_element_type=F32)
        qk_ref[0, rows, :] = _qk_norm_rope(col(0), qg_ref[...], bd_ref, *rope, q_scale)
        qk_ref[1, rows, :] = _qk_norm_rope(col(1), kg_ref[...], bd_ref, *rope, 1.0)
        vt_ref[:, rows] = col(2).T.astype(BF16)
        u_ref[rows, :] = col(3) * jax.nn.sigmoid(col(4))


def _proj(x, g, w_in, qg, kg, bd, rc, ra, rb, casts=(), *, tm=PROJ_TM):
    s, d = x.shape
    n_cols = w_in.shape[1]
    w = n_cols // 5
    once = pl.Buffered(1)
    cast_specs = [pl.BlockSpec(c.block, c.index_map) for c in casts]
    return pl.pallas_call(
        functools.partial(_proj_kernel, n_casts=len(casts), cast_scales=tuple(c.scale for c in casts)),
        grid=(s // tm,),
        in_specs=[
            pl.BlockSpec((tm, d), lambda i: (i, 0)),
            pl.BlockSpec((1, d), lambda i: (0, 0)),
            pl.BlockSpec((d, n_cols), lambda i: (0, 0), pipeline_mode=once),
            pl.BlockSpec((1, w), lambda i: (0, 0)),
            pl.BlockSpec((1, w), lambda i: (0, 0)),
            pl.BlockSpec((MXU_DIM, MXU_DIM), lambda i: (0, 0)),
            pl.BlockSpec((tm, LANES), lambda i: (i, 0)),
            pl.BlockSpec((tm, LANES), lambda i: (i, 0)),
            pl.BlockSpec((tm, LANES), lambda i: (i, 0)),
        ] + cast_specs,
        out_specs=[
            pl.BlockSpec((2, tm, w), lambda i: (0, i, 0)),
            pl.BlockSpec((w, tm), lambda i: (0, i)),
            pl.BlockSpec((tm, w), lambda i: (i, 0)),
        ] + cast_specs,
        out_shape=[
            jax.ShapeDtypeStruct((2, s, w), BF16),
            jax.ShapeDtypeStruct((w, s), BF16),
            jax.ShapeDtypeStruct((s, w), F32),
        ] + [jax.ShapeDtypeStruct(c.weight.shape, BF16) for c in casts],
        compiler_params=pltpu.CompilerParams(
            dimension_semantics=("arbitrary",), vmem_limit_bytes=VMEM_LIMIT),
        name="proj",
    )(x, g, w_in, qg, kg, bd, rc, ra, rb, *[c.weight for c in casts])


_NT = (((1,), (1,)), ((), ()))


_ONES_ROWS = 16
_CHAIN_GROUP = 2


def _attn_kernel(q_ref, k_ref, vt_ref, lq1_ref, lk1_ref, lq2_ref, lk2_ref, hg_ref, o_ref,
                 acc1, acc2, m1, m2, s_ref, *, t, ts):
    i = pl.program_id(1)
    q = q_ref[...]
    lane = lax.broadcasted_iota(jnp.int32, q.shape, 1)
    zero = jnp.zeros_like(q)
    qz = (jnp.where(lane < DIFF_QKDIM, q, zero), jnp.where(lane >= DIFF_QKDIM, q, zero))
    accs, ms = (acc1, acc2), (m1, m2)
    ones = jnp.ones((_ONES_ROWS, t), BF16)
    chains = [(st, c) for st in range(t // ts) for c in range(2)]

    def score(j, chain, diagonal):
        st, c = chain
        start = pl.multiple_of(j * t, t)
        nk = (st + 1) * ts if diagonal else t
        kb = k_ref[pl.ds(start, nk), :]
        s = lax.dot_general(kb, qz[c][st * ts:(st + 1) * ts, :], _NT, preferred_element_type=F32)
        if diagonal:
            krow = lax.broadcasted_iota(jnp.int32, s.shape, 0)
            qcol = lax.broadcasted_iota(jnp.int32, s.shape, 1) + st * ts
            s = jnp.where(krow <= qcol, s, NEG_BIG)
        s_ref[2 * st + c, 0:nk, :] = s
        return jnp.max(s, axis=0, keepdims=True), nk

    def softmax(s_max, chain, diagonal):
        st, c = chain
        cols = slice(st * ts, (st + 1) * ts)
        m_blk, nk = s_max
        if diagonal:
            m_new, alpha = m_blk, None
        else:
            m_old = ms[c][:, cols]
            m_new = jnp.maximum(m_old, m_blk)
            alpha = jnp.exp2(m_old - m_new)
        ms[c][:, cols] = m_new
        return jnp.exp2(s_ref[2 * st + c, 0:nk, :] - m_new).astype(BF16), alpha

    def pv(j, chain, prob):
        st, c = chain
        p, alpha = prob
        nk = p.shape[0]
        start = pl.multiple_of(j * t, t)
        cols = slice(st * ts, (st + 1) * ts)
        vb = jnp.concatenate([vt_ref[:, pl.ds(start, nk)], ones[:, :nk]], axis=0)
        acc_new = jnp.dot(vb, p, preferred_element_type=F32)
        accs[c][st] = acc_new if alpha is None else alpha * accs[c][st] + acc_new

    def blocks(js, diagonal):
        groups = [(j, chains[g:g + _CHAIN_GROUP]) for j in js for g in range(0, len(chains), _CHAIN_GROUP)]
        j0, first = groups[0]
        maxima = [score(j0, chain, diagonal) for chain in first]
        for g, (j, group) in enumerate(groups):
            nxt = []
            if g + 1 < len(groups):
                j_nxt, group_nxt = groups[g + 1]
                nxt = [score(j_nxt, chain, diagonal) for chain in group_nxt]
            probs = [softmax(mx, chain, diagonal) for chain, mx in zip(group, maxima)]
            for chain, prob in zip(group, probs):
                pv(j, chain, prob)
            maxima = nxt

    blocks([i], True)

    def body(n, carry):
        blocks([2 * n, 2 * n + 1], False)
        return carry

    lax.fori_loop(0, i // 2, body, 0)
    pl.when(i % 2 == 1)(lambda: blocks([i - 1], False))

    lam = (jnp.exp(jnp.sum(lq1_ref[...] * lk1_ref[...], axis=-1, keepdims=True))
           - jnp.exp(jnp.sum(lq2_ref[...] * lk2_ref[...], axis=-1, keepdims=True))
           + LAMBDA_INIT)
    nv = DIFF_VDIM
    for st in range(t // ts):
        ot = (acc1[st, 0:nv, :] / acc1[st, nv:nv + 1, :]
              - lam * (acc2[st, 0:nv, :] / acc2[st, nv:nv + 1, :]))
        mean_sq = jnp.mean(ot * ot, axis=0, keepdims=True)
        on = ot * lax.rsqrt(mean_sq + EPS)
        o_ref[st * ts:(st + 1) * ts, :] = (on.T * hg_ref[...] * (1.0 - LAMBDA_INIT)).astype(BF16)


def _attn(qk, vt, lq1, lk1, lq2, lk2, hg, *, t=1024, ts=256):
    _, s, w = qk.shape
    nh = w // DIFF_VDIM
    vec = lambda n: pl.BlockSpec((1, n), lambda h, i: (0, 0))
    return pl.pallas_call(
        functools.partial(_attn_kernel, t=t, ts=ts),
        grid=(nh, s // t),
        in_specs=[
            pl.BlockSpec((None, t, DIFF_VDIM), lambda h, i: (0, i, h)),
            pl.BlockSpec((None, s, DIFF_VDIM), lambda h, i: (1, 0, h)),
            pl.BlockSpec((DIFF_VDIM, s), lambda h, i: (h, 0)),
            vec(DIFF_QKDIM), vec(DIFF_QKDIM), vec(DIFF_QKDIM), vec(DIFF_QKDIM),
            vec(DIFF_VDIM),
        ],
        out_specs=pl.BlockSpec((t, DIFF_VDIM), lambda h, i: (i, h)),
        out_shape=jax.ShapeDtypeStruct((s, w), BF16),
        scratch_shapes=[
            pltpu.VMEM((t // ts, DIFF_VDIM + _ONES_ROWS, ts), F32),
            pltpu.VMEM((t // ts, DIFF_VDIM + _ONES_ROWS, ts), F32),
            pltpu.VMEM((1, t), F32), pltpu.VMEM((1, t), F32),
            pltpu.VMEM((2 * (t // ts), t, ts), F32),
        ],
        compiler_params=pltpu.CompilerParams(
            dimension_semantics=("arbitrary", "arbitrary"), vmem_limit_bytes=VMEM_LIMIT),
        name="attn",
    )(qk, qk, vt, lq1, lk1, lq2, lk2, hg)


_HALO = 32
_SUB = 8
_ROWS = 32


def _conv_kernel(cur_ref, halo_ref, w_ref, b_ref, y_ref, e_ref, *, tc):
    i = pl.program_id(0)
    e_ref[0, 0:_HALO, :] = jnp.where(i > 0, halo_ref[...], 0.0)
    e_ref[0, _HALO:_HALO + tc, :] = cur_ref[...]
    span = tc + _HALO - _SUB
    for r in range(1, _SUB):
        e_ref[r, 0:span, :] = e_ref[0, r:r + span, :]

    first = _HALO - (CONV_K - 1)
    n_sub = _ROWS // _SUB

    def chunk(c, carry):
        t0 = pl.multiple_of(c * _ROWS, _ROWS)
        acc = [b_ref[...] for _ in range(n_sub)]
        for r in range(_SUB):
            taps_r = [(k, (first + k) // _SUB) for k in range(CONV_K) if (first + k) % _SUB == r]
            weights = {k: w_ref[k] for k, _ in taps_r}
            for idx in range(max(a for _, a in taps_r) + n_sub):
                uses = [(k, idx - a) for k, a in taps_r if 0 <= idx - a < n_sub]
                if not uses:
                    continue
                tile = e_ref[r, pl.ds(t0 + idx * _SUB, _SUB), :]
                for k, q in uses:
                    acc[q] = acc[q] + weights[k] * tile
        for q in range(n_sub):
            y_ref[pl.ds(t0 + q * _SUB, _SUB), :] = acc[q]
        return carry

    lax.fori_loop(0, tc // _ROWS, chunk, 0)


def _conv(u, taps, bias, *, tc=512):
    s, c = u.shape
    halo_blocks = tc // _HALO
    return pl.pallas_call(
        functools.partial(_conv_kernel, tc=tc),
        grid=(s // tc,),
        in_specs=[
            pl.BlockSpec((tc, c), lambda i: (i, 0)),
            pl.BlockSpec((_HALO, c), lambda i: (jnp.maximum(i * halo_blocks - 1, 0), 0)),
            pl.BlockSpec((CONV_K, _SUB, c), lambda i: (0, 0, 0)),
            pl.BlockSpec((_SUB, c), lambda i: (0, 0)),
        ],
        out_specs=pl.BlockSpec((tc, c), lambda i: (i, 0)),
        out_shape=jax.ShapeDtypeStruct((s, c), F32),
        scratch_shapes=[pltpu.VMEM((_SUB, tc + _HALO, c), F32)],
        compiler_params=pltpu.CompilerParams(
            dimension_semantics=("arbitrary",), vmem_limit_bytes=VMEM_LIMIT),
        name="conv",
    )(u, u, taps, bias)


def _memkv_kernel(mem_ref, g_ref, w_ref, kg_ref, k_ref, v_ref):
    h = _rms_rows(mem_ref[...], g_ref[...]).astype(BF16)
    kv = jnp.dot(h, w_ref[...], preferred_element_type=F32)
    width = k_ref.shape[1]
    for hd in range(XATTN_HEADS):
        sl = slice(hd * XATTN_HDIM, (hd + 1) * XATTN_HDIM)
        k_ref[:, sl] = _rms_rows(kv[:, sl], kg_ref[...]).astype(BF16)
    v_ref[...] = kv[:, width:].astype(BF16)


def _memkv(mem, g, w_kv, kg):
    n, _ = mem.shape
    width = w_kv.shape[1] // 2
    return pl.pallas_call(
        _memkv_kernel,
        out_shape=[jax.ShapeDtypeStruct((n, width), BF16), jax.ShapeDtypeStruct((n, width), BF16)],
        compiler_params=pltpu.CompilerParams(vmem_limit_bytes=VMEM_LIMIT),
        name="memkv",
    )(mem, g, w_kv, kg)


def _post_kernel(x_ref, od_ref, y_ref, lng_ref, lnb_ref, wo_ref, g_ref, wq_ref, qg_ref, km_ref, vm_ref,
                 wxo_ref, o_ref):
    half = od_ref.shape[1]
    x2 = x_ref[...] + jnp.dot(od_ref[...], wo_ref[0:half, :], preferred_element_type=F32)
    y = y_ref[...]
    yc = y - jnp.mean(y, axis=-1, keepdims=True)
    var = jnp.mean(yc * yc, axis=-1, keepdims=True)
    yn = yc * lax.rsqrt(var + EPS) * lng_ref[...] + lnb_ref[...]
    oc = (yn * jax.nn.sigmoid(yn)).astype(BF16)
    x2 = x2 + jnp.dot(oc, wo_ref[half:, :], preferred_element_type=F32)
    h = _rms_rows(x2, g_ref[...]).astype(BF16)
    q = jnp.dot(h, wq_ref[...], preferred_element_type=F32)
    scale = 1.0 / math.sqrt(XATTN_HDIM)
    outs = []
    for hd in range(XATTN_HEADS):
        sl = slice(hd * XATTN_HDIM, (hd + 1) * XATTN_HDIM)
        qn = _rms_rows(q[:, sl], qg_ref[...]).astype(BF16)
        sc = lax.dot_general(qn, km_ref[:, sl], _NT, preferred_element_type=F32) * scale
        sc = sc - jnp.max(sc, axis=-1, keepdims=True)
        p = jnp.exp(sc)
        p = p / jnp.sum(p, axis=-1, keepdims=True)
        outs.append(jnp.dot(p.astype(BF16), vm_ref[:, sl], preferred_element_type=F32))
    o = jnp.concatenate(outs, axis=1).astype(BF16)
    o_ref[...] = x2 + jnp.dot(o, wxo_ref[...], preferred_element_type=F32)


def _post(x, od, y_conv, ln_g, ln_b, w_out, g, wq, qg, km, vm, wxo, *, tm=512):
    s, d = x.shape
    half = od.shape[1]
    xw = wq.shape[1]
    n_mem = km.shape[0]
    const = lambda shape: pl.BlockSpec(shape, lambda i: (0, 0))
    return pl.pallas_call(
        _post_kernel,
        grid=(s // tm,),
        in_specs=[
            pl.BlockSpec((tm, d), lambda i: (i, 0)),
            pl.BlockSpec((tm, half), lambda i: (i, 0)),
            pl.BlockSpec((tm, half), lambda i: (i, 0)),
            const((1, half)), const((1, half)),
            const((2 * half, d)), const((1, d)), const((d, xw)), const((1, XATTN_HDIM)),
            const((n_mem, xw)), const((n_mem, xw)), const((xw, d)),
        ],
        out_specs=pl.BlockSpec((tm, d), lambda i: (i, 0)),
        out_shape=jax.ShapeDtypeStruct((s, d), F32),
        compiler_params=pltpu.CompilerParams(
            dimension_semantics=("arbitrary",), vmem_limit_bytes=VMEM_LIMIT),
        name="post",
    )(x, od, y_conv, ln_g, ln_b, w_out, g, wq, qg, km, vm, wxo)


def _rope_tables(positions):
    half = ROT_DIM // 2
    inv_freq = ROPE_THETA ** (-jnp.arange(0, ROT_DIM, 2, dtype=F32) / ROT_DIM)
    ang = positions.astype(F32)[:, None] * inv_freq
    cos, sin = jnp.cos(ang), jnp.sin(ang)
    n = positions.shape[0]
    rest = DIFF_QKDIM - ROT_DIM
    rc = jnp.concatenate([cos, cos, jnp.ones((n, rest), F32)], axis=1)
    ra = jnp.concatenate([-sin, jnp.zeros((n, half + rest), F32)], axis=1)
    rb = jnp.concatenate([jnp.zeros((n, half), F32), sin, jnp.zeros((n, rest), F32)], axis=1)
    rep = LANES // DIFF_QKDIM
    return jnp.tile(rc, (1, rep)), jnp.tile(ra, (1, rep)), jnp.tile(rb, (1, rep))


def kernel(x, mem, positions, ffn1_norm_g, ffn1_w_gate, ffn1_w_up, ffn1_w_down, mix_norm_g, w_in, q_norm_g, k_norm_g, lambda_q1, lambda_k1, lambda_q2, lambda_k2, diff_head_norm_g, conv_w, conv_b, conv_ln_g, conv_ln_b, w_out, xattn_norm_g, mem_norm_g, xattn_w_q, xattn_w_kv, xattn_q_norm_g, xattn_k_norm_g, xattn_w_o, ffn2_norm_g, ffn2_w_gate, ffn2_w_up, ffn2_w_down):
    b, s, d = x.shape
    assert b == 1 and ffn1_norm_g.shape[0] == 1
    bf = lambda w: w[0].astype(BF16)
    row = lambda v: v[0][None, :]

    rc, ra, rb = _rope_tables(positions[0])
    width = DIFF_HEADS * DIFF_VDIM
    qg = jnp.tile(q_norm_g[0], width // DIFF_QKDIM)[None, :]
    kg = jnp.tile(k_norm_g[0], width // DIFF_QKDIM)[None, :]
    blk = jnp.arange(MXU_DIM) // DIFF_QKDIM
    bd = (blk[:, None] == blk[None, :]).astype(BF16)

    n_i, n_j = s // FFN_TM, ffn1_w_gate.shape[-1] // FFN_TF

    def by_rows(w, scale=1.0):
        r, c = w.shape
        last = c // FFN_TF - 1
        return _HostedCast(w, (r // n_i, FFN_TF), lambda i, j: (i, jnp.minimum(j, last)), scale)

    def by_cols(w, scale=1.0):
        r, c = w.shape
        return _HostedCast(w, (r // n_j, c // n_i), lambda i, j: (j, i), scale)

    x1, wg2, wu2, wd2_half, w_in_bf = _ffn(
        x[0], row(ffn1_norm_g), bf(ffn1_w_gate), bf(ffn1_w_up), (0.5 * ffn1_w_down[0]).astype(BF16),
        [by_rows(ffn2_w_gate[0]), by_rows(ffn2_w_up[0]), by_cols(ffn2_w_down[0], 0.5), by_rows(w_in[0])])

    n_p = s // PROJ_TM
    row_block = lambda w: _HostedCast(w, (w.shape[0] // n_p, w.shape[1]), lambda i: (i, 0))
    qk, vt, u, w_out_bf, wq_bf, wkv_bf, wo_bf = _proj(
        x1, row(mix_norm_g), w_in_bf, qg, kg, bd, rc, ra, rb,
        [row_block(w_out[0]), row_block(xattn_w_q[0]), row_block(xattn_w_kv[0]), row_block(xattn_w_o[0])])
    o_diff = _attn(qk, vt, row(lambda_q1), row(lambda_k1), row(lambda_q2), row(lambda_k2),
                   row(diff_head_norm_g))
    conv_taps = jnp.broadcast_to(conv_w[0][:, None, :], (CONV_K, _SUB, conv_w.shape[-1]))
    conv_bias = jnp.broadcast_to(conv_b[0][None, :], (_SUB, conv_b.shape[-1]))
    y_conv = _conv(u, conv_taps, conv_bias)

    km, vm = _memkv(mem[0], row(mem_norm_g), wkv_bf, row(xattn_k_norm_g))
    x3 = _post(x1, o_diff, y_conv, row(conv_ln_g), row(conv_ln_b), w_out_bf, row(xattn_norm_g), wq_bf,
               row(xattn_q_norm_g), km, vm, wo_bf)

    out = _ffn(x3, row(ffn2_norm_g), wg2, wu2, wd2_half)[0]
    return out[None]
```

```python
import functools
import math
from typing import Callable, NamedTuple

import jax
import jax.numpy as jnp
from jax import lax
from jax.experimental import pallas as pl
from jax.experimental.pallas import tpu as pltpu

F32 = jnp.float32
BF16 = jnp.bfloat16

EPS = 1e-6
ROPE_THETA = 500000.0
DIFF_HEADS = 8
DIFF_VDIM = 128
DIFF_QKDIM = 64
ROT_DIM = 16
CONV_K = 31
XATTN_HEADS = 4
XATTN_HDIM = 128
LAMBDA_INIT = 0.8 - 0.6 * math.exp(0.0)

LANES = 128
MXU_DIM = 256
VMEM_LIMIT = 56 * 1024 * 1024

NEG_BIG = -1e30


def _rms_rows(x, g):
    ms = jnp.mean(x * x, axis=-1, keepdims=True)
    return x * lax.rsqrt(ms + EPS) * g


class _HostedCast(NamedTuple):
    weight: jax.Array
    block: tuple
    index_map: Callable
    scale: float = 1.0


def _cast_blocks(srcs, dsts, scales):
    for src, dst, scale in zip(srcs, dsts, scales):
        v = src[...]
        dst[...] = (v if scale == 1.0 else v * scale).astype(BF16)


FFN_TM = 1024
FFN_TF = 512
_FFN_ROWS = 256


def _ffn_kernel(*refs, n_casts, cast_scales):
    x_ref, g_ref, wg_ref, wu_ref, wd_ref = refs[:5]
    cast_in = refs[5:5 + n_casts]
    o_ref = refs[5 + n_casts]
    cast_out = refs[6 + n_casts:6 + 2 * n_casts]
    h_ref = refs[6 + 2 * n_casts]
    j = pl.program_id(1)

    n_chunks = o_ref.shape[0] // _FFN_ROWS

    def body(first):
        def gate_up(r):
            rows = slice(r * _FFN_ROWS, (r + 1) * _FFN_ROWS)
            if first:
                h = _rms_rows(x_ref[rows, :], g_ref[...]).astype(BF16)
                h_ref[rows, :] = h
            else:
                h = h_ref[rows, :]
            a = jnp.dot(h, wg_ref[...], preferred_element_type=F32)
            b = jnp.dot(h, wu_ref[...], preferred_element_type=F32)
            return a, b

        ab = gate_up(0)
        for r in range(n_chunks):
            a, b = ab
            if r + 1 < n_chunks:
                ab = gate_up(r + 1)
            act = (a * jax.nn.sigmoid(a) * b).astype(BF16)
            rows = slice(r * _FFN_ROWS, (r + 1) * _FFN_ROWS)
            base = x_ref[rows, :] if first else o_ref[rows, :]
            o_ref[rows, :] = base + jnp.dot(act, wd_ref[...], preferred_element_type=F32)
        _cast_blocks(cast_in, cast_out, cast_scales)

    pl.when(j == 0)(lambda: body(True))
    pl.when(j > 0)(lambda: body(False))


def _ffn(x, g, wg, wu, wd_half, casts=(), *, tm=FFN_TM, tf=FFN_TF):
    s, d = x.shape
    f = wg.shape[1]
    cast_specs = [pl.BlockSpec(c.block, c.index_map) for c in casts]
    outs = pl.pallas_call(
        functools.partial(_ffn_kernel, n_casts=len(casts), cast_scales=tuple(c.scale for c in casts)),
        grid=(s // tm, f // tf),
        in_specs=[
            pl.BlockSpec((tm, d), lambda i, j: (i, 0)),
            pl.BlockSpec((1, d), lambda i, j: (0, 0)),
            pl.BlockSpec((d, tf), lambda i, j: (0, j)),
            pl.BlockSpec((d, tf), lambda i, j: (0, j)),
            pl.BlockSpec((tf, d), lambda i, j: (j, 0)),
        ] + cast_specs,
        out_specs=[pl.BlockSpec((tm, d), lambda i, j: (i, 0))] + cast_specs,
        out_shape=[jax.ShapeDtypeStruct((s, d), F32)]
        + [jax.ShapeDtypeStruct(c.weight.shape, BF16) for c in casts],
        scratch_shapes=[pltpu.VMEM((tm, d), BF16)],
        compiler_params=pltpu.CompilerParams(
            dimension_semantics=("arbitrary", "arbitrary"), vmem_limit_bytes=VMEM_LIMIT),
        name="ffn",
    )(x, g, wg, wu, wd_half, *[c.weight for c in casts])
    return outs


def _group_sumsq(p, bd_ref):
    p2 = p * p
    hi = p2.astype(BF16)
    lo = (p2 - hi.astype(F32)).astype(BF16)
    outs = []
    for c in range(p.shape[1] // MXU_DIM):
        sl = slice(c * MXU_DIM, (c + 1) * MXU_DIM)
        outs.append(jnp.dot(hi[:, sl], bd_ref[...], preferred_element_type=F32)
                    + jnp.dot(lo[:, sl], bd_ref[...], preferred_element_type=F32))
    return jnp.concatenate(outs, axis=1)


def _qk_norm_rope(p, gain, bd_ref, rc_ref, ra_ref, rb_ref, scale):
    ss = _group_sumsq(p, bd_ref)
    xn = p * lax.rsqrt(ss * (1.0 / DIFF_QKDIM) + EPS) * gain
    rc, ra, rb = rc_ref[...], ra_ref[...], rb_ref[...]
    outs = []
    for c in range(p.shape[1] // LANES):
        xc = xn[:, c * LANES:(c + 1) * LANES]
        up = pltpu.roll(xc, LANES - ROT_DIM // 2, axis=1)
        dn = pltpu.roll(xc, ROT_DIM // 2, axis=1)
        outs.append(xc * rc + up * ra + dn * rb)
    out = jnp.concatenate(outs, axis=1)
    if scale != 1.0:
        out = out * scale
    return out.astype(BF16)


PROJ_TM = 512
_PROJ_ROWS = 256


def _proj_kernel(*refs, n_casts, cast_scales):
    x_ref, g_ref, w_ref, qg_ref, kg_ref, bd_ref, rc_ref, ra_ref, rb_ref = refs[:9]
    cast_in = refs[9:9 + n_casts]
    qk_ref, vt_ref, u_ref = refs[9 + n_casts:12 + n_casts]
    cast_out = refs[12 + n_casts:]
    _cast_blocks(cast_in, cast_out, cast_scales)
    w = u_ref.shape[1]
    q_scale = math.log2(math.e) / math.sqrt(DIFF_QKDIM)
    for r in range(x_ref.shape[0] // _PROJ_ROWS):
        rows = slice(r * _PROJ_ROWS, (r + 1) * _PROJ_ROWS)
        h = _rms_rows(x_ref[rows, :], g_ref[...]).astype(BF16)
        rope = (rc_ref.at[rows, :], ra_ref.at[rows, :], rb_ref.at[rows, :])
        col = lambda n: jnp.dot(h, w_ref[:, n * w:(n + 1) * w], preferred_element_type=F32)
        qk_ref[0, rows, :] = _qk_norm_rope(col(0), qg_ref[...], bd_ref, *rope, q_scale)
        qk_ref[1, rows, :] = _qk_norm_rope(col(1), kg_ref[...], bd_ref, *rope, 1.0)
        vt_ref[:, rows] = col(2).T.astype(BF16)
        u_ref[rows, :] = col(3) * jax.nn.sigmoid(col(4))


def _proj(x, g, w_in, qg, kg, bd, rc, ra, rb, casts=(), *, tm=PROJ_TM):
    s, d = x.shape
    n_cols = w_in.shape[1]
    w = n_cols // 5
    once = pl.Buffered(1)
    cast_specs = [pl.BlockSpec(c.block, c.index_map) for c in casts]
    return pl.pallas_call(
        functools.partial(_proj_kernel, n_casts=len(casts), cast_scales=tuple(c.scale for c in casts)),
        grid=(s // tm,),
        in_specs=[
            pl.BlockSpec((tm, d), lambda i: (i, 0)),
            pl.BlockSpec((1, d), lambda i: (0, 0)),
            pl.BlockSpec((d, n_cols), lambda i: (0, 0), pipeline_mode=once),
            pl.BlockSpec((1, w), lambda i: (0, 0)),
            pl.BlockSpec((1, w), lambda i: (0, 0)),
            pl.BlockSpec((MXU_DIM, MXU_DIM), lambda i: (0, 0)),
            pl.BlockSpec((tm, LANES), lambda i: (i, 0)),
            pl.BlockSpec((tm, LANES), lambda i: (i, 0)),
            pl.BlockSpec((tm, LANES), lambda i: (i, 0)),
        ] + cast_specs,
        out_specs=[
            pl.BlockSpec((2, tm, w), lambda i: (0, i, 0)),
            pl.BlockSpec((w, tm), lambda i: (0, i)),
            pl.BlockSpec((tm, w), lambda i: (i, 0)),
        ] + cast_specs,
        out_shape=[
            jax.ShapeDtypeStruct((2, s, w), BF16),
            jax.ShapeDtypeStruct((w, s), BF16),
            jax.ShapeDtypeStruct((s, w), F32),
        ] + [jax.ShapeDtypeStruct(c.weight.shape, BF16) for c in casts],
        compiler_params=pltpu.CompilerParams(
            dimension_semantics=("arbitrary",), vmem_limit_bytes=VMEM_LIMIT),
        name="proj",
    )(x, g, w_in, qg, kg, bd, rc, ra, rb, *[c.weight for c in casts])


_NT = (((1,), (1,)), ((), ()))


_ONES_ROWS = 16
_CHAIN_GROUP = 2


def _attn_kernel(q_ref, k_ref, vt_ref, lq1_ref, lk1_ref, lq2_ref, lk2_ref, hg_ref, o_ref,
                 acc1, acc2, m1, m2, s_ref, *, t, ts):
    i = pl.program_id(1)
    qt = q_ref[...].astype(F32).T
    row = lax.broadcasted_iota(jnp.int32, qt.shape, 0)
    qz = (jnp.where(row < DIFF_QKDIM, qt, 0.0).astype(BF16), jnp.where(row >= DIFF_QKDIM, qt, 0.0).astype(BF16))
    accs, ms = (acc1, acc2), (m1, m2)
    ones = jnp.ones((_ONES_ROWS, t), BF16)
    chains = [(st, c) for st in range(t // ts) for c in range(2)]

    def score(j, chain, diagonal):
        st, c = chain
        start = pl.multiple_of(j * t, t)
        nk = (st + 1) * ts if diagonal else t
        kb = k_ref[pl.ds(start, nk), :]
        s = jnp.dot(kb, qz[c][:, st * ts:(st + 1) * ts], preferred_element_type=F32)
        if diagonal:
            krow = lax.broadcasted_iota(jnp.int32, s.shape, 0)
            qcol = lax.broadcasted_iota(jnp.int32, s.shape, 1) + st * ts
            s = jnp.where(krow <= qcol, s, NEG_BIG)
        s_ref[2 * st + c, 0:nk, :] = s
        return jnp.max(s, axis=0, keepdims=True), nk

    def softmax(s_max, chain, diagonal):
        st, c = chain
        cols = slice(st * ts, (st + 1) * ts)
        m_blk, nk = s_max
        if diagonal:
            m_new, alpha = m_blk, None
        else:
            m_old = ms[c][:, cols]
            m_new = jnp.maximum(m_old, m_blk)
            alpha = jnp.exp2(m_old - m_new)
        ms[c][:, cols] = m_new
        return jnp.exp2(s_ref[2 * st + c, 0:nk, :] - m_new).astype(BF16), alpha

    def pv(j, chain, prob):
        st, c = chain
        p, alpha = prob
        nk = p.shape[0]
        start = pl.multiple_of(j * t, t)
        cols = slice(st * ts, (st + 1) * ts)
        vb = jnp.concatenate([vt_ref[:, pl.ds(start, nk)], ones[:, :nk]], axis=0)
        acc_new = jnp.dot(vb, p, preferred_element_type=F32)
        accs[c][st] = acc_new if alpha is None else alpha * accs[c][st] + acc_new

    def blocks(js, diagonal):
        groups = [(j, chains[g:g + _CHAIN_GROUP]) for j in js for g in range(0, len(chains), _CHAIN_GROUP)]
        j0, first = groups[0]
        maxima = [score(j0, chain, diagonal) for chain in first]
        for g, (j, group) in enumerate(groups):
            nxt = []
            if g + 1 < len(groups):
                j_nxt, group_nxt = groups[g + 1]
                nxt = [score(j_nxt, chain, diagonal) for chain in group_nxt]
            probs = [softmax(mx, chain, diagonal) for chain, mx in zip(group, maxima)]
            for chain, prob in zip(group, probs):
                pv(j, chain, prob)
            maxima = nxt

    blocks([i], True)

    def body(n, carry):
        blocks([2 * n, 2 * n + 1], False)
        return carry

    lax.fori_loop(0, i // 2, body, 0)
    pl.when(i % 2 == 1)(lambda: blocks([i - 1], False))

    lam = (jnp.exp(jnp.sum(lq1_ref[...] * lk1_ref[...], axis=-1, keepdims=True))
           - jnp.exp(jnp.sum(lq2_ref[...] * lk2_ref[...], axis=-1, keepdims=True))
           + LAMBDA_INIT)
    nv = DIFF_VDIM
    for st in range(t // ts):
        ot = (acc1[st, 0:nv, :] / acc1[st, nv:nv + 1, :]
              - lam * (acc2[st, 0:nv, :] / acc2[st, nv:nv + 1, :]))
        mean_sq = jnp.mean(ot * ot, axis=0, keepdims=True)
        on = ot * lax.rsqrt(mean_sq + EPS)
        o_ref[st * ts:(st + 1) * ts, :] = (on.T * hg_ref[...] * (1.0 - LAMBDA_INIT)).astype(BF16)


def _attn(qk, vt, lq1, lk1, lq2, lk2, hg, *, t=1024, ts=256):
    _, s, w = qk.shape
    nh = w // DIFF_VDIM
    vec = lambda n: pl.BlockSpec((1, n), lambda h, i: (0, 0))
    return pl.pallas_call(
        functools.partial(_attn_kernel, t=t, ts=ts),
        grid=(nh, s // t),
        in_specs=[
            pl.BlockSpec((None, t, DIFF_VDIM), lambda h, i: (0, i, h)),
            pl.BlockSpec((None, s, DIFF_VDIM), lambda h, i: (1, 0, h)),
            pl.BlockSpec((DIFF_VDIM, s), lambda h, i: (h, 0)),
            vec(DIFF_QKDIM), vec(DIFF_QKDIM), vec(DIFF_QKDIM), vec(DIFF_QKDIM),
            vec(DIFF_VDIM),
        ],
        out_specs=pl.BlockSpec((t, DIFF_VDIM), lambda h, i: (i, h)),
        out_shape=jax.ShapeDtypeStruct((s, w), BF16),
        scratch_shapes=[
            pltpu.VMEM((t // ts, DIFF_VDIM + _ONES_ROWS, ts), F32),
            pltpu.VMEM((t // ts, DIFF_VDIM + _ONES_ROWS, ts), F32),
            pltpu.VMEM((1, t), F32), pltpu.VMEM((1, t), F32),
            pltpu.VMEM((2 * (t // ts), t, ts), F32),
        ],
        compiler_params=pltpu.CompilerParams(
            dimension_semantics=("arbitrary", "arbitrary"), vmem_limit_bytes=VMEM_LIMIT),
        name="attn",
    )(qk, qk, vt, lq1, lk1, lq2, lk2, hg)


_HALO = 32
_SUB = 8
_ROWS = 32


def _conv_kernel(cur_ref, halo_ref, w_ref, b_ref, y_ref, e_ref, *, tc):
    i = pl.program_id(0)
    e_ref[0, 0:_HALO, :] = jnp.where(i > 0, halo_ref[...], 0.0)
    e_ref[0, _HALO:_HALO + tc, :] = cur_ref[...]
    span = tc + _HALO - _SUB
    for r in range(1, _SUB):
        e_ref[r, 0:span, :] = e_ref[0, r:r + span, :]

    first = _HALO - (CONV_K - 1)
    n_sub = _ROWS // _SUB

    def chunk(c, carry):
        t0 = pl.multiple_of(c * _ROWS, _ROWS)
        acc = [b_ref[...] for _ in range(n_sub)]
        for r in range(_SUB):
            taps_r = [(k, (first + k) // _SUB) for k in range(CONV_K) if (first + k) % _SUB == r]
            weights = {k: w_ref[k] for k, _ in taps_r}
            for idx in range(max(a for _, a in taps_r) + n_sub):
                uses = [(k, idx - a) for k, a in taps_r if 0 <= idx - a < n_sub]
                if not uses:
                    continue
                tile = e_ref[r, pl.ds(t0 + idx * _SUB, _SUB), :]
                for k, q in uses:
                    acc[q] = acc[q] + weights[k] * tile
        for q in range(n_sub):
            y_ref[pl.ds(t0 + q * _SUB, _SUB), :] = acc[q]
        return carry

    lax.fori_loop(0, tc // _ROWS, chunk, 0)


def _conv(u, taps, bias, *, tc=512):
    s, c = u.shape
    halo_blocks = tc // _HALO
    return pl.pallas_call(
        functools.partial(_conv_kernel, tc=tc),
        grid=(s // tc,),
        in_specs=[
            pl.BlockSpec((tc, c), lambda i: (i, 0)),
            pl.BlockSpec((_HALO, c), lambda i: (jnp.maximum(i * halo_blocks - 1, 0), 0)),
            pl.BlockSpec((CONV_K, _SUB, c), lambda i: (0, 0, 0)),
            pl.BlockSpec((_SUB, c), lambda i: (0, 0)),
        ],
        out_specs=pl.BlockSpec((tc, c), lambda i: (i, 0)),
        out_shape=jax.ShapeDtypeStruct((s, c), F32),
        scratch_shapes=[pltpu.VMEM((_SUB, tc + _HALO, c), F32)],
        compiler_params=pltpu.CompilerParams(
            dimension_semantics=("arbitrary",), vmem_limit_bytes=VMEM_LIMIT),
        name="conv",
    )(u, u, taps, bias)


def _memkv_kernel(mem_ref, g_ref, w_ref, kg_ref, k_ref, v_ref):
    h = _rms_rows(mem_ref[...], g_ref[...]).astype(BF16)
    kv = jnp.dot(h, w_ref[...], preferred_element_type=F32)
    width = k_ref.shape[1]
    for hd in range(XATTN_HEADS):
        sl = slice(hd * XATTN_HDIM, (hd + 1) * XATTN_HDIM)
        k_ref[:, sl] = _rms_rows(kv[:, sl], kg_ref[...]).astype(BF16)
    v_ref[...] = kv[:, width:].astype(BF16)


def _memkv(mem, g, w_kv, kg):
    n, _ = mem.shape
    width = w_kv.shape[1] // 2
    return pl.pallas_call(
        _memkv_kernel,
        out_shape=[jax.ShapeDtypeStruct((n, width), BF16), jax.ShapeDtypeStruct((n, width), BF16)],
        compiler_params=pltpu.CompilerParams(vmem_limit_bytes=VMEM_LIMIT),
        name="memkv",
    )(mem, g, w_kv, kg)


def _post_kernel(x_ref, od_ref, y_ref, lng_ref, lnb_ref, wo_ref, g_ref, wq_ref, qg_ref, km_ref, vm_ref,
                 wxo_ref, o_ref):
    half = od_ref.shape[1]
    x2 = x_ref[...] + jnp.dot(od_ref[...], wo_ref[0:half, :], preferred_element_type=F32)
    y = y_ref[...]
    yc = y - jnp.mean(y, axis=-1, keepdims=True)
    var = jnp.mean(yc * yc, axis=-1, keepdims=True)
    yn = yc * lax.rsqrt(var + EPS) * lng_ref[...] + lnb_ref[...]
    oc = (yn * jax.nn.sigmoid(yn)).astype(BF16)
    x2 = x2 + jnp.dot(oc, wo_ref[half:, :], preferred_element_type=F32)
    h = _rms_rows(x2, g_ref[...]).astype(BF16)
    q = jnp.dot(h, wq_ref[...], preferred_element_type=F32)
    scale = 1.0 / math.sqrt(XATTN_HDIM)
    outs = []
    for hd in range(XATTN_HEADS):
        sl = slice(hd * XATTN_HDIM, (hd + 1) * XATTN_HDIM)
        qn = _rms_rows(q[:, sl], qg_ref[...]).astype(BF16)
        sc = lax.dot_general(qn, km_ref[:, sl], _NT, preferred_element_type=F32) * scale
        sc = sc - jnp.max(sc, axis=-1, keepdims=True)
        p = jnp.exp(sc)
        p = p / jnp.sum(p, axis=-1, keepdims=True)
        outs.append(jnp.dot(p.astype(BF16), vm_ref[:, sl], preferred_element_type=F32))
    o = jnp.concatenate(outs, axis=1).astype(BF16)
    o_ref[...] = x2 + jnp.dot(o, wxo_ref[...], preferred_element_type=F32)


def _post(x, od, y_conv, ln_g, ln_b, w_out, g, wq, qg, km, vm, wxo, *, tm=512):
    s, d = x.shape
    half = od.shape[1]
    xw = wq.shape[1]
    n_mem = km.shape[0]
    const = lambda shape: pl.BlockSpec(shape, lambda i: (0, 0))
    return pl.pallas_call(
        _post_kernel,
        grid=(s // tm,),
        in_specs=[
            pl.BlockSpec((tm, d), lambda i: (i, 0)),
            pl.BlockSpec((tm, half), lambda i: (i, 0)),
            pl.BlockSpec((tm, half), lambda i: (i, 0)),
            const((1, half)), const((1, half)),
            const((2 * half, d)), const((1, d)), const((d, xw)), const((1, XATTN_HDIM)),
            const((n_mem, xw)), const((n_mem, xw)), const((xw, d)),
        ],
        out_specs=pl.BlockSpec((tm, d), lambda i: (i, 0)),
        out_shape=jax.ShapeDtypeStruct((s, d), F32),
        compiler_params=pltpu.CompilerParams(
            dimension_semantics=("arbitrary",), vmem_limit_bytes=VMEM_LIMIT),
        name="post",
    )(x, od, y_conv, ln_g, ln_b, w_out, g, wq, qg, km, vm, wxo)


def _rope_tables(positions):
    half = ROT_DIM // 2
    inv_freq = ROPE_THETA ** (-jnp.arange(0, ROT_DIM, 2, dtype=F32) / ROT_DIM)
    ang = positions.astype(F32)[:, None] * inv_freq
    cos, sin = jnp.cos(ang), jnp.sin(ang)
    n = positions.shape[0]
    rest = DIFF_QKDIM - ROT_DIM
    rc = jnp.concatenate([cos, cos, jnp.ones((n, rest), F32)], axis=1)
    ra = jnp.concatenate([-sin, jnp.zeros((n, half + rest), F32)], axis=1)
    rb = jnp.concatenate([jnp.zeros((n, half), F32), sin, jnp.zeros((n, rest), F32)], axis=1)
    rep = LANES // DIFF_QKDIM
    return jnp.tile(rc, (1, rep)), jnp.tile(ra, (1, rep)), jnp.tile(rb, (1, rep))


def kernel(x, mem, positions, ffn1_norm_g, ffn1_w_gate, ffn1_w_up, ffn1_w_down, mix_norm_g, w_in, q_norm_g, k_norm_g, lambda_q1, lambda_k1, lambda_q2, lambda_k2, diff_head_norm_g, conv_w, conv_b, conv_ln_g, conv_ln_b, w_out, xattn_norm_g, mem_norm_g, xattn_w_q, xattn_w_kv, xattn_q_norm_g, xattn_k_norm_g, xattn_w_o, ffn2_norm_g, ffn2_w_gate, ffn2_w_up, ffn2_w_down):
    b, s, d = x.shape
    assert b == 1 and ffn1_norm_g.shape[0] == 1
    bf = lambda w: w[0].astype(BF16)
    row = lambda v: v[0][None, :]

    rc, ra, rb = _rope_tables(positions[0])
    width = DIFF_HEADS * DIFF_VDIM
    qg = jnp.tile(q_norm_g[0], width // DIFF_QKDIM)[None, :]
    kg = jnp.tile(k_norm_g[0], width // DIFF_QKDIM)[None, :]
    blk = jnp.arange(MXU_DIM) // DIFF_QKDIM
    bd = (blk[:, None] == blk[None, :]).astype(BF16)

    n_i, n_j = s // FFN_TM, ffn1_w_gate.shape[-1] // FFN_TF

    def by_rows(w, scale=1.0):
        r, c = w.shape
        last = c // FFN_TF - 1
        return _HostedCast(w, (r // n_i, FFN_TF), lambda i, j: (i, jnp.minimum(j, last)), scale)

    def by_cols(w, scale=1.0):
        r, c = w.shape
        return _HostedCast(w, (r // n_j, c // n_i), lambda i, j: (j, i), scale)

    x1, wg2, wu2, wd2_half, w_in_bf = _ffn(
        x[0], row(ffn1_norm_g), bf(ffn1_w_gate), bf(ffn1_w_up), (0.5 * ffn1_w_down[0]).astype(BF16),
        [by_rows(ffn2_w_gate[0]), by_rows(ffn2_w_up[0]), by_cols(ffn2_w_down[0], 0.5), by_rows(w_in[0])])

    n_p = s // PROJ_TM
    row_block = lambda w: _HostedCast(w, (w.shape[0] // n_p, w.shape[1]), lambda i: (i, 0))
    qk, vt, u, w_out_bf, wq_bf, wkv_bf, wo_bf = _proj(
        x1, row(mix_norm_g), w_in_bf, qg, kg, bd, rc, ra, rb,
        [row_block(w_out[0]), row_block(xattn_w_q[0]), row_block(xattn_w_kv[0]), row_block(xattn_w_o[0])])
    o_diff = _attn(qk, vt, row(lambda_q1), row(lambda_k1), row(lambda_q2), row(lambda_k2),
                   row(diff_head_norm_g))
    conv_taps = jnp.broadcast_to(conv_w[0][:, None, :], (CONV_K, _SUB, conv_w.shape[-1]))
    conv_bias = jnp.broadcast_to(conv_b[0][None, :], (_SUB, conv_b.shape[-1]))
    y_conv = _conv(u, conv_taps, conv_bias)

    km, vm = _memkv(mem[0], row(mem_norm_g), wkv_bf, row(xattn_k_norm_g))
    x3 = _post(x1, o_diff, y_conv, row(conv_ln_g), row(conv_ln_b), w_out_bf, row(xattn_norm_g), wq_bf,
               row(xattn_q_norm_g), km, vm, wo_bf)

    out = _ffn(x3, row(ffn2_norm_g), wg2, wu2, wd2_half)[0]
    return out[None]
```

```python
import functools
import math
from typing import Callable, NamedTuple

import jax
import jax.numpy as jnp
from jax import lax
from jax.experimental import pallas as pl
from jax.experimental.pallas import tpu as pltpu

F32 = jnp.float32
BF16 = jnp.bfloat16

EPS = 1e-6
ROPE_THETA = 500000.0
DIFF_HEADS = 8
DIFF_VDIM = 128
DIFF_QKDIM = 64
ROT_DIM = 16
CONV_K = 31
XATTN_HEADS = 4
XATTN_HDIM = 128
LAMBDA_INIT = 0.8 - 0.6 * math.exp(0.0)

LANES = 128
MXU_DIM = 256
VMEM_LIMIT = 56 * 1024 * 1024

NEG_BIG = -1e30


def _rms_rows(x, g):
    ms = jnp.mean(x * x, axis=-1, keepdims=True)
    return x * lax.rsqrt(ms + EPS) * g


class _HostedCast(NamedTuple):
    weight: jax.Array
    block: tuple
    index_map: Callable
    scale: float = 1.0


def _cast_blocks(srcs, dsts, scales):
    for src, dst, scale in zip(srcs, dsts, scales):
        v = src[...]
        dst[...] = (v if scale == 1.0 else v * scale).astype(BF16)


FFN_TM = 1024
FFN_TF = 512
_FFN_ROWS = 256


def _ffn_kernel(*refs, n_casts, cast_scales):
    x_ref, g_ref, wg_ref, wu_ref, wd_ref = refs[:5]
    cast_in = refs[5:5 + n_casts]
    o_ref = refs[5 + n_casts]
    cast_out = refs[6 + n_casts:6 + 2 * n_casts]
    h_ref = refs[6 + 2 * n_casts]
    j = pl.program_id(1)

    n_chunks = o_ref.shape[0] // _FFN_ROWS

    def body(first):
        def gate_up(r):
            rows = slice(r * _FFN_ROWS, (r + 1) * _FFN_ROWS)
            if first:
                h = _rms_rows(x_ref[rows, :], g_ref[...]).astype(BF16)
                h_ref[rows, :] = h
            else:
                h = h_ref[rows, :]
            a = jnp.dot(h, wg_ref[...], preferred_element_type=F32)
            b = jnp.dot(h, wu_ref[...], preferred_element_type=F32)
            return a, b

        ab = gate_up(0)
        for r in range(n_chunks):
            a, b = ab
            if r + 1 < n_chunks:
                ab = gate_up(r + 1)
            act = (a * jax.nn.sigmoid(a) * b).astype(BF16)
            rows = slice(r * _FFN_ROWS, (r + 1) * _FFN_ROWS)
            base = x_ref[rows, :] if first else o_ref[rows, :]
            o_ref[rows, :] = base + jnp.dot(act, wd_ref[...], preferred_element_type=F32)
        _cast_blocks(cast_in, cast_out, cast_scales)

    pl.when(j == 0)(lambda: body(True))
    pl.when(j > 0)(lambda: body(False))


def _ffn(x, g, wg, wu, wd_half, casts=(), *, tm=FFN_TM, tf=FFN_TF):
    s, d = x.shape
    f = wg.shape[1]
    cast_specs = [pl.BlockSpec(c.block, c.index_map) for c in casts]
    outs = pl.pallas_call(
        functools.partial(_ffn_kernel, n_casts=len(casts), cast_scales=tuple(c.scale for c in casts)),
        grid=(s // tm, f // tf),
        in_specs=[
            pl.BlockSpec((tm, d), lambda i, j: (i, 0)),
            pl.BlockSpec((1, d), lambda i, j: (0, 0)),
            pl.BlockSpec((d, tf), lambda i, j: (0, j)),
            pl.BlockSpec((d, tf), lambda i, j: (0, j)),
            pl.BlockSpec((tf, d), lambda i, j: (j, 0)),
        ] + cast_specs,
        out_specs=[pl.BlockSpec((tm, d), lambda i, j: (i, 0))] + cast_specs,
        out_shape=[jax.ShapeDtypeStruct((s, d), F32)]
        + [jax.ShapeDtypeStruct(c.weight.shape, BF16) for c in casts],
        scratch_shapes=[pltpu.VMEM((tm, d), BF16)],
        compiler_params=pltpu.CompilerParams(
            dimension_semantics=("arbitrary", "arbitrary"), vmem_limit_bytes=VMEM_LIMIT),
        name="ffn",
    )(x, g, wg, wu, wd_half, *[c.weight for c in casts])
    return outs


def _group_sumsq(p, bd_ref):
    p2 = (p * p).astype(BF16)
    outs = [jnp.dot(p2[:, c * MXU_DIM:(c + 1) * MXU_DIM], bd_ref[...], preferred_element_type=F32)
            for c in range(p.shape[1] // MXU_DIM)]
    return jnp.concatenate(outs, axis=1)


def _qk_norm_rope(p, gain, bd_ref, rc_ref, ra_ref, rb_ref, scale):
    ss = _group_sumsq(p, bd_ref)
    xn = p * lax.rsqrt(ss * (1.0 / DIFF_QKDIM) + EPS) * gain
    rc, ra, rb = rc_ref[...], ra_ref[...], rb_ref[...]
    outs = []
    for c in range(p.shape[1] // LANES):
        xc = xn[:, c * LANES:(c + 1) * LANES]
        up = pltpu.roll(xc, LANES - ROT_DIM // 2, axis=1)
        dn = pltpu.roll(xc, ROT_DIM // 2, axis=1)
        outs.append(xc * rc + up * ra + dn * rb)
    out = jnp.concatenate(outs, axis=1)
    if scale != 1.0:
        out = out * scale
    return out.astype(BF16)


PROJ_TM = 512
_PROJ_ROWS = 256


def _proj_kernel(*refs, n_casts, cast_scales):
    x_ref, g_ref, w_ref, qg_ref, kg_ref, bd_ref, rope_ref = refs[:7]
    cast_in = refs[7:7 + n_casts]
    qk_ref, vt_ref, u_ref = refs[7 + n_casts:10 + n_casts]
    cast_out = refs[10 + n_casts:]
    _cast_blocks(cast_in, cast_out, cast_scales)
    w = u_ref.shape[1]
    q_scale = math.log2(math.e) / math.sqrt(DIFF_QKDIM)
    for r in range(x_ref.shape[0] // _PROJ_ROWS):
        rows = slice(r * _PROJ_ROWS, (r + 1) * _PROJ_ROWS)
        h = _rms_rows(x_ref[rows, :], g_ref[...]).astype(BF16)
        rope = tuple(rope_ref.at[rows, n * LANES:(n + 1) * LANES] for n in range(3))
        col = lambda n: jnp.dot(h, w_ref[:, n * w:(n + 1) * w], preferred_element_type=F32)
        qk_ref[0, rows, :] = _qk_norm_rope(col(0), qg_ref[...], bd_ref, *rope, q_scale)
        qk_ref[1, rows, :] = _qk_norm_rope(col(1), kg_ref[...], bd_ref, *rope, 1.0)
        vt_ref[:, rows] = col(2).T.astype(BF16)
        u_ref[rows, :] = col(3) * jax.nn.sigmoid(col(4))


def _proj(x, g, w_in, qg, kg, bd, rope, casts=(), *, tm=PROJ_TM):
    s, d = x.shape
    n_cols = w_in.shape[1]
    w = n_cols // 5
    once = pl.Buffered(1)
    cast_specs = [pl.BlockSpec(c.block, c.index_map) for c in casts]
    return pl.pallas_call(
        functools.partial(_proj_kernel, n_casts=len(casts), cast_scales=tuple(c.scale for c in casts)),
        grid=(s // tm,),
        in_specs=[
            pl.BlockSpec((tm, d), lambda i: (i, 0)),
            pl.BlockSpec((1, d), lambda i: (0, 0)),
            pl.BlockSpec((d, n_cols), lambda i: (0, 0), pipeline_mode=once),
            pl.BlockSpec((1, w), lambda i: (0, 0)),
            pl.BlockSpec((1, w), lambda i: (0, 0)),
            pl.BlockSpec((MXU_DIM, MXU_DIM), lambda i: (0, 0)),
            pl.BlockSpec((tm, 3 * LANES), lambda i: (i, 0)),
        ] + cast_specs,
        out_specs=[
            pl.BlockSpec((2, tm, w), lambda i: (0, i, 0)),
            pl.BlockSpec((w, tm), lambda i: (0, i)),
            pl.BlockSpec((tm, w), lambda i: (i, 0)),
        ] + cast_specs,
        out_shape=[
            jax.ShapeDtypeStruct((2, s, w), BF16),
            jax.ShapeDtypeStruct((w, s), BF16),
            jax.ShapeDtypeStruct((s, w), F32),
        ] + [jax.ShapeDtypeStruct(c.weight.shape, BF16) for c in casts],
        compiler_params=pltpu.CompilerParams(
            dimension_semantics=("arbitrary",), vmem_limit_bytes=VMEM_LIMIT),
        name="proj",
    )(x, g, w_in, qg, kg, bd, rope, *[c.weight for c in casts])


_ONES_ROWS = 16
_CHAIN_GROUP = 2


def _attn_kernel(q_ref, k_ref, vt_ref, lq1_ref, lk1_ref, lq2_ref, lk2_ref, hg_ref, o_ref,
                 acc1, acc2, m1, m2, s_ref, *, t, ts):
    i = pl.program_id(1)
    qt = q_ref[...].astype(F32).T
    row = lax.broadcasted_iota(jnp.int32, qt.shape, 0)
    qz = (jnp.where(row < DIFF_QKDIM, qt, 0.0).astype(BF16), jnp.where(row >= DIFF_QKDIM, qt, 0.0).astype(BF16))
    accs, ms = (acc1, acc2), (m1, m2)
    ones = jnp.ones((_ONES_ROWS, t), BF16)
    chains = [(st, c) for st in range(t // ts) for c in range(2)]

    def score(j, chain, diagonal):
        st, c = chain
        start = pl.multiple_of(j * t, t)
        nk = (st + 1) * ts if diagonal else t
        kb = k_ref[pl.ds(start, nk), :]
        s = jnp.dot(kb, qz[c][:, st * ts:(st + 1) * ts], preferred_element_type=F32)
        if diagonal:
            krow = lax.broadcasted_iota(jnp.int32, s.shape, 0)
            qcol = lax.broadcasted_iota(jnp.int32, s.shape, 1) + st * ts
            s = jnp.where(krow <= qcol, s, NEG_BIG)
        s_ref[2 * st + c, 0:nk, :] = s
        return jnp.max(s, axis=0, keepdims=True), nk

    def softmax(s_max, chain, diagonal):
        st, c = chain
        cols = slice(st * ts, (st + 1) * ts)
        m_blk, nk = s_max
        if diagonal:
            m_new, alpha = m_blk, None
        else:
            m_old = ms[c][:, cols]
            m_new = jnp.maximum(m_old, m_blk)
            alpha = jnp.exp2(m_old - m_new)
        ms[c][:, cols] = m_new
        return jnp.exp2(s_ref[2 * st + c, 0:nk, :] - m_new).astype(BF16), alpha

    def pv(j, chain, prob):
        st, c = chain
        p, alpha = prob
        nk = p.shape[0]
        start = pl.multiple_of(j * t, t)
        cols = slice(st * ts, (st + 1) * ts)
        vb = jnp.concatenate([vt_ref[:, pl.ds(start, nk)], ones[:, :nk]], axis=0)
        acc_new = jnp.dot(vb, p, preferred_element_type=F32)
        accs[c][st] = acc_new if alpha is None else alpha * accs[c][st] + acc_new

    def blocks(js, diagonal):
        groups = [(j, chains[g:g + _CHAIN_GROUP]) for j in js for g in range(0, len(chains), _CHAIN_GROUP)]
        j0, first = groups[0]
        maxima = [score(j0, chain, diagonal) for chain in first]
        for g, (j, group) in enumerate(groups):
            nxt = []
            if g + 1 < len(groups):
                j_nxt, group_nxt = groups[g + 1]
                nxt = [score(j_nxt, chain, diagonal) for chain in group_nxt]
            probs = [softmax(mx, chain, diagonal) for chain, mx in zip(group, maxima)]
            for chain, prob in zip(group, probs):
                pv(j, chain, prob)
            maxima = nxt

    blocks([i], True)

    def body(n, carry):
        blocks([2 * n, 2 * n + 1], False)
        return carry

    lax.fori_loop(0, i // 2, body, 0)
    pl.when(i % 2 == 1)(lambda: blocks([i - 1], False))

    lam = (jnp.exp(jnp.sum(lq1_ref[...] * lk1_ref[...], axis=-1, keepdims=True))
           - jnp.exp(jnp.sum(lq2_ref[...] * lk2_ref[...], axis=-1, keepdims=True))
           + LAMBDA_INIT)
    nv = DIFF_VDIM
    for st in range(t // ts):
        ot = (acc1[st, 0:nv, :] / acc1[st, nv:nv + 1, :]
              - lam * (acc2[st, 0:nv, :] / acc2[st, nv:nv + 1, :]))
        mean_sq = jnp.mean(ot * ot, axis=0, keepdims=True)
        on = ot * lax.rsqrt(mean_sq + EPS)
        o_ref[st * ts:(st + 1) * ts, :] = (on.T * hg_ref[...] * (1.0 - LAMBDA_INIT)).astype(BF16)


def _attn(qk, vt, lq1, lk1, lq2, lk2, hg, *, t=1024, ts=256):
    _, s, w = qk.shape
    nh = w // DIFF_VDIM
    vec = lambda n: pl.BlockSpec((1, n), lambda h, i: (0, 0))
    return pl.pallas_call(
        functools.partial(_attn_kernel, t=t, ts=ts),
        grid=(nh, s // t),
        in_specs=[
            pl.BlockSpec((None, t, DIFF_VDIM), lambda h, i: (0, i, h)),
            pl.BlockSpec((None, s, DIFF_VDIM), lambda h, i: (1, 0, h)),
            pl.BlockSpec((DIFF_VDIM, s), lambda h, i: (h, 0)),
            vec(DIFF_QKDIM), vec(DIFF_QKDIM), vec(DIFF_QKDIM), vec(DIFF_QKDIM),
            vec(DIFF_VDIM),
        ],
        out_specs=pl.BlockSpec((t, DIFF_VDIM), lambda h, i: (i, h)),
        out_shape=jax.ShapeDtypeStruct((s, w), BF16),
        scratch_shapes=[
            pltpu.VMEM((t // ts, DIFF_VDIM + _ONES_ROWS, ts), F32),
            pltpu.VMEM((t // ts, DIFF_VDIM + _ONES_ROWS, ts), F32),
            pltpu.VMEM((1, t), F32), pltpu.VMEM((1, t), F32),
            pltpu.VMEM((2 * (t // ts), t, ts), F32),
        ],
        compiler_params=pltpu.CompilerParams(
            dimension_semantics=("arbitrary", "arbitrary"), vmem_limit_bytes=VMEM_LIMIT),
        name="attn",
    )(qk, qk, vt, lq1, lk1, lq2, lk2, hg)


_HALO = 32
_SUB = 8
_ROWS = 32


def _conv_kernel(cur_ref, halo_ref, w_ref, b_ref, y_ref, e_ref, *, tc):
    i = pl.program_id(0)
    e_ref[0, 0:_HALO, :] = jnp.where(i > 0, halo_ref[...], 0.0)
    e_ref[0, _HALO:_HALO + tc, :] = cur_ref[...]
    span = tc + _HALO - _SUB
    for r in range(1, _SUB):
        e_ref[r, 0:span, :] = e_ref[0, r:r + span, :]

    first = _HALO - (CONV_K - 1)
    n_sub = _ROWS // _SUB

    def chunk(c, carry):
        t0 = pl.multiple_of(c * _ROWS, _ROWS)
        acc = [b_ref[...] for _ in range(n_sub)]
        for r in range(_SUB):
            taps_r = [(k, (first + k) // _SUB) for k in range(CONV_K) if (first + k) % _SUB == r]
            weights = {k: w_ref[k] for k, _ in taps_r}
            for idx in range(max(a for _, a in taps_r) + n_sub):
                uses = [(k, idx - a) for k, a in taps_r if 0 <= idx - a < n_sub]
                if not uses:
                    continue
                tile = e_ref[r, pl.ds(t0 + idx * _SUB, _SUB), :]
                for k, q in uses:
                    acc[q] = acc[q] + weights[k] * tile
        for q in range(n_sub):
            y_ref[pl.ds(t0 + q * _SUB, _SUB), :] = acc[q]
        return carry

    lax.fori_loop(0, tc // _ROWS, chunk, 0, unroll=2)


def _conv(u, taps, bias, *, tc=512):
    s, c = u.shape
    halo_blocks = tc // _HALO
    return pl.pallas_call(
        functools.partial(_conv_kernel, tc=tc),
        grid=(s // tc,),
        in_specs=[
            pl.BlockSpec((tc, c), lambda i: (i, 0)),
            pl.BlockSpec((_HALO, c), lambda i: (jnp.maximum(i * halo_blocks - 1, 0), 0)),
            pl.BlockSpec((CONV_K, _SUB, c), lambda i: (0, 0, 0)),
            pl.BlockSpec((_SUB, c), lambda i: (0, 0)),
        ],
        out_specs=pl.BlockSpec((tc, c), lambda i: (i, 0)),
        out_shape=jax.ShapeDtypeStruct((s, c), F32),
        scratch_shapes=[pltpu.VMEM((_SUB, tc + _HALO, c), F32)],
        compiler_params=pltpu.CompilerParams(
            dimension_semantics=("arbitrary",), vmem_limit_bytes=VMEM_LIMIT),
        name="conv",
    )(u, u, taps, bias)


def _memkv_kernel(mem_ref, g_ref, w_ref, kg_ref, k_ref, v_ref):
    h = _rms_rows(mem_ref[...], g_ref[...]).astype(BF16)
    kv = jnp.dot(h, w_ref[...], preferred_element_type=F32)
    width = v_ref.shape[1]
    for hd in range(XATTN_HEADS):
        sl = slice(hd * XATTN_HDIM, (hd + 1) * XATTN_HDIM)
        k_ref[sl, :] = _rms_rows(kv[:, sl], kg_ref[...]).T.astype(BF16)
    v_ref[...] = kv[:, width:].astype(BF16)


def _memkv(mem, g, w_kv, kg):
    n, _ = mem.shape
    width = w_kv.shape[1] // 2
    return pl.pallas_call(
        _memkv_kernel,
        out_shape=[jax.ShapeDtypeStruct((width, n), BF16), jax.ShapeDtypeStruct((n, width), BF16)],
        compiler_params=pltpu.CompilerParams(vmem_limit_bytes=VMEM_LIMIT),
        name="memkv",
    )(mem, g, w_kv, kg)


def _post_kernel(x_ref, od_ref, y_ref, lng_ref, lnb_ref, wo_ref, g_ref, wq_ref, qg_ref, kmt_ref, vm_ref,
                 wxo_ref, o_ref):
    half = od_ref.shape[1]
    x2 = x_ref[...] + jnp.dot(od_ref[...], wo_ref[0:half, :], preferred_element_type=F32)
    y = y_ref[...]
    yc = y - jnp.mean(y, axis=-1, keepdims=True)
    var = jnp.mean(yc * yc, axis=-1, keepdims=True)
    yn = yc * lax.rsqrt(var + EPS) * lng_ref[...] + lnb_ref[...]
    oc = (yn * jax.nn.sigmoid(yn)).astype(BF16)
    x2 = x2 + jnp.dot(oc, wo_ref[half:, :], preferred_element_type=F32)
    h = _rms_rows(x2, g_ref[...]).astype(BF16)
    q = jnp.dot(h, wq_ref[...], preferred_element_type=F32)
    scale = 1.0 / math.sqrt(XATTN_HDIM)
    outs = []
    for hd in range(XATTN_HEADS):
        sl = slice(hd * XATTN_HDIM, (hd + 1) * XATTN_HDIM)
        qn = _rms_rows(q[:, sl], qg_ref[...]).astype(BF16)
        sc = jnp.dot(qn, kmt_ref[sl, :], preferred_element_type=F32) * scale
        sc = sc - jnp.max(sc, axis=-1, keepdims=True)
        p = jnp.exp(sc)
        p = p / jnp.sum(p, axis=-1, keepdims=True)
        outs.append(jnp.dot(p.astype(BF16), vm_ref[:, sl], preferred_element_type=F32))
    o = jnp.concatenate(outs, axis=1).astype(BF16)
    o_ref[...] = x2 + jnp.dot(o, wxo_ref[...], preferred_element_type=F32)


def _post(x, od, y_conv, ln_g, ln_b, w_out, g, wq, qg, kmt, vm, wxo, *, tm=512):
    s, d = x.shape
    half = od.shape[1]
    xw = wq.shape[1]
    n_mem = vm.shape[0]
    const = lambda shape: pl.BlockSpec(shape, lambda i: (0, 0))
    return pl.pallas_call(
        _post_kernel,
        grid=(s // tm,),
        in_specs=[
            pl.BlockSpec((tm, d), lambda i: (i, 0)),
            pl.BlockSpec((tm, half), lambda i: (i, 0)),
            pl.BlockSpec((tm, half), lambda i: (i, 0)),
            const((1, half)), const((1, half)),
            const((2 * half, d)), const((1, d)), const((d, xw)), const((1, XATTN_HDIM)),
            const((xw, n_mem)), const((n_mem, xw)), const((xw, d)),
        ],
        out_specs=pl.BlockSpec((tm, d), lambda i: (i, 0)),
        out_shape=jax.ShapeDtypeStruct((s, d), F32),
        compiler_params=pltpu.CompilerParams(
            dimension_semantics=("arbitrary",), vmem_limit_bytes=VMEM_LIMIT),
        name="post",
    )(x, od, y_conv, ln_g, ln_b, w_out, g, wq, qg, kmt, vm, wxo)


def _rope_tables(positions):
    half = ROT_DIM // 2
    inv_freq = ROPE_THETA ** (-jnp.arange(0, ROT_DIM, 2, dtype=F32) / ROT_DIM)
    ang = positions.astype(F32)[:, None] * inv_freq
    cos, sin = jnp.cos(ang), jnp.sin(ang)
    n = positions.shape[0]
    rest = DIFF_QKDIM - ROT_DIM
    rc = [cos, cos, jnp.ones((n, rest), F32)]
    ra = [-sin, jnp.zeros((n, half + rest), F32)]
    rb = [jnp.zeros((n, half), F32), sin, jnp.zeros((n, rest), F32)]
    rep = LANES // DIFF_QKDIM
    return jnp.concatenate(rc * rep + ra * rep + rb * rep, axis=1)


def kernel(x, mem, positions, ffn1_norm_g, ffn1_w_gate, ffn1_w_up, ffn1_w_down, mix_norm_g, w_in, q_norm_g, k_norm_g, lambda_q1, lambda_k1, lambda_q2, lambda_k2, diff_head_norm_g, conv_w, conv_b, conv_ln_g, conv_ln_b, w_out, xattn_norm_g, mem_norm_g, xattn_w_q, xattn_w_kv, xattn_q_norm_g, xattn_k_norm_g, xattn_w_o, ffn2_norm_g, ffn2_w_gate, ffn2_w_up, ffn2_w_down):
    b, s, d = x.shape
    assert b == 1 and ffn1_norm_g.shape[0] == 1
    bf = lambda w: w[0].astype(BF16)
    row = lambda v: v[0][None, :]

    rope = _rope_tables(positions[0])
    width = DIFF_HEADS * DIFF_VDIM
    qg = jnp.tile(q_norm_g[0], width // DIFF_QKDIM)[None, :]
    kg = jnp.tile(k_norm_g[0], width // DIFF_QKDIM)[None, :]
    blk = jnp.arange(MXU_DIM) // DIFF_QKDIM
    bd = (blk[:, None] == blk[None, :]).astype(BF16)

    n_i, n_j = s // FFN_TM, ffn1_w_gate.shape[-1] // FFN_TF

    def by_rows(w, scale=1.0):
        r, c = w.shape
        last = c // FFN_TF - 1
        return _HostedCast(w, (r // n_i, FFN_TF), lambda i, j: (i, jnp.minimum(j, last)), scale)

    def by_cols(w, scale=1.0):
        r, c = w.shape
        return _HostedCast(w, (r // n_j, c // n_i), lambda i, j: (j, i), scale)

    x1, wg2, wu2, wd2_half, w_in_bf = _ffn(
        x[0], row(ffn1_norm_g), bf(ffn1_w_gate), bf(ffn1_w_up), (0.5 * ffn1_w_down[0]).astype(BF16),
        [by_rows(ffn2_w_gate[0]), by_rows(ffn2_w_up[0]), by_cols(ffn2_w_down[0], 0.5), by_rows(w_in[0])])

    n_p = s // PROJ_TM
    row_block = lambda w: _HostedCast(w, (w.shape[0] // n_p, w.shape[1]), lambda i: (i, 0))
    qk, vt, u, w_out_bf, wq_bf, wkv_bf, wo_bf = _proj(
        x1, row(mix_norm_g), w_in_bf, qg, kg, bd, rope,
        [row_block(w_out[0]), row_block(xattn_w_q[0]), row_block(xattn_w_kv[0]), row_block(xattn_w_o[0])])
    o_diff = _attn(qk, vt, row(lambda_q1), row(lambda_k1), row(lambda_q2), row(lambda_k2),
                   row(diff_head_norm_g))
    conv_taps = jnp.broadcast_to(conv_w[0][:, None, :], (CONV_K, _SUB, conv_w.shape[-1]))
    conv_bias = jnp.broadcast_to(conv_b[0][None, :], (_SUB, conv_b.shape[-1]))
    y_conv = _conv(u, conv_taps, conv_bias)

    km, vm = _memkv(mem[0], row(mem_norm_g), wkv_bf, row(xattn_k_norm_g))
    x3 = _post(x1, o_diff, y_conv, row(conv_ln_g), row(conv_ln_b), w_out_bf, row(xattn_norm_g), wq_bf,
               row(xattn_q_norm_g), km, vm, wo_bf)

    out = _ffn(x3, row(ffn2_norm_g), wg2, wu2, wd2_half)[0]
    return out[None]
```

```python
import functools
import math
from typing import Callable, NamedTuple

import jax
import jax.numpy as jnp
from jax import lax
from jax.experimental import pallas as pl
from jax.experimental.pallas import tpu as pltpu

F32 = jnp.float32
BF16 = jnp.bfloat16

EPS = 1e-6
ROPE_THETA = 500000.0
DIFF_HEADS = 8
DIFF_VDIM = 128
DIFF_QKDIM = 64
ROT_DIM = 16
CONV_K = 31
XATTN_HEADS = 4
XATTN_HDIM = 128
LAMBDA_INIT = 0.8 - 0.6 * math.exp(0.0)

LANES = 128
MXU_DIM = 256
VMEM_LIMIT = 56 * 1024 * 1024

NEG_BIG = -1e30


def _rms_rows(x, g):
    ms = jnp.mean(x * x, axis=-1, keepdims=True)
    return x * lax.rsqrt(ms + EPS) * g


class _HostedCast(NamedTuple):
    weight: jax.Array
    block: tuple
    index_map: Callable
    scale: float = 1.0


def _cast_blocks(srcs, dsts, scales):
    for src, dst, scale in zip(srcs, dsts, scales):
        v = src[...]
        dst[...] = (v if scale == 1.0 else v * scale).astype(BF16)


FFN_TM = 1024
FFN_TF = 512
_FFN_ROWS = 256


def _ffn_kernel(*refs, n_casts, cast_scales):
    x_ref, g_ref, wg_ref, wu_ref, wd_ref = refs[:5]
    cast_in = refs[5:5 + n_casts]
    o_ref = refs[5 + n_casts]
    cast_out = refs[6 + n_casts:6 + 2 * n_casts]
    h_ref = refs[6 + 2 * n_casts]
    j = pl.program_id(1)

    n_chunks = o_ref.shape[0] // _FFN_ROWS

    def body(first):
        def gate_up(r):
            rows = slice(r * _FFN_ROWS, (r + 1) * _FFN_ROWS)
            if first:
                h = _rms_rows(x_ref[rows, :], g_ref[...]).astype(BF16)
                h_ref[rows, :] = h
            else:
                h = h_ref[rows, :]
            a = jnp.dot(h, wg_ref[...], preferred_element_type=F32)
            b = jnp.dot(h, wu_ref[...], preferred_element_type=F32)
            return a, b

        ab = gate_up(0)
        for r in range(n_chunks):
            a, b = ab
            if r + 1 < n_chunks:
                ab = gate_up(r + 1)
            act = (a * jax.nn.sigmoid(a) * b).astype(BF16)
            rows = slice(r * _FFN_ROWS, (r + 1) * _FFN_ROWS)
            base = x_ref[rows, :] if first else o_ref[rows, :]
            o_ref[rows, :] = base + jnp.dot(act, wd_ref[...], preferred_element_type=F32)
        _cast_blocks(cast_in, cast_out, cast_scales)

    pl.when(j == 0)(lambda: body(True))
    pl.when(j > 0)(lambda: body(False))


def _ffn(x, g, wg, wu, wd_half, casts=(), *, tm=FFN_TM, tf=FFN_TF):
    s, d = x.shape
    f = wg.shape[1]
    cast_specs = [pl.BlockSpec(c.block, c.index_map) for c in casts]
    outs = pl.pallas_call(
        functools.partial(_ffn_kernel, n_casts=len(casts), cast_scales=tuple(c.scale for c in casts)),
        grid=(s // tm, f // tf),
        in_specs=[
            pl.BlockSpec((tm, d), lambda i, j: (i, 0)),
            pl.BlockSpec((1, d), lambda i, j: (0, 0)),
            pl.BlockSpec((d, tf), lambda i, j: (0, j)),
            pl.BlockSpec((d, tf), lambda i, j: (0, j)),
            pl.BlockSpec((tf, d), lambda i, j: (j, 0)),
        ] + cast_specs,
        out_specs=[pl.BlockSpec((tm, d), lambda i, j: (i, 0))] + cast_specs,
        out_shape=[jax.ShapeDtypeStruct((s, d), F32)]
        + [jax.ShapeDtypeStruct(c.weight.shape, BF16) for c in casts],
        scratch_shapes=[pltpu.VMEM((tm, d), BF16)],
        compiler_params=pltpu.CompilerParams(
            dimension_semantics=("arbitrary", "arbitrary"), vmem_limit_bytes=VMEM_LIMIT),
        name="ffn",
    )(x, g, wg, wu, wd_half, *[c.weight for c in casts])
    return outs


def _group_sumsq(p, bd_ref):
    p2 = (p * p).astype(BF16)
    outs = [jnp.dot(p2[:, c * MXU_DIM:(c + 1) * MXU_DIM], bd_ref[...], preferred_element_type=F32)
            for c in range(p.shape[1] // MXU_DIM)]
    return jnp.concatenate(outs, axis=1)


def _qk_norm_rope(p, gain, bd_ref, rc_ref, ra_ref, rb_ref, scale):
    ss = _group_sumsq(p, bd_ref)
    xn = p * lax.rsqrt(ss * (1.0 / DIFF_QKDIM) + EPS) * gain
    rc, ra, rb = rc_ref[...], ra_ref[...], rb_ref[...]
    outs = []
    for c in range(p.shape[1] // LANES):
        xc = xn[:, c * LANES:(c + 1) * LANES]
        up = pltpu.roll(xc, LANES - ROT_DIM // 2, axis=1)
        dn = pltpu.roll(xc, ROT_DIM // 2, axis=1)
        outs.append(xc * rc + up * ra + dn * rb)
    out = jnp.concatenate(outs, axis=1)
    if scale != 1.0:
        out = out * scale
    return out.astype(BF16)


PROJ_TM = 512
_PROJ_ROWS = 256


def _proj_kernel(*refs, n_casts, cast_scales):
    x_ref, g_ref, w_ref, qg_ref, kg_ref, bd_ref, rc_ref, ra_ref, rb_ref = refs[:9]
    cast_in = refs[9:9 + n_casts]
    qk_ref, vt_ref, u_ref = refs[9 + n_casts:12 + n_casts]
    cast_out = refs[12 + n_casts:]
    _cast_blocks(cast_in, cast_out, cast_scales)
    w = u_ref.shape[1]
    q_scale = math.log2(math.e) / math.sqrt(DIFF_QKDIM)
    for r in range(x_ref.shape[0] // _PROJ_ROWS):
        rows = slice(r * _PROJ_ROWS, (r + 1) * _PROJ_ROWS)
        h = _rms_rows(x_ref[rows, :], g_ref[...]).astype(BF16)
        rope = (rc_ref.at[rows, :], ra_ref.at[rows, :], rb_ref.at[rows, :])
        col = lambda n: jnp.dot(h, w_ref[:, n * w:(n + 1) * w], preferred_element_type=F32)
        qk_ref[0, rows, :] = _qk_norm_rope(col(0), qg_ref[...], bd_ref, *rope, q_scale)
        qk_ref[1, rows, :] = _qk_norm_rope(col(1), kg_ref[...], bd_ref, *rope, 1.0)
        vt_ref[:, rows] = col(2).T.astype(BF16)
        u_ref[rows, :] = col(3) * jax.nn.sigmoid(col(4))


def _proj(x, g, w_in, qg, kg, bd, rc, ra, rb, casts=(), *, tm=PROJ_TM):
    s, d = x.shape
    n_cols = w_in.shape[1]
    w = n_cols // 5
    once = pl.Buffered(1)
    cast_specs = [pl.BlockSpec(c.block, c.index_map) for c in casts]
    return pl.pallas_call(
        functools.partial(_proj_kernel, n_casts=len(casts), cast_scales=tuple(c.scale for c in casts)),
        grid=(s // tm,),
        in_specs=[
            pl.BlockSpec((tm, d), lambda i: (i, 0)),
            pl.BlockSpec((1, d), lambda i: (0, 0)),
            pl.BlockSpec((d, n_cols), lambda i: (0, 0), pipeline_mode=once),
            pl.BlockSpec((1, w), lambda i: (0, 0)),
            pl.BlockSpec((1, w), lambda i: (0, 0)),
            pl.BlockSpec((MXU_DIM, MXU_DIM), lambda i: (0, 0)),
            pl.BlockSpec((tm, LANES), lambda i: (i, 0)),
            pl.BlockSpec((tm, LANES), lambda i: (i, 0)),
            pl.BlockSpec((tm, LANES), lambda i: (i, 0)),
        ] + cast_specs,
        out_specs=[
            pl.BlockSpec((2, tm, w), lambda i: (0, i, 0)),
            pl.BlockSpec((w, tm), lambda i: (0, i)),
            pl.BlockSpec((tm, w), lambda i: (i, 0)),
        ] + cast_specs,
        out_shape=[
            jax.ShapeDtypeStruct((2, s, w), BF16),
            jax.ShapeDtypeStruct((w, s), BF16),
            jax.ShapeDtypeStruct((s, w), F32),
        ] + [jax.ShapeDtypeStruct(c.weight.shape, BF16) for c in casts],
        compiler_params=pltpu.CompilerParams(
            dimension_semantics=("arbitrary",), vmem_limit_bytes=VMEM_LIMIT),
        name="proj",
    )(x, g, w_in, qg, kg, bd, rc, ra, rb, *[c.weight for c in casts])


_ONES_ROWS = 16
_CHAIN_GROUP = 2


def _attn_kernel(q_ref, k_ref, vt_ref, lq1_ref, lk1_ref, lq2_ref, lk2_ref, hg_ref, o_ref,
                 acc1, acc2, m1, m2, s_ref, *, t, ts):
    i = pl.program_id(1)
    qt = q_ref[...].astype(F32).T
    row = lax.broadcasted_iota(jnp.int32, qt.shape, 0)
    qz = (jnp.where(row < DIFF_QKDIM, qt, 0.0).astype(BF16), jnp.where(row >= DIFF_QKDIM, qt, 0.0).astype(BF16))
    accs, ms = (acc1, acc2), (m1, m2)
    ones = jnp.ones((_ONES_ROWS, t), BF16)
    chains = [(st, c) for st in range(t // ts) for c in range(2)]

    def score(j, chain, diagonal):
        st, c = chain
        start = pl.multiple_of(j * t, t)
        nk = (st + 1) * ts if diagonal else t
        kb = k_ref[pl.ds(start, nk), :]
        s = jnp.dot(kb, qz[c][:, st * ts:(st + 1) * ts], preferred_element_type=F32)
        if diagonal:
            krow = lax.broadcasted_iota(jnp.int32, s.shape, 0)
            qcol = lax.broadcasted_iota(jnp.int32, s.shape, 1) + st * ts
            s = jnp.where(krow <= qcol, s, NEG_BIG)
        s_ref[2 * st + c, 0:nk, :] = s
        return jnp.max(s, axis=0, keepdims=True), nk

    def softmax(s_max, chain, diagonal):
        st, c = chain
        cols = slice(st * ts, (st + 1) * ts)
        m_blk, nk = s_max
        if diagonal:
            m_new, alpha = m_blk, None
        else:
            m_old = ms[c][:, cols]
            m_new = jnp.maximum(m_old, m_blk)
            alpha = jnp.exp2(m_old - m_new)
        ms[c][:, cols] = m_new
        return jnp.exp2(s_ref[2 * st + c, 0:nk, :] - m_new).astype(BF16), alpha

    def pv(j, chain, prob):
        st, c = chain
        p, alpha = prob
        nk = p.shape[0]
        start = pl.multiple_of(j * t, t)
        cols = slice(st * ts, (st + 1) * ts)
        vb = jnp.concatenate([vt_ref[:, pl.ds(start, nk)], ones[:, :nk]], axis=0)
        acc_new = jnp.dot(vb, p, preferred_element_type=F32)
        accs[c][st] = acc_new if alpha is None else alpha * accs[c][st] + acc_new

    def blocks(js, diagonal):
        groups = [(j, chains[g:g + _CHAIN_GROUP]) for j in js for g in range(0, len(chains), _CHAIN_GROUP)]
        j0, first = groups[0]
        maxima = [score(j0, chain, diagonal) for chain in first]
        for g, (j, group) in enumerate(groups):
            nxt = []
            if g + 1 < len(groups):
                j_nxt, group_nxt = groups[g + 1]
                nxt = [score(j_nxt, chain, diagonal) for chain in group_nxt]
            probs = [softmax(mx, chain, diagonal) for chain, mx in zip(group, maxima)]
            for chain, prob in zip(group, probs):
                pv(j, chain, prob)
            maxima = nxt

    blocks([i], True)

    def body(n, carry):
        blocks([2 * n, 2 * n + 1], False)
        return carry

    lax.fori_loop(0, i // 2, body, 0)
    pl.when(i % 2 == 1)(lambda: blocks([i - 1], False))

    lam = (jnp.exp(jnp.sum(lq1_ref[...] * lk1_ref[...], axis=-1, keepdims=True))
           - jnp.exp(jnp.sum(lq2_ref[...] * lk2_ref[...], axis=-1, keepdims=True))
           + LAMBDA_INIT)
    nv = DIFF_VDIM
    for st in range(t // ts):
        ot = (acc1[st, 0:nv, :] / acc1[st, nv:nv + 1, :]
              - lam * (acc2[st, 0:nv, :] / acc2[st, nv:nv + 1, :]))
        mean_sq = jnp.mean(ot * ot, axis=0, keepdims=True)
        on = ot * lax.rsqrt(mean_sq + EPS)
        o_ref[st * ts:(st + 1) * ts, :] = (on.T * hg_ref[...] * (1.0 - LAMBDA_INIT)).astype(BF16)


def _attn(qk, vt, lq1, lk1, lq2, lk2, hg, *, t=1024, ts=256):
    _, s, w = qk.shape
    nh = w // DIFF_VDIM
    vec = lambda n: pl.BlockSpec((1, n), lambda h, i: (0, 0))
    return pl.pallas_call(
        functools.partial(_attn_kernel, t=t, ts=ts),
        grid=(nh, s // t),
        in_specs=[
            pl.BlockSpec((None, t, DIFF_VDIM), lambda h, i: (0, i, h)),
            pl.BlockSpec((None, s, DIFF_VDIM), lambda h, i: (1, 0, h)),
            pl.BlockSpec((DIFF_VDIM, s), lambda h, i: (h, 0)),
            vec(DIFF_QKDIM), vec(DIFF_QKDIM), vec(DIFF_QKDIM), vec(DIFF_QKDIM),
            vec(DIFF_VDIM),
        ],
        out_specs=pl.BlockSpec((t, DIFF_VDIM), lambda h, i: (i, h)),
        out_shape=jax.ShapeDtypeStruct((s, w), BF16),
        scratch_shapes=[
            pltpu.VMEM((t // ts, DIFF_VDIM + _ONES_ROWS, ts), F32),
            pltpu.VMEM((t // ts, DIFF_VDIM + _ONES_ROWS, ts), F32),
            pltpu.VMEM((1, t), F32), pltpu.VMEM((1, t), F32),
            pltpu.VMEM((2 * (t // ts), t, ts), F32),
        ],
        compiler_params=pltpu.CompilerParams(
            dimension_semantics=("arbitrary", "arbitrary"), vmem_limit_bytes=VMEM_LIMIT),
        name="attn",
    )(qk, qk, vt, lq1, lk1, lq2, lk2, hg)


_HALO = 32
_SUB = 8
_ROWS = 32


def _conv_kernel(cur_ref, halo_ref, w_ref, b_ref, y_ref, e_ref, *, tc):
    i = pl.program_id(0)
    e_ref[0, 0:_HALO, :] = jnp.where(i > 0, halo_ref[...], 0.0)
    e_ref[0, _HALO:_HALO + tc, :] = cur_ref[...]
    span = tc + _HALO - _SUB
    for r in range(1, _SUB):
        e_ref[r, 0:span, :] = e_ref[0, r:r + span, :]

    first = _HALO - (CONV_K - 1)
    n_sub = _ROWS // _SUB

    def chunk(c, carry):
        t0 = pl.multiple_of(c * _ROWS, _ROWS)
        acc = [b_ref[...] for _ in range(n_sub)]
        for r in range(_SUB):
            taps_r = [(k, (first + k) // _SUB) for k in range(CONV_K) if (first + k) % _SUB == r]
            weights = {k: w_ref[k] for k, _ in taps_r}
            for idx in range(max(a for _, a in taps_r) + n_sub):
                uses = [(k, idx - a) for k, a in taps_r if 0 <= idx - a < n_sub]
                if not uses:
                    continue
                tile = e_ref[r, pl.ds(t0 + idx * _SUB, _SUB), :]
                for k, q in uses:
                    acc[q] = acc[q] + weights[k] * tile
        for q in range(n_sub):
            y_ref[pl.ds(t0 + q * _SUB, _SUB), :] = acc[q]
        return carry

    lax.fori_loop(0, tc // _ROWS, chunk, 0, unroll=2)


def _conv(u, taps, bias, *, tc=512):
    s, c = u.shape
    halo_blocks = tc // _HALO
    return pl.pallas_call(
        functools.partial(_conv_kernel, tc=tc),
        grid=(s // tc,),
        in_specs=[
            pl.BlockSpec((tc, c), lambda i: (i, 0)),
            pl.BlockSpec((_HALO, c), lambda i: (jnp.maximum(i * halo_blocks - 1, 0), 0)),
            pl.BlockSpec((CONV_K, _SUB, c), lambda i: (0, 0, 0)),
            pl.BlockSpec((_SUB, c), lambda i: (0, 0)),
        ],
        out_specs=pl.BlockSpec((tc, c), lambda i: (i, 0)),
        out_shape=jax.ShapeDtypeStruct((s, c), F32),
        scratch_shapes=[pltpu.VMEM((_SUB, tc + _HALO, c), F32)],
        compiler_params=pltpu.CompilerParams(
            dimension_semantics=("arbitrary",), vmem_limit_bytes=VMEM_LIMIT),
        name="conv",
    )(u, u, taps, bias)


def _memkv_kernel(mem_ref, g_ref, w_ref, kg_ref, k_ref, v_ref):
    h = _rms_rows(mem_ref[...], g_ref[...]).astype(BF16)
    kv = jnp.dot(h, w_ref[...], preferred_element_type=F32)
    width = v_ref.shape[1]
    for hd in range(XATTN_HEADS):
        sl = slice(hd * XATTN_HDIM, (hd + 1) * XATTN_HDIM)
        k_ref[sl, :] = _rms_rows(kv[:, sl], kg_ref[...]).T.astype(BF16)
    v_ref[...] = kv[:, width:].astype(BF16)


def _memkv(mem, g, w_kv, kg):
    n, _ = mem.shape
    width = w_kv.shape[1] // 2
    return pl.pallas_call(
        _memkv_kernel,
        out_shape=[jax.ShapeDtypeStruct((width, n), BF16), jax.ShapeDtypeStruct((n, width), BF16)],
        compiler_params=pltpu.CompilerParams(vmem_limit_bytes=VMEM_LIMIT),
        name="memkv",
    )(mem, g, w_kv, kg)


def _post_kernel(x_ref, od_ref, y_ref, lng_ref, lnb_ref, wo_ref, g_ref, wq_ref, qg_ref, kmt_ref, vm_ref,
                 wxo_ref, o_ref):
    half = od_ref.shape[1]
    x2 = x_ref[...] + jnp.dot(od_ref[...], wo_ref[0:half, :], preferred_element_type=F32)
    y = y_ref[...]
    yc = y - jnp.mean(y, axis=-1, keepdims=True)
    var = jnp.mean(yc * yc, axis=-1, keepdims=True)
    yn = yc * lax.rsqrt(var + EPS) * lng_ref[...] + lnb_ref[...]
    oc = (yn * jax.nn.sigmoid(yn)).astype(BF16)
    x2 = x2 + jnp.dot(oc, wo_ref[half:, :], preferred_element_type=F32)
    h = _rms_rows(x2, g_ref[...]).astype(BF16)
    q = jnp.dot(h, wq_ref[...], preferred_element_type=F32)
    scale = 1.0 / math.sqrt(XATTN_HDIM)
    outs = []
    for hd in range(XATTN_HEADS):
        sl = slice(hd * XATTN_HDIM, (hd + 1) * XATTN_HDIM)
        qn = _rms_rows(q[:, sl], qg_ref[...]).astype(BF16)
        sc = jnp.dot(qn, kmt_ref[sl, :], preferred_element_type=F32) * scale
        sc = sc - jnp.max(sc, axis=-1, keepdims=True)
        p = jnp.exp(sc)
        p = p / jnp.sum(p, axis=-1, keepdims=True)
        outs.append(jnp.dot(p.astype(BF16), vm_ref[:, sl], preferred_element_type=F32))
    o = jnp.concatenate(outs, axis=1).astype(BF16)
    o_ref[...] = x2 + jnp.dot(o, wxo_ref[...], preferred_element_type=F32)


def _post(x, od, y_conv, ln_g, ln_b, w_out, g, wq, qg, kmt, vm, wxo, *, tm=512):
    s, d = x.shape
    half = od.shape[1]
    xw = wq.shape[1]
    n_mem = vm.shape[0]
    const = lambda shape: pl.BlockSpec(shape, lambda i: (0, 0))
    return pl.pallas_call(
        _post_kernel,
        grid=(s // tm,),
        in_specs=[
            pl.BlockSpec((tm, d), lambda i: (i, 0)),
            pl.BlockSpec((tm, half), lambda i: (i, 0)),
            pl.BlockSpec((tm, half), lambda i: (i, 0)),
            const((1, half)), const((1, half)),
            const((2 * half, d)), const((1, d)), const((d, xw)), const((1, XATTN_HDIM)),
            const((xw, n_mem)), const((n_mem, xw)), const((xw, d)),
        ],
        out_specs=pl.BlockSpec((tm, d), lambda i: (i, 0)),
        out_shape=jax.ShapeDtypeStruct((s, d), F32),
        compiler_params=pltpu.CompilerParams(
            dimension_semantics=("arbitrary",), vmem_limit_bytes=VMEM_LIMIT),
        name="post",
    )(x, od, y_conv, ln_g, ln_b, w_out, g, wq, qg, kmt, vm, wxo)


def _rope_tables(positions):
    half = ROT_DIM // 2
    inv_freq = ROPE_THETA ** (-jnp.arange(0, ROT_DIM, 2, dtype=F32) / ROT_DIM)
    ang = positions.astype(F32)[:, None] * inv_freq
    cos, sin = jnp.cos(ang), jnp.sin(ang)
    n = positions.shape[0]
    rest = DIFF_QKDIM - ROT_DIM
    rc = jnp.concatenate([cos, cos, jnp.ones((n, rest), F32)], axis=1)
    ra = jnp.concatenate([-sin, jnp.zeros((n, half + rest), F32)], axis=1)
    rb = jnp.concatenate([jnp.zeros((n, half), F32), sin, jnp.zeros((n, rest), F32)], axis=1)
    rep = LANES // DIFF_QKDIM
    return jnp.tile(rc, (1, rep)), jnp.tile(ra, (1, rep)), jnp.tile(rb, (1, rep))


def kernel(x, mem, positions, ffn1_norm_g, ffn1_w_gate, ffn1_w_up, ffn1_w_down, mix_norm_g, w_in, q_norm_g, k_norm_g, lambda_q1, lambda_k1, lambda_q2, lambda_k2, diff_head_norm_g, conv_w, conv_b, conv_ln_g, conv_ln_b, w_out, xattn_norm_g, mem_norm_g, xattn_w_q, xattn_w_kv, xattn_q_norm_g, xattn_k_norm_g, xattn_w_o, ffn2_norm_g, ffn2_w_gate, ffn2_w_up, ffn2_w_down):
    b, s, d = x.shape
    assert b == 1 and ffn1_norm_g.shape[0] == 1
    bf = lambda w: w[0].astype(BF16)
    row = lambda v: v[0][None, :]

    rc, ra, rb = _rope_tables(positions[0])
    width = DIFF_HEADS * DIFF_VDIM
    qg = jnp.tile(q_norm_g[0], width // DIFF_QKDIM)[None, :]
    kg = jnp.tile(k_norm_g[0], width // DIFF_QKDIM)[None, :]
    blk = jnp.arange(MXU_DIM) // DIFF_QKDIM
    bd = (blk[:, None] == blk[None, :]).astype(BF16)

    n_i, n_j = s // FFN_TM, ffn1_w_gate.shape[-1] // FFN_TF

    def by_rows(w, scale=1.0):
        r, c = w.shape
        last = c // FFN_TF - 1
        return _HostedCast(w, (r // n_i, FFN_TF), lambda i, j: (i, jnp.minimum(j, last)), scale)

    def by_cols(w, scale=1.0):
        r, c = w.shape
        return _HostedCast(w, (r // n_j, c // n_i), lambda i, j: (j, i), scale)

    x1, wg2, wu2, wd2_half, w_in_bf = _ffn(
        x[0], row(ffn1_norm_g), bf(ffn1_w_gate), bf(ffn1_w_up), (0.5 * ffn1_w_down[0]).astype(BF16),
        [by_rows(ffn2_w_gate[0]), by_rows(ffn2_w_up[0]), by_cols(ffn2_w_down[0], 0.5), by_rows(w_in[0])])

    n_p = s // PROJ_TM
    row_block = lambda w: _HostedCast(w, (w.shape[0] // n_p, w.shape[1]), lambda i: (i, 0))
    qk, vt, u, w_out_bf, wq_bf, wkv_bf, wo_bf = _proj(
        x1, row(mix_norm_g), w_in_bf, qg, kg, bd, rc, ra, rb,
        [row_block(w_out[0]), row_block(xattn_w_q[0]), row_block(xattn_w_kv[0]), row_block(xattn_w_o[0])])
    o_diff = _attn(qk, vt, row(lambda_q1), row(lambda_k1), row(lambda_q2), row(lambda_k2),
                   row(diff_head_norm_g))
    conv_taps = jnp.broadcast_to(conv_w[0][:, None, :], (CONV_K, _SUB, conv_w.shape[-1]))
    conv_bias = jnp.broadcast_to(conv_b[0][None, :], (_SUB, conv_b.shape[-1]))
    y_conv = _conv(u, conv_taps, conv_bias)

    km, vm = _memkv(mem[0], row(mem_norm_g), wkv_bf, row(xattn_k_norm_g))
    x3 = _post(x1, o_diff, y_conv, row(conv_ln_g), row(conv_ln_b), w_out_bf, row(xattn_norm_g), wq_bf,
               row(xattn_q_norm_g), km, vm, wo_bf)

    out = _ffn(x3, row(ffn2_norm_g), wg2, wu2, wd2_half)[0]
    return out[None]
```

```python
import functools
import math
from typing import Callable, NamedTuple

import jax
import jax.numpy as jnp
from jax import lax
from jax.experimental import pallas as pl
from jax.experimental.pallas import tpu as pltpu

F32 = jnp.float32
BF16 = jnp.bfloat16

EPS = 1e-6
ROPE_THETA = 500000.0
DIFF_HEADS = 8
DIFF_VDIM = 128
DIFF_QKDIM = 64
ROT_DIM = 16
CONV_K = 31
XATTN_HEADS = 4
XATTN_HDIM = 128
LAMBDA_INIT = 0.8 - 0.6 * math.exp(0.0)

LANES = 128
_BF16_ROWS = 16
MXU_DIM = 256
VMEM_LIMIT = 56 * 1024 * 1024

NEG_BIG = -1e30


def _rms_rows(x, g):
    ms = jnp.mean(x * x, axis=-1, keepdims=True)
    return x * lax.rsqrt(ms + EPS) * g


class _HostedCast(NamedTuple):
    weight: jax.Array
    block: tuple
    index_map: Callable
    scale: float = 1.0


def _cast_blocks(srcs, dsts, scales):
    for src, dst, scale in zip(srcs, dsts, scales):
        v = src[...]
        dst[...] = (v if scale == 1.0 else v * scale).astype(BF16)


FFN_TM = 1024
FFN_TF = 512
_FFN_ROWS = 256


def _ffn_kernel(*refs, n_casts, cast_scales):
    x_ref, g_ref, wg_ref, wu_ref, wd_ref = refs[:5]
    cast_in = refs[5:5 + n_casts]
    o_ref = refs[5 + n_casts]
    cast_out = refs[6 + n_casts:6 + 2 * n_casts]
    h_ref = refs[6 + 2 * n_casts]
    j = pl.program_id(1)

    n_chunks = o_ref.shape[0] // _FFN_ROWS

    def body(first):
        def gate_up(r):
            rows = slice(r * _FFN_ROWS, (r + 1) * _FFN_ROWS)
            if first:
                h = _rms_rows(x_ref[rows, :], g_ref[...]).astype(BF16)
                h_ref[rows, :] = h
            else:
                h = h_ref[rows, :]
            a = jnp.dot(h, wg_ref[...], preferred_element_type=F32)
            b = jnp.dot(h, wu_ref[...], preferred_element_type=F32)
            return a, b

        ab = gate_up(0)
        for r in range(n_chunks):
            a, b = ab
            if r + 1 < n_chunks:
                ab = gate_up(r + 1)
            act = (a * jax.nn.sigmoid(a) * b).astype(BF16)
            rows = slice(r * _FFN_ROWS, (r + 1) * _FFN_ROWS)
            base = x_ref[rows, :] if first else o_ref[rows, :]
            o_ref[rows, :] = base + jnp.dot(act, wd_ref[...], preferred_element_type=F32)
        _cast_blocks(cast_in, cast_out, cast_scales)

    pl.when(j == 0)(lambda: body(True))
    pl.when(j > 0)(lambda: body(False))


def _ffn(x, g, wg, wu, wd_half, casts=(), *, tm=FFN_TM, tf=FFN_TF):
    s, d = x.shape
    f = wg.shape[1]
    cast_specs = [pl.BlockSpec(c.block, c.index_map) for c in casts]
    outs = pl.pallas_call(
        functools.partial(_ffn_kernel, n_casts=len(casts), cast_scales=tuple(c.scale for c in casts)),
        grid=(s // tm, f // tf),
        in_specs=[
            pl.BlockSpec((tm, d), lambda i, j: (i, 0)),
            pl.BlockSpec((1, d), lambda i, j: (0, 0)),
            pl.BlockSpec((d, tf), lambda i, j: (0, j)),
            pl.BlockSpec((d, tf), lambda i, j: (0, j)),
            pl.BlockSpec((tf, d), lambda i, j: (j, 0)),
        ] + cast_specs,
        out_specs=[pl.BlockSpec((tm, d), lambda i, j: (i, 0))] + cast_specs,
        out_shape=[jax.ShapeDtypeStruct((s, d), F32)]
        + [jax.ShapeDtypeStruct(c.weight.shape, BF16) for c in casts],
        scratch_shapes=[pltpu.VMEM((tm, d), BF16)],
        compiler_params=pltpu.CompilerParams(
            dimension_semantics=("arbitrary", "arbitrary"), vmem_limit_bytes=VMEM_LIMIT),
        name="ffn",
    )(x, g, wg, wu, wd_half, *[c.weight for c in casts])
    return outs


def _group_sumsq(p, bd_ref):
    p2 = (p * p).astype(BF16)
    outs = [jnp.dot(p2[:, c * MXU_DIM:(c + 1) * MXU_DIM], bd_ref[...], preferred_element_type=F32)
            for c in range(p.shape[1] // MXU_DIM)]
    return jnp.concatenate(outs, axis=1)


def _qk_norm_rope(p, gain, bd_ref, rc_ref, ra_ref, rb_ref, scale):
    ss = _group_sumsq(p, bd_ref)
    xn = p * lax.rsqrt(ss * (1.0 / DIFF_QKDIM) + EPS) * gain
    rc, ra, rb = rc_ref[...], ra_ref[...], rb_ref[...]
    outs = []
    for c in range(p.shape[1] // LANES):
        xc = xn[:, c * LANES:(c + 1) * LANES]
        up = pltpu.roll(xc, LANES - ROT_DIM // 2, axis=1)
        dn = pltpu.roll(xc, ROT_DIM // 2, axis=1)
        outs.append(xc * rc + up * ra + dn * rb)
    out = jnp.concatenate(outs, axis=1)
    if scale != 1.0:
        out = out * scale
    return out.astype(BF16)


PROJ_TM = 512
_PROJ_ROWS = 256


def _proj_kernel(*refs, n_casts, cast_scales):
    x_ref, g_ref, w_ref, qg_ref, kg_ref, bd_ref, rc_ref, ra_ref, rb_ref = refs[:9]
    cast_in = refs[9:9 + n_casts]
    qk_ref, vt_ref, u_ref = refs[9 + n_casts:12 + n_casts]
    cast_out = refs[12 + n_casts:]
    _cast_blocks(cast_in, cast_out, cast_scales)
    w = u_ref.shape[1]
    q_scale = math.log2(math.e) / math.sqrt(DIFF_QKDIM)
    for r in range(x_ref.shape[0] // _PROJ_ROWS):
        rows = slice(r * _PROJ_ROWS, (r + 1) * _PROJ_ROWS)
        h = _rms_rows(x_ref[rows, :], g_ref[...]).astype(BF16)
        rope = (rc_ref.at[rows, :], ra_ref.at[rows, :], rb_ref.at[rows, :])
        col = lambda n: jnp.dot(h, w_ref[:, n * w:(n + 1) * w], preferred_element_type=F32)
        qk_ref[0, rows, :] = _qk_norm_rope(col(0), qg_ref[...], bd_ref, *rope, q_scale)
        qk_ref[1, rows, :] = _qk_norm_rope(col(1), kg_ref[...], bd_ref, *rope, 1.0)
        vt_ref[:, rows] = col(2).T.astype(BF16)
        u_ref[rows, :] = col(3) * jax.nn.sigmoid(col(4))


def _proj(x, g, w_in, qg, kg, bd, rc, ra, rb, casts=(), *, tm=PROJ_TM):
    s, d = x.shape
    n_cols = w_in.shape[1]
    w = n_cols // 5
    once = pl.Buffered(1)
    cast_specs = [pl.BlockSpec(c.block, c.index_map) for c in casts]
    return pl.pallas_call(
        functools.partial(_proj_kernel, n_casts=len(casts), cast_scales=tuple(c.scale for c in casts)),
        grid=(s // tm,),
        in_specs=[
            pl.BlockSpec((tm, d), lambda i: (i, 0)),
            pl.BlockSpec((1, d), lambda i: (0, 0)),
            pl.BlockSpec((d, n_cols), lambda i: (0, 0), pipeline_mode=once),
            pl.BlockSpec((1, w), lambda i: (0, 0)),
            pl.BlockSpec((1, w), lambda i: (0, 0)),
            pl.BlockSpec((MXU_DIM, MXU_DIM), lambda i: (0, 0)),
            pl.BlockSpec((tm, LANES), lambda i: (i, 0)),
            pl.BlockSpec((tm, LANES), lambda i: (i, 0)),
            pl.BlockSpec((tm, LANES), lambda i: (i, 0)),
        ] + cast_specs,
        out_specs=[
            pl.BlockSpec((2, tm, w), lambda i: (0, i, 0)),
            pl.BlockSpec((w, tm), lambda i: (0, i)),
            pl.BlockSpec((tm, w), lambda i: (i, 0)),
        ] + cast_specs,
        out_shape=[
            jax.ShapeDtypeStruct((2, s, w), BF16),
            jax.ShapeDtypeStruct((w, s), BF16),
            jax.ShapeDtypeStruct((s, w), F32),
        ] + [jax.ShapeDtypeStruct(c.weight.shape, BF16) for c in casts],
        compiler_params=pltpu.CompilerParams(
            dimension_semantics=("arbitrary",), vmem_limit_bytes=VMEM_LIMIT),
        name="proj",
    )(x, g, w_in, qg, kg, bd, rc, ra, rb, *[c.weight for c in casts])


_ONES_ROWS = 16
_CHAIN_GROUP = 2


def _attn_kernel(q_ref, k_ref, vt_ref, lq1_ref, lk1_ref, lq2_ref, lk2_ref, hg_ref, o_ref,
                 acc1, acc2, m1, m2, s_ref, *, t, ts):
    i = pl.program_id(1)
    qt = q_ref[...].astype(F32).T
    row = lax.broadcasted_iota(jnp.int32, qt.shape, 0)
    qz = (jnp.where(row < DIFF_QKDIM, qt, 0.0).astype(BF16), jnp.where(row >= DIFF_QKDIM, qt, 0.0).astype(BF16))
    accs, ms = (acc1, acc2), (m1, m2)
    ones = jnp.ones((_ONES_ROWS, t), BF16)
    chains = [(st, c) for st in range(t // ts) for c in range(2)]

    def score(j, chain, diagonal):
        st, c = chain
        start = pl.multiple_of(j * t, t)
        nk = (st + 1) * ts if diagonal else t
        kb = k_ref[pl.ds(start, nk), :]
        s = jnp.dot(kb, qz[c][:, st * ts:(st + 1) * ts], preferred_element_type=F32)
        if diagonal:
            krow = lax.broadcasted_iota(jnp.int32, s.shape, 0)
            qcol = lax.broadcasted_iota(jnp.int32, s.shape, 1) + st * ts
            s = jnp.where(krow <= qcol, s, NEG_BIG)
        s_ref[2 * st + c, 0:nk, :] = s
        return jnp.max(s, axis=0, keepdims=True), nk

    def softmax(s_max, chain, diagonal):
        st, c = chain
        cols = slice(st * ts, (st + 1) * ts)
        m_blk, nk = s_max
        if diagonal:
            m_new, alpha = m_blk, None
        else:
            m_old = ms[c][:, cols]
            m_new = jnp.maximum(m_old, m_blk)
            alpha = jnp.exp2(m_old - m_new)
        ms[c][:, cols] = m_new
        return jnp.exp2(s_ref[2 * st + c, 0:nk, :] - m_new).astype(BF16), alpha

    def pv(j, chain, prob):
        st, c = chain
        p, alpha = prob
        nk = p.shape[0]
        start = pl.multiple_of(j * t, t)
        cols = slice(st * ts, (st + 1) * ts)
        vb = jnp.concatenate([vt_ref[:, pl.ds(start, nk)], ones[:, :nk]], axis=0)
        acc_new = jnp.dot(vb, p, preferred_element_type=F32)
        accs[c][st] = acc_new if alpha is None else alpha * accs[c][st] + acc_new

    def blocks(js, diagonal):
        groups = [(j, chains[g:g + _CHAIN_GROUP]) for j in js for g in range(0, len(chains), _CHAIN_GROUP)]
        j0, first = groups[0]
        maxima = [score(j0, chain, diagonal) for chain in first]
        for g, (j, group) in enumerate(groups):
            nxt = []
            if g + 1 < len(groups):
                j_nxt, group_nxt = groups[g + 1]
                nxt = [score(j_nxt, chain, diagonal) for chain in group_nxt]
            probs = [softmax(mx, chain, diagonal) for chain, mx in zip(group, maxima)]
            for chain, prob in zip(group, probs):
                pv(j, chain, prob)
            maxima = nxt

    blocks([i], True)

    def body(n, carry):
        blocks([2 * n, 2 * n + 1], False)
        return carry

    lax.fori_loop(0, i // 2, body, 0)
    pl.when(i % 2 == 1)(lambda: blocks([i - 1], False))

    lam = (jnp.exp(jnp.sum(lq1_ref[...] * lk1_ref[...], axis=-1, keepdims=True))
           - jnp.exp(jnp.sum(lq2_ref[...] * lk2_ref[...], axis=-1, keepdims=True))
           + LAMBDA_INIT)
    nv = DIFF_VDIM
    for st in range(t // ts):
        ot = (acc1[st, 0:nv, :] / acc1[st, nv:nv + 1, :]
              - lam * (acc2[st, 0:nv, :] / acc2[st, nv:nv + 1, :]))
        mean_sq = jnp.mean(ot * ot, axis=0, keepdims=True)
        on = ot * lax.rsqrt(mean_sq + EPS)
        o_ref[st * ts:(st + 1) * ts, :] = (on.T * hg_ref[...] * (1.0 - LAMBDA_INIT)).astype(BF16)


def _attn(qk, vt, lq1, lk1, lq2, lk2, hg, *, t=1024, ts=256):
    _, s, w = qk.shape
    nh = w // DIFF_VDIM
    assert s % t == 0 and t % ts == 0 and ts % MXU_DIM == 0
    vec = lambda n: pl.BlockSpec((1, n), lambda h, i: (0, 0))
    return pl.pallas_call(
        functools.partial(_attn_kernel, t=t, ts=ts),
        grid=(nh, s // t),
        in_specs=[
            pl.BlockSpec((None, t, DIFF_VDIM), lambda h, i: (0, i, h)),
            pl.BlockSpec((None, s, DIFF_VDIM), lambda h, i: (1, 0, h)),
            pl.BlockSpec((DIFF_VDIM, s), lambda h, i: (h, 0)),
            vec(DIFF_QKDIM), vec(DIFF_QKDIM), vec(DIFF_QKDIM), vec(DIFF_QKDIM),
            vec(DIFF_VDIM),
        ],
        out_specs=pl.BlockSpec((t, DIFF_VDIM), lambda h, i: (i, h)),
        out_shape=jax.ShapeDtypeStruct((s, w), BF16),
        scratch_shapes=[
            pltpu.VMEM((t // ts, DIFF_VDIM + _ONES_ROWS, ts), F32),
            pltpu.VMEM((t // ts, DIFF_VDIM + _ONES_ROWS, ts), F32),
            pltpu.VMEM((1, t), F32), pltpu.VMEM((1, t), F32),
            pltpu.VMEM((2 * (t // ts), t, ts), F32),
        ],
        compiler_params=pltpu.CompilerParams(
            dimension_semantics=("arbitrary", "arbitrary"), vmem_limit_bytes=VMEM_LIMIT),
        name="attn",
    )(qk, qk, vt, lq1, lk1, lq2, lk2, hg)


_HALO = 32
_SUB = 8
_ROWS = 32


def _conv_kernel(cur_ref, halo_ref, w_ref, b_ref, y_ref, e_ref, *, tc):
    i = pl.program_id(0)
    e_ref[0, 0:_HALO, :] = jnp.where(i > 0, halo_ref[...], 0.0)
    e_ref[0, _HALO:_HALO + tc, :] = cur_ref[...]
    span = tc + _HALO - _SUB
    for r in range(1, _SUB):
        e_ref[r, 0:span, :] = e_ref[0, r:r + span, :]

    first = _HALO - (CONV_K - 1)
    n_sub = _ROWS // _SUB

    def chunk(c, carry):
        t0 = pl.multiple_of(c * _ROWS, _ROWS)
        acc = [b_ref[...] for _ in range(n_sub)]
        for r in range(_SUB):
            taps_r = [(k, (first + k) // _SUB) for k in range(CONV_K) if (first + k) % _SUB == r]
            weights = {k: w_ref[k] for k, _ in taps_r}
            for idx in range(max(a for _, a in taps_r) + n_sub):
                uses = [(k, idx - a) for k, a in taps_r if 0 <= idx - a < n_sub]
                if not uses:
                    continue
                tile = e_ref[r, pl.ds(t0 + idx * _SUB, _SUB), :]
                for k, q in uses:
                    acc[q] = acc[q] + weights[k] * tile
        for q in range(n_sub):
            y_ref[pl.ds(t0 + q * _SUB, _SUB), :] = acc[q]
        return carry

    lax.fori_loop(0, tc // _ROWS, chunk, 0, unroll=2)


def _conv(u, taps, bias, *, tc=512):
    s, c = u.shape
    assert s % tc == 0 and tc % _HALO == 0 and tc % (2 * _ROWS) == 0 and _HALO >= CONV_K - 1
    halo_blocks = tc // _HALO
    return pl.pallas_call(
        functools.partial(_conv_kernel, tc=tc),
        grid=(s // tc,),
        in_specs=[
            pl.BlockSpec((tc, c), lambda i: (i, 0)),
            pl.BlockSpec((_HALO, c), lambda i: (jnp.maximum(i * halo_blocks - 1, 0), 0)),
            pl.BlockSpec((CONV_K, _SUB, c), lambda i: (0, 0, 0)),
            pl.BlockSpec((_SUB, c), lambda i: (0, 0)),
        ],
        out_specs=pl.BlockSpec((tc, c), lambda i: (i, 0)),
        out_shape=jax.ShapeDtypeStruct((s, c), F32),
        scratch_shapes=[pltpu.VMEM((_SUB, tc + _HALO, c), F32)],
        compiler_params=pltpu.CompilerParams(
            dimension_semantics=("arbitrary",), vmem_limit_bytes=VMEM_LIMIT),
        name="conv",
    )(u, u, taps, bias)


def _memkv_kernel(mem_ref, g_ref, w_ref, kg_ref, k_ref, v_ref):
    h = _rms_rows(mem_ref[...], g_ref[...]).astype(BF16)
    kv = jnp.dot(h, w_ref[...], preferred_element_type=F32)
    width = v_ref.shape[1]
    for hd in range(XATTN_HEADS):
        sl = slice(hd * XATTN_HDIM, (hd + 1) * XATTN_HDIM)
        k_ref[sl, :] = _rms_rows(kv[:, sl], kg_ref[...]).T.astype(BF16)
    v_ref[...] = kv[:, width:].astype(BF16)


def _memkv(mem, g, w_kv, kg):
    n, _ = mem.shape
    width = w_kv.shape[1] // 2
    return pl.pallas_call(
        _memkv_kernel,
        out_shape=[jax.ShapeDtypeStruct((width, n), BF16), jax.ShapeDtypeStruct((n, width), BF16)],
        compiler_params=pltpu.CompilerParams(vmem_limit_bytes=VMEM_LIMIT),
        name="memkv",
    )(mem, g, w_kv, kg)


def _post_kernel(x_ref, od_ref, y_ref, lng_ref, lnb_ref, wo_ref, g_ref, wq_ref, qg_ref, kmt_ref, vm_ref,
                 wxo_ref, o_ref):
    half = od_ref.shape[1]
    x2 = x_ref[...] + jnp.dot(od_ref[...], wo_ref[0:half, :], preferred_element_type=F32)
    y = y_ref[...]
    yc = y - jnp.mean(y, axis=-1, keepdims=True)
    var = jnp.mean(yc * yc, axis=-1, keepdims=True)
    yn = yc * lax.rsqrt(var + EPS) * lng_ref[...] + lnb_ref[...]
    oc = (yn * jax.nn.sigmoid(yn)).astype(BF16)
    x2 = x2 + jnp.dot(oc, wo_ref[half:, :], preferred_element_type=F32)
    h = _rms_rows(x2, g_ref[...]).astype(BF16)
    q = jnp.dot(h, wq_ref[...], preferred_element_type=F32)
    scale = 1.0 / math.sqrt(XATTN_HDIM)
    outs = []
    for hd in range(XATTN_HEADS):
        sl = slice(hd * XATTN_HDIM, (hd + 1) * XATTN_HDIM)
        qn = _rms_rows(q[:, sl], qg_ref[...]).astype(BF16)
        sc = jnp.dot(qn, kmt_ref[sl, :], preferred_element_type=F32) * scale
        sc = sc - jnp.max(sc, axis=-1, keepdims=True)
        p = jnp.exp(sc)
        p = p / jnp.sum(p, axis=-1, keepdims=True)
        outs.append(jnp.dot(p.astype(BF16), vm_ref[:, sl], preferred_element_type=F32))
    o = jnp.concatenate(outs, axis=1).astype(BF16)
    o_ref[...] = x2 + jnp.dot(o, wxo_ref[...], preferred_element_type=F32)


def _post(x, od, y_conv, ln_g, ln_b, w_out, g, wq, qg, kmt, vm, wxo, *, tm=512):
    s, d = x.shape
    half = od.shape[1]
    xw = wq.shape[1]
    n_mem = vm.shape[0]
    const = lambda shape: pl.BlockSpec(shape, lambda i: (0, 0))
    return pl.pallas_call(
        _post_kernel,
        grid=(s // tm,),
        in_specs=[
            pl.BlockSpec((tm, d), lambda i: (i, 0)),
            pl.BlockSpec((tm, half), lambda i: (i, 0)),
            pl.BlockSpec((tm, half), lambda i: (i, 0)),
            const((1, half)), const((1, half)),
            const((2 * half, d)), const((1, d)), const((d, xw)), const((1, XATTN_HDIM)),
            const((xw, n_mem)), const((n_mem, xw)), const((xw, d)),
        ],
        out_specs=pl.BlockSpec((tm, d), lambda i: (i, 0)),
        out_shape=jax.ShapeDtypeStruct((s, d), F32),
        compiler_params=pltpu.CompilerParams(
            dimension_semantics=("arbitrary",), vmem_limit_bytes=VMEM_LIMIT),
        name="post",
    )(x, od, y_conv, ln_g, ln_b, w_out, g, wq, qg, kmt, vm, wxo)


def _rope_tables(positions):
    half = ROT_DIM // 2
    inv_freq = ROPE_THETA ** (-jnp.arange(0, ROT_DIM, 2, dtype=F32) / ROT_DIM)
    ang = positions.astype(F32)[:, None] * inv_freq
    cos, sin = jnp.cos(ang), jnp.sin(ang)
    n = positions.shape[0]
    rest = DIFF_QKDIM - ROT_DIM
    rc = jnp.concatenate([cos, cos, jnp.ones((n, rest), F32)], axis=1)
    ra = jnp.concatenate([-sin, jnp.zeros((n, half + rest), F32)], axis=1)
    rb = jnp.concatenate([jnp.zeros((n, half), F32), sin, jnp.zeros((n, rest), F32)], axis=1)
    rep = LANES // DIFF_QKDIM
    return jnp.tile(rc, (1, rep)), jnp.tile(ra, (1, rep)), jnp.tile(rb, (1, rep))


def kernel(x, mem, positions, ffn1_norm_g, ffn1_w_gate, ffn1_w_up, ffn1_w_down, mix_norm_g, w_in, q_norm_g, k_norm_g, lambda_q1, lambda_k1, lambda_q2, lambda_k2, diff_head_norm_g, conv_w, conv_b, conv_ln_g, conv_ln_b, w_out, xattn_norm_g, mem_norm_g, xattn_w_q, xattn_w_kv, xattn_q_norm_g, xattn_k_norm_g, xattn_w_o, ffn2_norm_g, ffn2_w_gate, ffn2_w_up, ffn2_w_down):
    b, s, d = x.shape
    assert b == 1 and ffn1_norm_g.shape[0] == 1, "one sequence, one layer"
    assert s % FFN_TM == 0 and s % PROJ_TM == 0 and ffn1_w_gate.shape[-1] % FFN_TF == 0
    bf = lambda w: w[0].astype(BF16)
    row = lambda v: v[0][None, :]

    rc, ra, rb = _rope_tables(positions[0])
    width = DIFF_HEADS * DIFF_VDIM
    qg = jnp.tile(q_norm_g[0], width // DIFF_QKDIM)[None, :]
    kg = jnp.tile(k_norm_g[0], width // DIFF_QKDIM)[None, :]
    blk = jnp.arange(MXU_DIM) // DIFF_QKDIM
    bd = (blk[:, None] == blk[None, :]).astype(BF16)

    n_i, n_j = s // FFN_TM, ffn1_w_gate.shape[-1] // FFN_TF

    def by_rows(w, scale=1.0):
        r, c = w.shape
        assert r % (n_i * _BF16_ROWS) == 0 and c % FFN_TF == 0 and c // FFN_TF <= n_j, w.shape
        last = c // FFN_TF - 1
        return _HostedCast(w, (r // n_i, FFN_TF), lambda i, j: (i, jnp.minimum(j, last)), scale)

    def by_cols(w, scale=1.0):
        r, c = w.shape
        assert r % (n_j * _BF16_ROWS) == 0 and c % (n_i * LANES) == 0, w.shape
        return _HostedCast(w, (r // n_j, c // n_i), lambda i, j: (j, i), scale)

    x1, wg2, wu2, wd2_half, w_in_bf = _ffn(
        x[0], row(ffn1_norm_g), bf(ffn1_w_gate), bf(ffn1_w_up), (0.5 * ffn1_w_down[0]).astype(BF16),
        [by_rows(ffn2_w_gate[0]), by_rows(ffn2_w_up[0]), by_cols(ffn2_w_down[0], 0.5), by_rows(w_in[0])])

    n_p = s // PROJ_TM

    def row_block(w):
        assert w.shape[0] % (n_p * _BF16_ROWS) == 0, w.shape
        return _HostedCast(w, (w.shape[0] // n_p, w.shape[1]), lambda i: (i, 0))

    qk, vt, u, w_out_bf, wq_bf, wkv_bf, wo_bf = _proj(
        x1, row(mix_norm_g), w_in_bf, qg, kg, bd, rc, ra, rb,
        [row_block(w_out[0]), row_block(xattn_w_q[0]), row_block(xattn_w_kv[0]), row_block(xattn_w_o[0])])
    o_diff = _attn(qk, vt, row(lambda_q1), row(lambda_k1), row(lambda_q2), row(lambda_k2),
                   row(diff_head_norm_g))
    conv_taps = jnp.broadcast_to(conv_w[0][:, None, :], (CONV_K, _SUB, conv_w.shape[-1]))
    conv_bias = jnp.broadcast_to(conv_b[0][None, :], (_SUB, conv_b.shape[-1]))
    y_conv = _conv(u, conv_taps, conv_bias)

    km, vm = _memkv(mem[0], row(mem_norm_g), wkv_bf, row(xattn_k_norm_g))
    x3 = _post(x1, o_diff, y_conv, row(conv_ln_g), row(conv_ln_b), w_out_bf, row(xattn_norm_g), wq_bf,
               row(xattn_q_norm_g), km, vm, wo_bf)

    out = _ffn(x3, row(ffn2_norm_g), wg2, wu2, wd2_half)[0]
    return out[None]
```

```python
import functools
import math
from typing import Callable, NamedTuple

import jax
import jax.numpy as jnp
from jax import lax
from jax.experimental import pallas as pl
from jax.experimental.pallas import tpu as pltpu

F32 = jnp.float32
BF16 = jnp.bfloat16

EPS = 1e-6
ROPE_THETA = 500000.0
DIFF_HEADS = 8
DIFF_VDIM = 128
DIFF_QKDIM = 64
ROT_DIM = 16
CONV_K = 31
XATTN_HEADS = 4
XATTN_HDIM = 128
LAMBDA_INIT = 0.8 - 0.6 * math.exp(0.0)

LANES = 128
_BF16_ROWS = 16
MXU_DIM = 256
VMEM_LIMIT = 56 * 1024 * 1024

NEG_BIG = -1e30


def _rms_rows(x, g):
    ms = jnp.mean(x * x, axis=-1, keepdims=True)
    return x * lax.rsqrt(ms + EPS) * g


class _HostedCast(NamedTuple):
    weight: jax.Array
    block: tuple
    index_map: Callable
    scale: float = 1.0


def _cast_blocks(srcs, dsts, scales):
    for src, dst, scale in zip(srcs, dsts, scales):
        v = src[...]
        dst[...] = (v if scale == 1.0 else v * scale).astype(BF16)


FFN_TM = 1024
FFN_TF = 512
_FFN_ROWS = 256


def _ffn_kernel(*refs, n_casts, cast_scales):
    x_ref, g_ref, wg_ref, wu_ref, wd_ref = refs[:5]
    cast_in = refs[5:5 + n_casts]
    o_ref = refs[5 + n_casts]
    cast_out = refs[6 + n_casts:6 + 2 * n_casts]
    h_ref = refs[6 + 2 * n_casts]
    j = pl.program_id(1)

    n_chunks = o_ref.shape[0] // _FFN_ROWS

    def body(first):
        def gate_up(r):
            rows = slice(r * _FFN_ROWS, (r + 1) * _FFN_ROWS)
            if first:
                h = _rms_rows(x_ref[rows, :], g_ref[...]).astype(BF16)
                h_ref[rows, :] = h
            else:
                h = h_ref[rows, :]
            a = jnp.dot(h, wg_ref[...], preferred_element_type=F32)
            b = jnp.dot(h, wu_ref[...], preferred_element_type=F32)
            return a, b

        ab = gate_up(0)
        for r in range(n_chunks):
            a, b = ab
            if r + 1 < n_chunks:
                ab = gate_up(r + 1)
            act = (a * jax.nn.sigmoid(a) * b).astype(BF16)
            rows = slice(r * _FFN_ROWS, (r + 1) * _FFN_ROWS)
            base = x_ref[rows, :] if first else o_ref[rows, :]
            o_ref[rows, :] = base + jnp.dot(act, wd_ref[...], preferred_element_type=F32)
        _cast_blocks(cast_in, cast_out, cast_scales)

    pl.when(j == 0)(lambda: body(True))
    pl.when(j > 0)(lambda: body(False))


def _ffn(x, g, wg, wu, wd_half, casts=(), *, tm=FFN_TM, tf=FFN_TF):
    s, d = x.shape
    f = wg.shape[1]
    cast_specs = [pl.BlockSpec(c.block, c.index_map) for c in casts]
    outs = pl.pallas_call(
        functools.partial(_ffn_kernel, n_casts=len(casts), cast_scales=tuple(c.scale for c in casts)),
        grid=(s // tm, f // tf),
        in_specs=[
            pl.BlockSpec((tm, d), lambda i, j: (i, 0)),
            pl.BlockSpec((1, d), lambda i, j: (0, 0)),
            pl.BlockSpec((d, tf), lambda i, j: (0, j)),
            pl.BlockSpec((d, tf), lambda i, j: (0, j)),
            pl.BlockSpec((tf, d), lambda i, j: (j, 0)),
        ] + cast_specs,
        out_specs=[pl.BlockSpec((tm, d), lambda i, j: (i, 0))] + cast_specs,
        out_shape=[jax.ShapeDtypeStruct((s, d), F32)]
        + [jax.ShapeDtypeStruct(c.weight.shape, BF16) for c in casts],
        scratch_shapes=[pltpu.VMEM((tm, d), BF16)],
        compiler_params=pltpu.CompilerParams(
            dimension_semantics=("arbitrary", "arbitrary"), vmem_limit_bytes=VMEM_LIMIT),
        name="ffn",
    )(x, g, wg, wu, wd_half, *[c.weight for c in casts])
    return outs


def _group_sumsq(p, bd_ref):
    p2 = (p * p).astype(BF16)
    outs = [jnp.dot(p2[:, c * MXU_DIM:(c + 1) * MXU_DIM], bd_ref[...], preferred_element_type=F32)
            for c in range(p.shape[1] // MXU_DIM)]
    return jnp.concatenate(outs, axis=1)


def _qk_norm_rope(p, gain, bd_ref, rc_ref, ra_ref, rb_ref, scale):
    ss = _group_sumsq(p, bd_ref)
    xn = p * lax.rsqrt(ss * (1.0 / DIFF_QKDIM) + EPS) * gain
    rc, ra, rb = rc_ref[...], ra_ref[...], rb_ref[...]
    outs = []
    for c in range(p.shape[1] // LANES):
        xc = xn[:, c * LANES:(c + 1) * LANES]
        up = pltpu.roll(xc, LANES - ROT_DIM // 2, axis=1)
        dn = pltpu.roll(xc, ROT_DIM // 2, axis=1)
        outs.append(xc * rc + up * ra + dn * rb)
    out = jnp.concatenate(outs, axis=1)
    if scale != 1.0:
        out = out * scale
    return out.astype(BF16)


PROJ_TM = 512
_PROJ_ROWS = 256


def _proj_kernel(*refs, n_casts, cast_scales):
    x_ref, g_ref, w_ref, qg_ref, kg_ref, bd_ref, rc_ref, ra_ref, rb_ref = refs[:9]
    cast_in = refs[9:9 + n_casts]
    qk_ref, vt_ref, u_ref = refs[9 + n_casts:12 + n_casts]
    cast_out = refs[12 + n_casts:]
    _cast_blocks(cast_in, cast_out, cast_scales)
    w = u_ref.shape[1]
    q_scale = math.log2(math.e) / math.sqrt(DIFF_QKDIM)
    for r in range(x_ref.shape[0] // _PROJ_ROWS):
        rows = slice(r * _PROJ_ROWS, (r + 1) * _PROJ_ROWS)
        h = _rms_rows(x_ref[rows, :], g_ref[...]).astype(BF16)
        rope = (rc_ref.at[rows, :], ra_ref.at[rows, :], rb_ref.at[rows, :])
        col = lambda n: jnp.dot(h, w_ref[:, n * w:(n + 1) * w], preferred_element_type=F32)
        qk_ref[0, rows, :] = _qk_norm_rope(col(0), qg_ref[...], bd_ref, *rope, q_scale)
        qk_ref[1, rows, :] = _qk_norm_rope(col(1), kg_ref[...], bd_ref, *rope, 1.0)
        vt_ref[:, rows] = col(2).T.astype(BF16)
        u_ref[rows, :] = col(3) * jax.nn.sigmoid(col(4))


def _proj(x, g, w_in, qg, kg, bd, rc, ra, rb, casts=(), *, tm=PROJ_TM):
    s, d = x.shape
    n_cols = w_in.shape[1]
    w = n_cols // 5
    once = pl.Buffered(1)
    cast_specs = [pl.BlockSpec(c.block, c.index_map) for c in casts]
    return pl.pallas_call(
        functools.partial(_proj_kernel, n_casts=len(casts), cast_scales=tuple(c.scale for c in casts)),
        grid=(s // tm,),
        in_specs=[
            pl.BlockSpec((tm, d), lambda i: (i, 0)),
            pl.BlockSpec((1, d), lambda i: (0, 0)),
            pl.BlockSpec((d, n_cols), lambda i: (0, 0), pipeline_mode=once),
            pl.BlockSpec((1, w), lambda i: (0, 0)),
            pl.BlockSpec((1, w), lambda i: (0, 0)),
            pl.BlockSpec((MXU_DIM, MXU_DIM), lambda i: (0, 0)),
            pl.BlockSpec((tm, LANES), lambda i: (i, 0)),
            pl.BlockSpec((tm, LANES), lambda i: (i, 0)),
            pl.BlockSpec((tm, LANES), lambda i: (i, 0)),
        ] + cast_specs,
        out_specs=[
            pl.BlockSpec((2, tm, w), lambda i: (0, i, 0)),
            pl.BlockSpec((w, tm), lambda i: (0, i)),
            pl.BlockSpec((tm, w), lambda i: (i, 0)),
        ] + cast_specs,
        out_shape=[
            jax.ShapeDtypeStruct((2, s, w), BF16),
            jax.ShapeDtypeStruct((w, s), BF16),
            jax.ShapeDtypeStruct((s, w), F32),
        ] + [jax.ShapeDtypeStruct(c.weight.shape, BF16) for c in casts],
        compiler_params=pltpu.CompilerParams(
            dimension_semantics=("arbitrary",), vmem_limit_bytes=VMEM_LIMIT),
        name="proj",
    )(x, g, w_in, qg, kg, bd, rc, ra, rb, *[c.weight for c in casts])


_ONES_ROWS = 16
_CHAIN_GROUP = 2


def _attn_kernel(q_ref, k_ref, vt_ref, lq1_ref, lk1_ref, lq2_ref, lk2_ref, hg_ref, o_ref,
                 acc1, acc2, m1, m2, s_ref, *, t, ts):
    i = pl.program_id(1)
    qt = q_ref[...].astype(F32).T
    row = lax.broadcasted_iota(jnp.int32, qt.shape, 0)
    qz = (jnp.where(row < DIFF_QKDIM, qt, 0.0).astype(BF16), jnp.where(row >= DIFF_QKDIM, qt, 0.0).astype(BF16))
    accs, ms = (acc1, acc2), (m1, m2)
    ones = jnp.ones((_ONES_ROWS, t), BF16)
    chains = [(st, c) for st in range(t // ts) for c in range(2)]

    def score(j, chain, diagonal):
        st, c = chain
        start = pl.multiple_of(j * t, t)
        nk = (st + 1) * ts if diagonal else t
        kb = k_ref[pl.ds(start, nk), :]
        s = jnp.dot(kb, qz[c][:, st * ts:(st + 1) * ts], preferred_element_type=F32)
        if diagonal:
            krow = lax.broadcasted_iota(jnp.int32, s.shape, 0)
            qcol = lax.broadcasted_iota(jnp.int32, s.shape, 1) + st * ts
            s = jnp.where(krow <= qcol, s, NEG_BIG)
        s_ref[2 * st + c, 0:nk, :] = s
        return jnp.max(s, axis=0, keepdims=True), nk

    def softmax(s_max, chain, diagonal):
        st, c = chain
        cols = slice(st * ts, (st + 1) * ts)
        m_blk, nk = s_max
        if diagonal:
            m_new, alpha = m_blk, None
        else:
            m_old = ms[c][:, cols]
            m_new = jnp.maximum(m_old, m_blk)
            alpha = jnp.exp2(m_old - m_new)
        ms[c][:, cols] = m_new
        return jnp.exp2(s_ref[2 * st + c, 0:nk, :] - m_new).astype(BF16), alpha

    def pv(j, chain, prob):
        st, c = chain
        p, alpha = prob
        nk = p.shape[0]
        start = pl.multiple_of(j * t, t)
        cols = slice(st * ts, (st + 1) * ts)
        vb = jnp.concatenate([vt_ref[:, pl.ds(start, nk)], ones[:, :nk]], axis=0)
        acc_new = jnp.dot(vb, p, preferred_element_type=F32)
        accs[c][st] = acc_new if alpha is None else alpha * accs[c][st] + acc_new

    def blocks(js, diagonal):
        groups = [(j, chains[g:g + _CHAIN_GROUP]) for j in js for g in range(0, len(chains), _CHAIN_GROUP)]
        j0, first = groups[0]
        maxima = [score(j0, chain, diagonal) for chain in first]
        for g, (j, group) in enumerate(groups):
            nxt = []
            if g + 1 < len(groups):
                j_nxt, group_nxt = groups[g + 1]
                nxt = [score(j_nxt, chain, diagonal) for chain in group_nxt]
            probs = [softmax(mx, chain, diagonal) for chain, mx in zip(group, maxima)]
            for chain, prob in zip(group, probs):
                pv(j, chain, prob)
            maxima = nxt

    blocks([i], True)

    def body(n, carry):
        blocks([2 * n, 2 * n + 1], False)
        return carry

    lax.fori_loop(0, i // 2, body, 0)
    pl.when(i % 2 == 1)(lambda: blocks([i - 1], False))

    lam = (jnp.exp(jnp.sum(lq1_ref[...] * lk1_ref[...], axis=-1, keepdims=True))
           - jnp.exp(jnp.sum(lq2_ref[...] * lk2_ref[...], axis=-1, keepdims=True))
           + LAMBDA_INIT)
    nv = DIFF_VDIM
    for st in range(t // ts):
        ot = (acc1[st, 0:nv, :] / acc1[st, nv:nv + 1, :]
              - lam * (acc2[st, 0:nv, :] / acc2[st, nv:nv + 1, :]))
        mean_sq = jnp.mean(ot * ot, axis=0, keepdims=True)
        on = ot * lax.rsqrt(mean_sq + EPS)
        o_ref[st * ts:(st + 1) * ts, :] = (on.T * hg_ref[...] * (1.0 - LAMBDA_INIT)).astype(BF16)


def _attn(qk, vt, lq1, lk1, lq2, lk2, hg, *, t=1024, ts=256):
    _, s, w = qk.shape
    nh = w // DIFF_VDIM
    assert s % t == 0 and t % ts == 0 and ts % MXU_DIM == 0
    vec = lambda n: pl.BlockSpec((1, n), lambda h, i: (0, 0))
    return pl.pallas_call(
        functools.partial(_attn_kernel, t=t, ts=ts),
        grid=(nh, s // t),
        in_specs=[
            pl.BlockSpec((None, t, DIFF_VDIM), lambda h, i: (0, i, h)),
            pl.BlockSpec((None, s, DIFF_VDIM), lambda h, i: (1, 0, h)),
            pl.BlockSpec((DIFF_VDIM, s), lambda h, i: (h, 0)),
            vec(DIFF_QKDIM), vec(DIFF_QKDIM), vec(DIFF_QKDIM), vec(DIFF_QKDIM),
            vec(DIFF_VDIM),
        ],
        out_specs=pl.BlockSpec((t, DIFF_VDIM), lambda h, i: (i, h)),
        out_shape=jax.ShapeDtypeStruct((s, w), BF16),
        scratch_shapes=[
            pltpu.VMEM((t // ts, DIFF_VDIM + _ONES_ROWS, ts), F32),
            pltpu.VMEM((t // ts, DIFF_VDIM + _ONES_ROWS, ts), F32),
            pltpu.VMEM((1, t), F32), pltpu.VMEM((1, t), F32),
            pltpu.VMEM((2 * (t // ts), t, ts), F32),
        ],
        compiler_params=pltpu.CompilerParams(
            dimension_semantics=("arbitrary", "arbitrary"), vmem_limit_bytes=VMEM_LIMIT),
        name="attn",
    )(qk, qk, vt, lq1, lk1, lq2, lk2, hg)


_HALO = 32
_SUB = 8
_ROWS = 32


def _conv_kernel(cur_ref, halo_ref, w_ref, b_ref, y_ref, e_ref, *, tc):
    i = pl.program_id(0)
    e_ref[0, 0:_HALO, :] = jnp.where(i > 0, halo_ref[...], 0.0)
    e_ref[0, _HALO:_HALO + tc, :] = cur_ref[...]
    span = tc + _HALO - _SUB
    for r in range(1, _SUB):
        e_ref[r, 0:span, :] = e_ref[0, r:r + span, :]

    first = _HALO - (CONV_K - 1)
    n_sub = _ROWS // _SUB

    width = cur_ref.shape[1]
    n_lane_blocks = 2

    def chunk(c, carry):
        t0 = pl.multiple_of(c * _ROWS, _ROWS)
        for lb in range(n_lane_blocks):
            lanes = slice(lb * width // n_lane_blocks, (lb + 1) * width // n_lane_blocks)
            acc = [b_ref[:, lanes] for _ in range(n_sub)]
            for r in range(_SUB):
                taps_r = [(k, (first + k) // _SUB) for k in range(CONV_K) if (first + k) % _SUB == r]
                weights = {k: w_ref[k, :, lanes] for k, _ in taps_r}
                for idx in range(max(a for _, a in taps_r) + n_sub):
                    uses = [(k, idx - a) for k, a in taps_r if 0 <= idx - a < n_sub]
                    if not uses:
                        continue
                    tile = e_ref[r, pl.ds(t0 + idx * _SUB, _SUB), lanes]
                    for k, q in uses:
                        acc[q] = acc[q] + weights[k] * tile
            for q in range(n_sub):
                y_ref[pl.ds(t0 + q * _SUB, _SUB), lanes] = acc[q]
        return carry

    lax.fori_loop(0, tc // _ROWS, chunk, 0, unroll=2)


def _conv(u, taps, bias, *, tc=512):
    s, c = u.shape
    assert s % tc == 0 and tc % _HALO == 0 and tc % (2 * _ROWS) == 0 and _HALO >= CONV_K - 1
    halo_blocks = tc // _HALO
    return pl.pallas_call(
        functools.partial(_conv_kernel, tc=tc),
        grid=(s // tc,),
        in_specs=[
            pl.BlockSpec((tc, c), lambda i: (i, 0)),
            pl.BlockSpec((_HALO, c), lambda i: (jnp.maximum(i * halo_blocks - 1, 0), 0)),
            pl.BlockSpec((CONV_K, _SUB, c), lambda i: (0, 0, 0)),
            pl.BlockSpec((_SUB, c), lambda i: (0, 0)),
        ],
        out_specs=pl.BlockSpec((tc, c), lambda i: (i, 0)),
        out_shape=jax.ShapeDtypeStruct((s, c), F32),
        scratch_shapes=[pltpu.VMEM((_SUB, tc + _HALO, c), F32)],
        compiler_params=pltpu.CompilerParams(
            dimension_semantics=("arbitrary",), vmem_limit_bytes=VMEM_LIMIT),
        name="conv",
    )(u, u, taps, bias)


def _memkv_kernel(mem_ref, g_ref, w_ref, kg_ref, k_ref, v_ref):
    h = _rms_rows(mem_ref[...], g_ref[...]).astype(BF16)
    kv = jnp.dot(h, w_ref[...], preferred_element_type=F32)
    width = v_ref.shape[1]
    for hd in range(XATTN_HEADS):
        sl = slice(hd * XATTN_HDIM, (hd + 1) * XATTN_HDIM)
        k_ref[sl, :] = _rms_rows(kv[:, sl], kg_ref[...]).T.astype(BF16)
    v_ref[...] = kv[:, width:].astype(BF16)


def _memkv(mem, g, w_kv, kg):
    n, _ = mem.shape
    width = w_kv.shape[1] // 2
    return pl.pallas_call(
        _memkv_kernel,
        out_shape=[jax.ShapeDtypeStruct((width, n), BF16), jax.ShapeDtypeStruct((n, width), BF16)],
        compiler_params=pltpu.CompilerParams(vmem_limit_bytes=VMEM_LIMIT),
        name="memkv",
    )(mem, g, w_kv, kg)


def _post_kernel(x_ref, od_ref, y_ref, lng_ref, lnb_ref, wo_ref, g_ref, wq_ref, qg_ref, kmt_ref, vm_ref,
                 wxo_ref, o_ref):
    half = od_ref.shape[1]
    x2 = x_ref[...] + jnp.dot(od_ref[...], wo_ref[0:half, :], preferred_element_type=F32)
    y = y_ref[...]
    yc = y - jnp.mean(y, axis=-1, keepdims=True)
    var = jnp.mean(yc * yc, axis=-1, keepdims=True)
    yn = yc * lax.rsqrt(var + EPS) * lng_ref[...] + lnb_ref[...]
    oc = (yn * jax.nn.sigmoid(yn)).astype(BF16)
    x2 = x2 + jnp.dot(oc, wo_ref[half:, :], preferred_element_type=F32)
    h = _rms_rows(x2, g_ref[...]).astype(BF16)
    q = jnp.dot(h, wq_ref[...], preferred_element_type=F32)
    scale = 1.0 / math.sqrt(XATTN_HDIM)
    heads = [slice(hd * XATTN_HDIM, (hd + 1) * XATTN_HDIM) for hd in range(XATTN_HEADS)]
    scores = [jnp.dot(_rms_rows(q[:, sl], qg_ref[...]).astype(BF16), kmt_ref[sl, :],
                      preferred_element_type=F32) * scale for sl in heads]
    probs = []
    for sc in scores:
        p = jnp.exp(sc - jnp.max(sc, axis=-1, keepdims=True))
        probs.append((p / jnp.sum(p, axis=-1, keepdims=True)).astype(BF16))
    outs = [jnp.dot(p, vm_ref[:, sl], preferred_element_type=F32) for p, sl in zip(probs, heads)]
    o = jnp.concatenate(outs, axis=1).astype(BF16)
    o_ref[...] = x2 + jnp.dot(o, wxo_ref[...], preferred_element_type=F32)


def _post(x, od, y_conv, ln_g, ln_b, w_out, g, wq, qg, kmt, vm, wxo, *, tm=512):
    s, d = x.shape
    half = od.shape[1]
    xw = wq.shape[1]
    n_mem = vm.shape[0]
    const = lambda shape: pl.BlockSpec(shape, lambda i: (0, 0))
    return pl.pallas_call(
        _post_kernel,
        grid=(s // tm,),
        in_specs=[
            pl.BlockSpec((tm, d), lambda i: (i, 0)),
            pl.BlockSpec((tm, half), lambda i: (i, 0)),
            pl.BlockSpec((tm, half), lambda i: (i, 0)),
            const((1, half)), const((1, half)),
            const((2 * half, d)), const((1, d)), const((d, xw)), const((1, XATTN_HDIM)),
            const((xw, n_mem)), const((n_mem, xw)), const((xw, d)),
        ],
        out_specs=pl.BlockSpec((tm, d), lambda i: (i, 0)),
        out_shape=jax.ShapeDtypeStruct((s, d), F32),
        compiler_params=pltpu.CompilerParams(
            dimension_semantics=("arbitrary",), vmem_limit_bytes=VMEM_LIMIT),
        name="post",
    )(x, od, y_conv, ln_g, ln_b, w_out, g, wq, qg, kmt, vm, wxo)


def _rope_tables(positions):
    half = ROT_DIM // 2
    inv_freq = ROPE_THETA ** (-jnp.arange(0, ROT_DIM, 2, dtype=F32) / ROT_DIM)
    ang = positions.astype(F32)[:, None] * inv_freq
    cos, sin = jnp.cos(ang), jnp.sin(ang)
    n = positions.shape[0]
    rest = DIFF_QKDIM - ROT_DIM
    rc = jnp.concatenate([cos, cos, jnp.ones((n, rest), F32)], axis=1)
    ra = jnp.concatenate([-sin, jnp.zeros((n, half + rest), F32)], axis=1)
    rb = jnp.concatenate([jnp.zeros((n, half), F32), sin, jnp.zeros((n, rest), F32)], axis=1)
    rep = LANES // DIFF_QKDIM
    return jnp.tile(rc, (1, rep)), jnp.tile(ra, (1, rep)), jnp.tile(rb, (1, rep))


def kernel(x, mem, positions, ffn1_norm_g, ffn1_w_gate, ffn1_w_up, ffn1_w_down, mix_norm_g, w_in, q_norm_g, k_norm_g, lambda_q1, lambda_k1, lambda_q2, lambda_k2, diff_head_norm_g, conv_w, conv_b, conv_ln_g, conv_ln_b, w_out, xattn_norm_g, mem_norm_g, xattn_w_q, xattn_w_kv, xattn_q_norm_g, xattn_k_norm_g, xattn_w_o, ffn2_norm_g, ffn2_w_gate, ffn2_w_up, ffn2_w_down):
    b, s, d = x.shape
    assert b == 1 and ffn1_norm_g.shape[0] == 1, "one sequence, one layer"
    assert s % FFN_TM == 0 and s % PROJ_TM == 0 and ffn1_w_gate.shape[-1] % FFN_TF == 0
    bf = lambda w: w[0].astype(BF16)
    row = lambda v: v[0][None, :]

    rc, ra, rb = _rope_tables(positions[0])
    width = DIFF_HEADS * DIFF_VDIM
    qg = jnp.tile(q_norm_g[0], width // DIFF_QKDIM)[None, :]
    kg = jnp.tile(k_norm_g[0], width // DIFF_QKDIM)[None, :]
    blk = jnp.arange(MXU_DIM) // DIFF_QKDIM
    bd = (blk[:, None] == blk[None, :]).astype(BF16)

    n_i, n_j = s // FFN_TM, ffn1_w_gate.shape[-1] // FFN_TF

    def by_rows(w, scale=1.0):
        r, c = w.shape
        assert r % (n_i * _BF16_ROWS) == 0 and c % FFN_TF == 0 and c // FFN_TF <= n_j, w.shape
        last = c // FFN_TF - 1
        return _HostedCast(w, (r // n_i, FFN_TF), lambda i, j: (i, jnp.minimum(j, last)), scale)

    def by_cols(w, scale=1.0):
        r, c = w.shape
        assert r % (n_j * _BF16_ROWS) == 0 and c % (n_i * LANES) == 0, w.shape
        return _HostedCast(w, (r // n_j, c // n_i), lambda i, j: (j, i), scale)

    x1, wg2, wu2, wd2_half, w_in_bf = _ffn(
        x[0], row(ffn1_norm_g), bf(ffn1_w_gate), bf(ffn1_w_up), (0.5 * ffn1_w_down[0]).astype(BF16),
        [by_rows(ffn2_w_gate[0]), by_rows(ffn2_w_up[0]), by_cols(ffn2_w_down[0], 0.5), by_rows(w_in[0])])

    n_p = s // PROJ_TM

    def row_block(w):
        assert w.shape[0] % (n_p * _BF16_ROWS) == 0, w.shape
        return _HostedCast(w, (w.shape[0] // n_p, w.shape[1]), lambda i: (i, 0))

    qk, vt, u, w_out_bf, wq_bf, wkv_bf, wo_bf = _proj(
        x1, row(mix_norm_g), w_in_bf, qg, kg, bd, rc, ra, rb,
        [row_block(w_out[0]), row_block(xattn_w_q[0]), row_block(xattn_w_kv[0]), row_block(xattn_w_o[0])])
    o_diff = _attn(qk, vt, row(lambda_q1), row(lambda_k1), row(lambda_q2), row(lambda_k2),
                   row(diff_head_norm_g))
    conv_taps = jnp.broadcast_to(conv_w[0][:, None, :], (CONV_K, _SUB, conv_w.shape[-1]))
    conv_bias = jnp.broadcast_to(conv_b[0][None, :], (_SUB, conv_b.shape[-1]))
    y_conv = _conv(u, conv_taps, conv_bias)

    km, vm = _memkv(mem[0], row(mem_norm_g), wkv_bf, row(xattn_k_norm_g))
    x3 = _post(x1, o_diff, y_conv, row(conv_ln_g), row(conv_ln_b), w_out_bf, row(xattn_norm_g), wq_bf,
               row(xattn_q_norm_g), km, vm, wo_bf)

    out = _ffn(x3, row(ffn2_norm_g), wg2, wu2, wd2_half)[0]
    return out[None]
```

```python
import functools
import math
from typing import Callable, NamedTuple

import jax
import jax.numpy as jnp
from jax import lax
from jax.experimental import pallas as pl
from jax.experimental.pallas import tpu as pltpu

F32 = jnp.float32
BF16 = jnp.bfloat16

EPS = 1e-6
ROPE_THETA = 500000.0
DIFF_HEADS = 8
DIFF_VDIM = 128
DIFF_QKDIM = 64
ROT_DIM = 16
CONV_K = 31
XATTN_HEADS = 4
XATTN_HDIM = 128
LAMBDA_INIT = 0.8 - 0.6 * math.exp(0.0)

LANES = 128
_BF16_ROWS = 16
MXU_DIM = 256
VMEM_LIMIT = 56 * 1024 * 1024

NEG_BIG = -1e30


def _rms_rows(x, g):
    ms = jnp.mean(x * x, axis=-1, keepdims=True)
    return x * lax.rsqrt(ms + EPS) * g


class _HostedCast(NamedTuple):
    weight: jax.Array
    block: tuple
    index_map: Callable
    scale: float = 1.0


def _cast_blocks(srcs, dsts, scales):
    for src, dst, scale in zip(srcs, dsts, scales):
        v = src[...]
        dst[...] = (v if scale == 1.0 else v * scale).astype(BF16)


FFN_TM = 1024
FFN_TF = 512
_FFN_ROWS = 256


def _ffn_kernel(*refs, n_casts, cast_scales):
    x_ref, g_ref, wg_ref, wu_ref, wd_ref = refs[:5]
    cast_in = refs[5:5 + n_casts]
    o_ref = refs[5 + n_casts]
    cast_out = refs[6 + n_casts:6 + 2 * n_casts]
    h_ref = refs[6 + 2 * n_casts]
    j = pl.program_id(1)

    n_chunks = o_ref.shape[0] // _FFN_ROWS

    def body(first):
        def gate_up(r):
            rows = slice(r * _FFN_ROWS, (r + 1) * _FFN_ROWS)
            if first:
                h = _rms_rows(x_ref[rows, :], g_ref[...]).astype(BF16)
                h_ref[rows, :] = h
            else:
                h = h_ref[rows, :]
            a = jnp.dot(h, wg_ref[...], preferred_element_type=F32)
            b = jnp.dot(h, wu_ref[...], preferred_element_type=F32)
            return a, b

        ab = gate_up(0)
        for r in range(n_chunks):
            a, b = ab
            if r + 1 < n_chunks:
                ab = gate_up(r + 1)
            act = (a * jax.nn.sigmoid(a) * b).astype(BF16)
            rows = slice(r * _FFN_ROWS, (r + 1) * _FFN_ROWS)
            base = x_ref[rows, :] if first else o_ref[rows, :]
            o_ref[rows, :] = base + jnp.dot(act, wd_ref[...], preferred_element_type=F32)
        _cast_blocks(cast_in, cast_out, cast_scales)

    pl.when(j == 0)(lambda: body(True))
    pl.when(j > 0)(lambda: body(False))


def _ffn(x, g, wg, wu, wd_half, casts=(), *, tm=FFN_TM, tf=FFN_TF):
    s, d = x.shape
    f = wg.shape[1]
    cast_specs = [pl.BlockSpec(c.block, c.index_map) for c in casts]
    outs = pl.pallas_call(
        functools.partial(_ffn_kernel, n_casts=len(casts), cast_scales=tuple(c.scale for c in casts)),
        grid=(s // tm, f // tf),
        in_specs=[
            pl.BlockSpec((tm, d), lambda i, j: (i, 0)),
            pl.BlockSpec((1, d), lambda i, j: (0, 0)),
            pl.BlockSpec((d, tf), lambda i, j: (0, j)),
            pl.BlockSpec((d, tf), lambda i, j: (0, j)),
            pl.BlockSpec((tf, d), lambda i, j: (j, 0)),
        ] + cast_specs,
        out_specs=[pl.BlockSpec((tm, d), lambda i, j: (i, 0))] + cast_specs,
        out_shape=[jax.ShapeDtypeStruct((s, d), F32)]
        + [jax.ShapeDtypeStruct(c.weight.shape, BF16) for c in casts],
        scratch_shapes=[pltpu.VMEM((tm, d), BF16)],
        compiler_params=pltpu.CompilerParams(
            dimension_semantics=("arbitrary", "arbitrary"), vmem_limit_bytes=VMEM_LIMIT),
        name="ffn",
    )(x, g, wg, wu, wd_half, *[c.weight for c in casts])
    return outs


def _group_sumsq(p, bd_ref):
    p2 = (p * p).astype(BF16)
    outs = [jnp.dot(p2[:, c * MXU_DIM:(c + 1) * MXU_DIM], bd_ref[...], preferred_element_type=F32)
            for c in range(p.shape[1] // MXU_DIM)]
    return jnp.concatenate(outs, axis=1)


def _qk_norm_rope(p, gain, bd_ref, rc_ref, ra_ref, rb_ref, scale):
    ss = _group_sumsq(p, bd_ref)
    xn = p * lax.rsqrt(ss * (1.0 / DIFF_QKDIM) + EPS) * gain
    rc, ra, rb = rc_ref[...], ra_ref[...], rb_ref[...]
    outs = []
    for c in range(p.shape[1] // LANES):
        xc = xn[:, c * LANES:(c + 1) * LANES]
        up = pltpu.roll(xc, LANES - ROT_DIM // 2, axis=1)
        dn = pltpu.roll(xc, ROT_DIM // 2, axis=1)
        outs.append(xc * rc + up * ra + dn * rb)
    out = jnp.concatenate(outs, axis=1)
    if scale != 1.0:
        out = out * scale
    return out.astype(BF16)


PROJ_TM = 512
_PROJ_ROWS = 256


def _proj_kernel(*refs, n_casts, cast_scales):
    x_ref, g_ref, w_ref, qg_ref, kg_ref, bd_ref, rc_ref, ra_ref, rb_ref = refs[:9]
    cast_in = refs[9:9 + n_casts]
    qk_ref, vt_ref, u_ref = refs[9 + n_casts:12 + n_casts]
    cast_out = refs[12 + n_casts:]
    _cast_blocks(cast_in, cast_out, cast_scales)
    w = u_ref.shape[1]
    q_scale = math.log2(math.e) / math.sqrt(DIFF_QKDIM)
    for r in range(x_ref.shape[0] // _PROJ_ROWS):
        rows = slice(r * _PROJ_ROWS, (r + 1) * _PROJ_ROWS)
        h = _rms_rows(x_ref[rows, :], g_ref[...]).astype(BF16)
        rope = (rc_ref.at[rows, :], ra_ref.at[rows, :], rb_ref.at[rows, :])
        col = lambda n: jnp.dot(h, w_ref[:, n * w:(n + 1) * w], preferred_element_type=F32)
        qk_ref[0, rows, :] = _qk_norm_rope(col(0), qg_ref[...], bd_ref, *rope, q_scale)
        qk_ref[1, rows, :] = _qk_norm_rope(col(1), kg_ref[...], bd_ref, *rope, 1.0)
        vt_ref[:, rows] = col(2).T.astype(BF16)
        u_ref[rows, :] = col(3) * jax.nn.sigmoid(col(4))


def _proj(x, g, w_in, qg, kg, bd, rc, ra, rb, casts=(), *, tm=PROJ_TM):
    s, d = x.shape
    n_cols = w_in.shape[1]
    w = n_cols // 5
    once = pl.Buffered(1)
    cast_specs = [pl.BlockSpec(c.block, c.index_map) for c in casts]
    return pl.pallas_call(
        functools.partial(_proj_kernel, n_casts=len(casts), cast_scales=tuple(c.scale for c in casts)),
        grid=(s // tm,),
        in_specs=[
            pl.BlockSpec((tm, d), lambda i: (i, 0)),
            pl.BlockSpec((1, d), lambda i: (0, 0)),
            pl.BlockSpec((d, n_cols), lambda i: (0, 0), pipeline_mode=once),
            pl.BlockSpec((1, w), lambda i: (0, 0)),
            pl.BlockSpec((1, w), lambda i: (0, 0)),
            pl.BlockSpec((MXU_DIM, MXU_DIM), lambda i: (0, 0)),
            pl.BlockSpec((tm, LANES), lambda i: (i, 0)),
            pl.BlockSpec((tm, LANES), lambda i: (i, 0)),
            pl.BlockSpec((tm, LANES), lambda i: (i, 0)),
        ] + cast_specs,
        out_specs=[
            pl.BlockSpec((2, tm, w), lambda i: (0, i, 0)),
            pl.BlockSpec((w, tm), lambda i: (0, i)),
            pl.BlockSpec((tm, w), lambda i: (i, 0)),
        ] + cast_specs,
        out_shape=[
            jax.ShapeDtypeStruct((2, s, w), BF16),
            jax.ShapeDtypeStruct((w, s), BF16),
            jax.ShapeDtypeStruct((s, w), F32),
        ] + [jax.ShapeDtypeStruct(c.weight.shape, BF16) for c in casts],
        compiler_params=pltpu.CompilerParams(
            dimension_semantics=("arbitrary",), vmem_limit_bytes=VMEM_LIMIT),
        name="proj",
    )(x, g, w_in, qg, kg, bd, rc, ra, rb, *[c.weight for c in casts])


_ONES_ROWS = 16
_CHAIN_GROUP = 2


def _attn_kernel(q_ref, k_ref, vt_ref, lq1_ref, lk1_ref, lq2_ref, lk2_ref, hg_ref, o_ref,
                 acc1, acc2, m1, m2, s_ref, *, t, ts):
    i = pl.program_id(1)
    qt = q_ref[...].astype(F32).T
    row = lax.broadcasted_iota(jnp.int32, qt.shape, 0)
    qz = (jnp.where(row < DIFF_QKDIM, qt, 0.0).astype(BF16), jnp.where(row >= DIFF_QKDIM, qt, 0.0).astype(BF16))
    accs, ms = (acc1, acc2), (m1, m2)
    ones = jnp.ones((_ONES_ROWS, t), BF16)
    chains = [(st, c) for st in range(t // ts) for c in range(2)]

    def score(j, chain, diagonal):
        st, c = chain
        start = pl.multiple_of(j * t, t)
        nk = (st + 1) * ts if diagonal else t
        kb = k_ref[pl.ds(start, nk), :]
        s = jnp.dot(kb, qz[c][:, st * ts:(st + 1) * ts], preferred_element_type=F32)
        if diagonal:
            krow = lax.broadcasted_iota(jnp.int32, s.shape, 0)
            qcol = lax.broadcasted_iota(jnp.int32, s.shape, 1) + st * ts
            s = jnp.where(krow <= qcol, s, NEG_BIG)
        s_ref[2 * st + c, 0:nk, :] = s
        return jnp.max(s, axis=0, keepdims=True), nk

    def softmax(s_max, chain, diagonal):
        st, c = chain
        cols = slice(st * ts, (st + 1) * ts)
        m_blk, nk = s_max
        if diagonal:
            m_new, alpha = m_blk, None
        else:
            m_old = ms[c][:, cols]
            m_new = jnp.maximum(m_old, m_blk)
            alpha = jnp.exp2(m_old - m_new)
        ms[c][:, cols] = m_new
        return jnp.exp2(s_ref[2 * st + c, 0:nk, :] - m_new).astype(BF16), alpha

    def pv(j, chain, prob):
        st, c = chain
        p, alpha = prob
        nk = p.shape[0]
        start = pl.multiple_of(j * t, t)
        cols = slice(st * ts, (st + 1) * ts)
        vb = jnp.concatenate([vt_ref[:, pl.ds(start, nk)], ones[:, :nk]], axis=0)
        acc_new = jnp.dot(vb, p, preferred_element_type=F32)
        accs[c][st] = acc_new if alpha is None else alpha * accs[c][st] + acc_new

    def blocks(specs):
        groups = [(j, diag, chains[g:g + _CHAIN_GROUP])
                  for j, diag in specs for g in range(0, len(chains), _CHAIN_GROUP)]
        j0, diag0, first = groups[0]
        maxima = [score(j0, chain, diag0) for chain in first]
        for g, (j, diag, group) in enumerate(groups):
            nxt = []
            if g + 1 < len(groups):
                j_nxt, diag_nxt, group_nxt = groups[g + 1]
                nxt = [score(j_nxt, chain, diag_nxt) for chain in group_nxt]
            probs = [softmax(mx, chain, diag) for chain, mx in zip(group, maxima)]
            for chain, prob in zip(group, probs):
                pv(j, chain, prob)
            maxima = nxt

    def run(first, count):
        blocks([(first + n, False) for n in range(count)])

    pl.when(i == 0)(lambda: blocks([(i, True)]))
    pl.when(i > 0)(lambda: blocks([(i, True), (0, False)]))
    rest = jnp.maximum(i - 1, 0)
    quads = rest // 4

    def body(n, carry):
        run(1 + 4 * n, 4)
        return carry

    lax.fori_loop(0, quads, body, 0)
    pl.when(rest % 4 >= 2)(lambda: run(1 + 4 * quads, 2))
    pl.when(rest % 2 == 1)(lambda: run(rest, 1))

    lam = (jnp.exp(jnp.sum(lq1_ref[...] * lk1_ref[...], axis=-1, keepdims=True))
           - jnp.exp(jnp.sum(lq2_ref[...] * lk2_ref[...], axis=-1, keepdims=True))
           + LAMBDA_INIT)
    nv = DIFF_VDIM
    for st in range(t // ts):
        ot = (acc1[st, 0:nv, :] / acc1[st, nv:nv + 1, :]
              - lam * (acc2[st, 0:nv, :] / acc2[st, nv:nv + 1, :]))
        mean_sq = jnp.mean(ot * ot, axis=0, keepdims=True)
        on = ot * lax.rsqrt(mean_sq + EPS)
        o_ref[st * ts:(st + 1) * ts, :] = (on.T * hg_ref[...] * (1.0 - LAMBDA_INIT)).astype(BF16)


def _attn(qk, vt, lq1, lk1, lq2, lk2, hg, *, t=1024, ts=256):
    _, s, w = qk.shape
    nh = w // DIFF_VDIM
    assert s % t == 0 and t % ts == 0 and ts % MXU_DIM == 0
    vec = lambda n: pl.BlockSpec((1, n), lambda h, i: (0, 0))
    return pl.pallas_call(
        functools.partial(_attn_kernel, t=t, ts=ts),
        grid=(nh, s // t),
        in_specs=[
            pl.BlockSpec((None, t, DIFF_VDIM), lambda h, i: (0, i, h)),
            pl.BlockSpec((None, s, DIFF_VDIM), lambda h, i: (1, 0, h)),
            pl.BlockSpec((DIFF_VDIM, s), lambda h, i: (h, 0)),
            vec(DIFF_QKDIM), vec(DIFF_QKDIM), vec(DIFF_QKDIM), vec(DIFF_QKDIM),
            vec(DIFF_VDIM),
        ],
        out_specs=pl.BlockSpec((t, DIFF_VDIM), lambda h, i: (i, h)),
        out_shape=jax.ShapeDtypeStruct((s, w), BF16),
        scratch_shapes=[
            pltpu.VMEM((t // ts, DIFF_VDIM + _ONES_ROWS, ts), F32),
            pltpu.VMEM((t // ts, DIFF_VDIM + _ONES_ROWS, ts), F32),
            pltpu.VMEM((1, t), F32), pltpu.VMEM((1, t), F32),
            pltpu.VMEM((2 * (t // ts), t, ts), F32),
        ],
        compiler_params=pltpu.CompilerParams(
            dimension_semantics=("arbitrary", "arbitrary"), vmem_limit_bytes=VMEM_LIMIT),
        name="attn",
    )(qk, qk, vt, lq1, lk1, lq2, lk2, hg)


_HALO = 32
_SUB = 8
_ROWS = 32


def _conv_kernel(cur_ref, halo_ref, w_ref, b_ref, y_ref, e_ref, *, tc):
    i = pl.program_id(0)
    e_ref[0, 0:_HALO, :] = jnp.where(i > 0, halo_ref[...], 0.0)
    e_ref[0, _HALO:_HALO + tc, :] = cur_ref[...]
    span = tc + _HALO - _SUB
    for r in range(1, _SUB):
        e_ref[r, 0:span, :] = e_ref[0, r:r + span, :]

    first = _HALO - (CONV_K - 1)
    n_sub = _ROWS // _SUB

    width = cur_ref.shape[1]
    n_lane_blocks = 2

    def chunk(c, carry):
        t0 = pl.multiple_of(c * _ROWS, _ROWS)
        for lb in range(n_lane_blocks):
            lanes = slice(lb * width // n_lane_blocks, (lb + 1) * width // n_lane_blocks)
            acc = [b_ref[:, lanes] for _ in range(n_sub)]
            for r in range(_SUB):
                taps_r = [(k, (first + k) // _SUB) for k in range(CONV_K) if (first + k) % _SUB == r]
                weights = {k: w_ref[k, :, lanes] for k, _ in taps_r}
                for idx in range(max(a for _, a in taps_r) + n_sub):
                    uses = [(k, idx - a) for k, a in taps_r if 0 <= idx - a < n_sub]
                    if not uses:
                        continue
                    tile = e_ref[r, pl.ds(t0 + idx * _SUB, _SUB), lanes]
                    for k, q in uses:
                        acc[q] = acc[q] + weights[k] * tile
            for q in range(n_sub):
                y_ref[pl.ds(t0 + q * _SUB, _SUB), lanes] = acc[q]
        return carry

    lax.fori_loop(0, tc // _ROWS, chunk, 0, unroll=2)


def _conv(u, taps, bias, *, tc=512):
    s, c = u.shape
    assert s % tc == 0 and tc % _HALO == 0 and tc % (2 * _ROWS) == 0 and _HALO >= CONV_K - 1
    halo_blocks = tc // _HALO
    return pl.pallas_call(
        functools.partial(_conv_kernel, tc=tc),
        grid=(s // tc,),
        in_specs=[
            pl.BlockSpec((tc, c), lambda i: (i, 0)),
            pl.BlockSpec((_HALO, c), lambda i: (jnp.maximum(i * halo_blocks - 1, 0), 0)),
            pl.BlockSpec((CONV_K, _SUB, c), lambda i: (0, 0, 0)),
            pl.BlockSpec((_SUB, c), lambda i: (0, 0)),
        ],
        out_specs=pl.BlockSpec((tc, c), lambda i: (i, 0)),
        out_shape=jax.ShapeDtypeStruct((s, c), F32),
        scratch_shapes=[pltpu.VMEM((_SUB, tc + _HALO, c), F32)],
        compiler_params=pltpu.CompilerParams(
            dimension_semantics=("arbitrary",), vmem_limit_bytes=VMEM_LIMIT),
        name="conv",
    )(u, u, taps, bias)


def _memkv_kernel(mem_ref, g_ref, w_ref, kg_ref, k_ref, v_ref):
    h = _rms_rows(mem_ref[...], g_ref[...]).astype(BF16)
    kv = jnp.dot(h, w_ref[...], preferred_element_type=F32)
    width = v_ref.shape[1]
    for hd in range(XATTN_HEADS):
        sl = slice(hd * XATTN_HDIM, (hd + 1) * XATTN_HDIM)
        k_ref[sl, :] = _rms_rows(kv[:, sl], kg_ref[...]).T.astype(BF16)
    v_ref[...] = kv[:, width:].astype(BF16)


def _memkv(mem, g, w_kv, kg):
    n, _ = mem.shape
    width = w_kv.shape[1] // 2
    return pl.pallas_call(
        _memkv_kernel,
        out_shape=[jax.ShapeDtypeStruct((width, n), BF16), jax.ShapeDtypeStruct((n, width), BF16)],
        compiler_params=pltpu.CompilerParams(vmem_limit_bytes=VMEM_LIMIT),
        name="memkv",
    )(mem, g, w_kv, kg)


def _post_kernel(x_ref, od_ref, y_ref, lng_ref, lnb_ref, wo_ref, g_ref, wq_ref, qg_ref, kmt_ref, vm_ref,
                 wxo_ref, o_ref):
    half = od_ref.shape[1]
    x2 = x_ref[...] + jnp.dot(od_ref[...], wo_ref[0:half, :], preferred_element_type=F32)
    y = y_ref[...]
    yc = y - jnp.mean(y, axis=-1, keepdims=True)
    var = jnp.mean(yc * yc, axis=-1, keepdims=True)
    yn = yc * lax.rsqrt(var + EPS) * lng_ref[...] + lnb_ref[...]
    oc = (yn * jax.nn.sigmoid(yn)).astype(BF16)
    x2 = x2 + jnp.dot(oc, wo_ref[half:, :], preferred_element_type=F32)
    h = _rms_rows(x2, g_ref[...]).astype(BF16)
    q = jnp.dot(h, wq_ref[...], preferred_element_type=F32)
    scale = 1.0 / math.sqrt(XATTN_HDIM)
    heads = [slice(hd * XATTN_HDIM, (hd + 1) * XATTN_HDIM) for hd in range(XATTN_HEADS)]
    scores = [jnp.dot(_rms_rows(q[:, sl], qg_ref[...]).astype(BF16), kmt_ref[sl, :],
                      preferred_element_type=F32) * scale for sl in heads]
    probs = []
    for sc in scores:
        p = jnp.exp(sc - jnp.max(sc, axis=-1, keepdims=True))
        probs.append((p / jnp.sum(p, axis=-1, keepdims=True)).astype(BF16))
    outs = [jnp.dot(p, vm_ref[:, sl], preferred_element_type=F32) for p, sl in zip(probs, heads)]
    o = jnp.concatenate(outs, axis=1).astype(BF16)
    o_ref[...] = x2 + jnp.dot(o, wxo_ref[...], preferred_element_type=F32)


def _post(x, od, y_conv, ln_g, ln_b, w_out, g, wq, qg, kmt, vm, wxo, *, tm=512):
    s, d = x.shape
    half = od.shape[1]
    xw = wq.shape[1]
    n_mem = vm.shape[0]
    const = lambda shape: pl.BlockSpec(shape, lambda i: (0, 0))
    return pl.pallas_call(
        _post_kernel,
        grid=(s // tm,),
        in_specs=[
            pl.BlockSpec((tm, d), lambda i: (i, 0)),
            pl.BlockSpec((tm, half), lambda i: (i, 0)),
            pl.BlockSpec((tm, half), lambda i: (i, 0)),
            const((1, half)), const((1, half)),
            const((2 * half, d)), const((1, d)), const((d, xw)), const((1, XATTN_HDIM)),
            const((xw, n_mem)), const((n_mem, xw)), const((xw, d)),
        ],
        out_specs=pl.BlockSpec((tm, d), lambda i: (i, 0)),
        out_shape=jax.ShapeDtypeStruct((s, d), F32),
        compiler_params=pltpu.CompilerParams(
            dimension_semantics=("arbitrary",), vmem_limit_bytes=VMEM_LIMIT),
        name="post",
    )(x, od, y_conv, ln_g, ln_b, w_out, g, wq, qg, kmt, vm, wxo)


def _rope_tables(positions):
    half = ROT_DIM // 2
    inv_freq = ROPE_THETA ** (-jnp.arange(0, ROT_DIM, 2, dtype=F32) / ROT_DIM)
    ang = positions.astype(F32)[:, None] * inv_freq
    cos, sin = jnp.cos(ang), jnp.sin(ang)
    n = positions.shape[0]
    rest = DIFF_QKDIM - ROT_DIM
    rc = jnp.concatenate([cos, cos, jnp.ones((n, rest), F32)], axis=1)
    ra = jnp.concatenate([-sin, jnp.zeros((n, half + rest), F32)], axis=1)
    rb = jnp.concatenate([jnp.zeros((n, half), F32), sin, jnp.zeros((n, rest), F32)], axis=1)
    rep = LANES // DIFF_QKDIM
    return jnp.tile(rc, (1, rep)), jnp.tile(ra, (1, rep)), jnp.tile(rb, (1, rep))


def kernel(x, mem, positions, ffn1_norm_g, ffn1_w_gate, ffn1_w_up, ffn1_w_down, mix_norm_g, w_in, q_norm_g, k_norm_g, lambda_q1, lambda_k1, lambda_q2, lambda_k2, diff_head_norm_g, conv_w, conv_b, conv_ln_g, conv_ln_b, w_out, xattn_norm_g, mem_norm_g, xattn_w_q, xattn_w_kv, xattn_q_norm_g, xattn_k_norm_g, xattn_w_o, ffn2_norm_g, ffn2_w_gate, ffn2_w_up, ffn2_w_down):
    b, s, d = x.shape
    assert b == 1 and ffn1_norm_g.shape[0] == 1, "one sequence, one layer"
    assert s % FFN_TM == 0 and s % PROJ_TM == 0 and ffn1_w_gate.shape[-1] % FFN_TF == 0
    bf = lambda w: w[0].astype(BF16)
    row = lambda v: v[0][None, :]

    rc, ra, rb = _rope_tables(positions[0])
    width = DIFF_HEADS * DIFF_VDIM
    qg = jnp.tile(q_norm_g[0], width // DIFF_QKDIM)[None, :]
    kg = jnp.tile(k_norm_g[0], width // DIFF_QKDIM)[None, :]
    blk = jnp.arange(MXU_DIM) // DIFF_QKDIM
    bd = (blk[:, None] == blk[None, :]).astype(BF16)

    n_i, n_j = s // FFN_TM, ffn1_w_gate.shape[-1] // FFN_TF

    def by_rows(w, scale=1.0):
        r, c = w.shape
        assert r % (n_i * _BF16_ROWS) == 0 and c % FFN_TF == 0 and c // FFN_TF <= n_j, w.shape
        last = c // FFN_TF - 1
        return _HostedCast(w, (r // n_i, FFN_TF), lambda i, j: (i, jnp.minimum(j, last)), scale)

    def by_cols(w, scale=1.0):
        r, c = w.shape
        assert r % (n_j * _BF16_ROWS) == 0 and c % (n_i * LANES) == 0, w.shape
        return _HostedCast(w, (r // n_j, c // n_i), lambda i, j: (j, i), scale)

    x1, wg2, wu2, wd2_half, w_in_bf = _ffn(
        x[0], row(ffn1_norm_g), bf(ffn1_w_gate), bf(ffn1_w_up), (0.5 * ffn1_w_down[0]).astype(BF16),
        [by_rows(ffn2_w_gate[0]), by_rows(ffn2_w_up[0]), by_cols(ffn2_w_down[0], 0.5), by_rows(w_in[0])])

    n_p = s // PROJ_TM

    def row_block(w):
        assert w.shape[0] % (n_p * _BF16_ROWS) == 0, w.shape
        return _HostedCast(w, (w.shape[0] // n_p, w.shape[1]), lambda i: (i, 0))

    qk, vt, u, w_out_bf, wq_bf, wkv_bf, wo_bf = _proj(
        x1, row(mix_norm_g), w_in_bf, qg, kg, bd, rc, ra, rb,
        [row_block(w_out[0]), row_block(xattn_w_q[0]), row_block(xattn_w_kv[0]), row_block(xattn_w_o[0])])
    o_diff = _attn(qk, vt, row(lambda_q1), row(lambda_k1), row(lambda_q2), row(lambda_k2),
                   row(diff_head_norm_g))
    conv_taps = jnp.broadcast_to(conv_w[0][:, None, :], (CONV_K, _SUB, conv_w.shape[-1]))
    conv_bias = jnp.broadcast_to(conv_b[0][None, :], (_SUB, conv_b.shape[-1]))
    y_conv = _conv(u, conv_taps, conv_bias)

    km, vm = _memkv(mem[0], row(mem_norm_g), wkv_bf, row(xattn_k_norm_g))
    x3 = _post(x1, o_diff, y_conv, row(conv_ln_g), row(conv_ln_b), w_out_bf, row(xattn_norm_g), wq_bf,
               row(xattn_q_norm_g), km, vm, wo_bf)

    out = _ffn(x3, row(ffn2_norm_g), wg2, wu2, wd2_half)[0]
    return out[None]
```

```python
import functools
import math
from typing import Callable, NamedTuple

import jax
import jax.numpy as jnp
from jax import lax
from jax.experimental import pallas as pl
from jax.experimental.pallas import tpu as pltpu

F32 = jnp.float32
BF16 = jnp.bfloat16

EPS = 1e-6
ROPE_THETA = 500000.0
DIFF_HEADS = 8
DIFF_VDIM = 128
DIFF_QKDIM = 64
ROT_DIM = 16
CONV_K = 31
XATTN_HEADS = 4
XATTN_HDIM = 128
LAMBDA_INIT = 0.8 - 0.6 * math.exp(0.0)

LANES = 128
_BF16_ROWS = 16
MXU_DIM = 256
VMEM_LIMIT = 56 * 1024 * 1024

NEG_BIG = -1e30


def _rms_rows(x, g):
    ms = jnp.mean(x * x, axis=-1, keepdims=True)
    return x * lax.rsqrt(ms + EPS) * g


class _HostedCast(NamedTuple):
    weight: jax.Array
    block: tuple
    index_map: Callable
    scale: float = 1.0


def _cast_blocks(srcs, dsts, scales):
    for src, dst, scale in zip(srcs, dsts, scales):
        v = src[...]
        dst[...] = (v if scale == 1.0 else v * scale).astype(BF16)


FFN_TM = 1024
FFN_TF = 512
_FFN_ROWS = 256


def _ffn_kernel(*refs, n_casts, cast_scales):
    x_ref, g_ref, wg_ref, wu_ref, wd_ref = refs[:5]
    cast_in = refs[5:5 + n_casts]
    o_ref = refs[5 + n_casts]
    cast_out = refs[6 + n_casts:6 + 2 * n_casts]
    h_ref = refs[6 + 2 * n_casts]
    j = pl.program_id(1)

    n_chunks = o_ref.shape[0] // _FFN_ROWS

    def body(first):
        def gate_up(r):
            rows = slice(r * _FFN_ROWS, (r + 1) * _FFN_ROWS)
            if first:
                h = _rms_rows(x_ref[rows, :], g_ref[...]).astype(BF16)
                h_ref[rows, :] = h
            else:
                h = h_ref[rows, :]
            a = jnp.dot(h, wg_ref[...], preferred_element_type=F32)
            b = jnp.dot(h, wu_ref[...], preferred_element_type=F32)
            return a, b

        ab = gate_up(0)
        for r in range(n_chunks):
            a, b = ab
            if r + 1 < n_chunks:
                ab = gate_up(r + 1)
            act = (a * jax.nn.sigmoid(a) * b).astype(BF16)
            rows = slice(r * _FFN_ROWS, (r + 1) * _FFN_ROWS)
            base = x_ref[rows, :] if first else o_ref[rows, :]
            o_ref[rows, :] = base + jnp.dot(act, wd_ref[...], preferred_element_type=F32)
        _cast_blocks(cast_in, cast_out, cast_scales)

    pl.when(j == 0)(lambda: body(True))
    pl.when(j > 0)(lambda: body(False))


def _ffn(x, g, wg, wu, wd_half, casts=(), *, tm=FFN_TM, tf=FFN_TF):
    s, d = x.shape
    f = wg.shape[1]
    cast_specs = [pl.BlockSpec(c.block, c.index_map) for c in casts]
    outs = pl.pallas_call(
        functools.partial(_ffn_kernel, n_casts=len(casts), cast_scales=tuple(c.scale for c in casts)),
        grid=(s // tm, f // tf),
        in_specs=[
            pl.BlockSpec((tm, d), lambda i, j: (i, 0)),
            pl.BlockSpec((1, d), lambda i, j: (0, 0)),
            pl.BlockSpec((d, tf), lambda i, j: (0, j)),
            pl.BlockSpec((d, tf), lambda i, j: (0, j)),
            pl.BlockSpec((tf, d), lambda i, j: (j, 0)),
        ] + cast_specs,
        out_specs=[pl.BlockSpec((tm, d), lambda i, j: (i, 0))] + cast_specs,
        out_shape=[jax.ShapeDtypeStruct((s, d), F32)]
        + [jax.ShapeDtypeStruct(c.weight.shape, BF16) for c in casts],
        scratch_shapes=[pltpu.VMEM((tm, d), BF16)],
        compiler_params=pltpu.CompilerParams(
            dimension_semantics=("arbitrary", "arbitrary"), vmem_limit_bytes=VMEM_LIMIT),
        name="ffn",
    )(x, g, wg, wu, wd_half, *[c.weight for c in casts])
    return outs


def _group_sumsq(p, bd_ref):
    p2 = (p * p).astype(BF16)
    outs = [jnp.dot(p2[:, c * MXU_DIM:(c + 1) * MXU_DIM], bd_ref[...], preferred_element_type=F32)
            for c in range(p.shape[1] // MXU_DIM)]
    return jnp.concatenate(outs, axis=1)


def _qk_norm_rope(p, gain, bd_ref, rc_ref, ra_ref, rb_ref, scale):
    ss = _group_sumsq(p, bd_ref)
    xn = p * lax.rsqrt(ss * (1.0 / DIFF_QKDIM) + EPS) * gain
    rc, ra, rb = rc_ref[...], ra_ref[...], rb_ref[...]
    outs = []
    for c in range(p.shape[1] // LANES):
        xc = xn[:, c * LANES:(c + 1) * LANES]
        up = pltpu.roll(xc, LANES - ROT_DIM // 2, axis=1)
        dn = pltpu.roll(xc, ROT_DIM // 2, axis=1)
        outs.append(xc * rc + up * ra + dn * rb)
    out = jnp.concatenate(outs, axis=1)
    if scale != 1.0:
        out = out * scale
    return out.astype(BF16)


PROJ_TM = 512
_PROJ_ROWS = 256


def _proj_kernel(*refs, n_casts, cast_scales):
    x_ref, g_ref, w_ref, qg_ref, kg_ref, bd_ref, rc_ref, ra_ref, rb_ref = refs[:9]
    cast_in = refs[9:9 + n_casts]
    qk_ref, vt_ref, u_ref = refs[9 + n_casts:12 + n_casts]
    cast_out = refs[12 + n_casts:]
    _cast_blocks(cast_in, cast_out, cast_scales)
    w = u_ref.shape[1]
    q_scale = math.log2(math.e) / math.sqrt(DIFF_QKDIM)
    for r in range(x_ref.shape[0] // _PROJ_ROWS):
        rows = slice(r * _PROJ_ROWS, (r + 1) * _PROJ_ROWS)
        h = _rms_rows(x_ref[rows, :], g_ref[...]).astype(BF16)
        rope = (rc_ref.at[rows, :], ra_ref.at[rows, :], rb_ref.at[rows, :])
        col = lambda n: jnp.dot(h, w_ref[:, n * w:(n + 1) * w], preferred_element_type=F32)
        qk_ref[0, rows, :] = _qk_norm_rope(col(0), qg_ref[...], bd_ref, *rope, q_scale)
        qk_ref[1, rows, :] = _qk_norm_rope(col(1), kg_ref[...], bd_ref, *rope, 1.0)
        vt_ref[:, rows] = col(2).T.astype(BF16)
        u_ref[rows, :] = col(3) * jax.nn.sigmoid(col(4))


def _proj(x, g, w_in, qg, kg, bd, rc, ra, rb, casts=(), *, tm=PROJ_TM):
    s, d = x.shape
    n_cols = w_in.shape[1]
    w = n_cols // 5
    once = pl.Buffered(1)
    cast_specs = [pl.BlockSpec(c.block, c.index_map) for c in casts]
    return pl.pallas_call(
        functools.partial(_proj_kernel, n_casts=len(casts), cast_scales=tuple(c.scale for c in casts)),
        grid=(s // tm,),
        in_specs=[
            pl.BlockSpec((tm, d), lambda i: (i, 0)),
            pl.BlockSpec((1, d), lambda i: (0, 0)),
            pl.BlockSpec((d, n_cols), lambda i: (0, 0), pipeline_mode=once),
            pl.BlockSpec((1, w), lambda i: (0, 0)),
            pl.BlockSpec((1, w), lambda i: (0, 0)),
            pl.BlockSpec((MXU_DIM, MXU_DIM), lambda i: (0, 0)),
            pl.BlockSpec((tm, LANES), lambda i: (i, 0)),
            pl.BlockSpec((tm, LANES), lambda i: (i, 0)),
            pl.BlockSpec((tm, LANES), lambda i: (i, 0)),
        ] + cast_specs,
        out_specs=[
            pl.BlockSpec((2, tm, w), lambda i: (0, i, 0)),
            pl.BlockSpec((w, tm), lambda i: (0, i)),
            pl.BlockSpec((tm, w), lambda i: (i, 0)),
        ] + cast_specs,
        out_shape=[
            jax.ShapeDtypeStruct((2, s, w), BF16),
            jax.ShapeDtypeStruct((w, s), BF16),
            jax.ShapeDtypeStruct((s, w), F32),
        ] + [jax.ShapeDtypeStruct(c.weight.shape, BF16) for c in casts],
        compiler_params=pltpu.CompilerParams(
            dimension_semantics=("arbitrary",), vmem_limit_bytes=VMEM_LIMIT),
        name="proj",
    )(x, g, w_in, qg, kg, bd, rc, ra, rb, *[c.weight for c in casts])


_ONES_ROWS = 16
_CHAIN_GROUP = 2


def _attn_kernel(q_ref, k_ref, vt_ref, lq1_ref, lk1_ref, lq2_ref, lk2_ref, hg_ref, o_ref,
                 acc1, acc2, m1, m2, s_ref, *, t, ts):
    i = pl.program_id(1)
    qt = q_ref[...].astype(F32).T
    row = lax.broadcasted_iota(jnp.int32, qt.shape, 0)
    qz = (jnp.where(row < DIFF_QKDIM, qt, 0.0).astype(BF16), jnp.where(row >= DIFF_QKDIM, qt, 0.0).astype(BF16))
    accs, ms = (acc1, acc2), (m1, m2)
    ones = jnp.ones((_ONES_ROWS, t), BF16)
    chains = [(st, c) for st in range(t // ts) for c in range(2)]

    def score(j, chain, diagonal):
        st, c = chain
        start = pl.multiple_of(j * t, t)
        nk = (st + 1) * ts if diagonal else t
        kb = k_ref[pl.ds(start, nk), :]
        s = jnp.dot(kb, qz[c][:, st * ts:(st + 1) * ts], preferred_element_type=F32)
        if diagonal:
            krow = lax.broadcasted_iota(jnp.int32, s.shape, 0)
            qcol = lax.broadcasted_iota(jnp.int32, s.shape, 1) + st * ts
            s = jnp.where(krow <= qcol, s, NEG_BIG)
        s_ref[2 * st + c, 0:nk, :] = s
        return jnp.max(s, axis=0, keepdims=True), nk

    def softmax(s_max, chain, diagonal):
        st, c = chain
        cols = slice(st * ts, (st + 1) * ts)
        m_blk, nk = s_max
        if diagonal:
            m_new, alpha = m_blk, None
        else:
            m_old = ms[c][:, cols]
            m_new = jnp.maximum(m_old, m_blk)
            alpha = jnp.exp2(m_old - m_new)
        ms[c][:, cols] = m_new
        return jnp.exp2(s_ref[2 * st + c, 0:nk, :] - m_new).astype(BF16), alpha

    def pv(j, chain, prob):
        st, c = chain
        p, alpha = prob
        nk = p.shape[0]
        start = pl.multiple_of(j * t, t)
        cols = slice(st * ts, (st + 1) * ts)
        vb = jnp.concatenate([vt_ref[:, pl.ds(start, nk)], ones[:, :nk]], axis=0)
        acc_new = jnp.dot(vb, p, preferred_element_type=F32)
        accs[c][st] = acc_new if alpha is None else alpha * accs[c][st] + acc_new

    def blocks(specs):
        groups = [(j, diag, chains[g:g + _CHAIN_GROUP])
                  for j, diag in specs for g in range(0, len(chains), _CHAIN_GROUP)]
        j0, diag0, first = groups[0]
        maxima = [score(j0, chain, diag0) for chain in first]
        for g, (j, diag, group) in enumerate(groups):
            nxt = []
            if g + 1 < len(groups):
                j_nxt, diag_nxt, group_nxt = groups[g + 1]
                nxt = [score(j_nxt, chain, diag_nxt) for chain in group_nxt]
            probs = [softmax(mx, chain, diag) for chain, mx in zip(group, maxima)]
            for chain, prob in zip(group, probs):
                pv(j, chain, prob)
            maxima = nxt

    def run(first, count):
        blocks([(first + n, False) for n in range(count)])

    rest = jnp.maximum(i - 1, 0)
    odd = rest % 2 == 1
    pl.when(i == 0)(lambda: blocks([(i, True)]))
    pl.when(jnp.logical_and(i > 0, jnp.logical_not(odd)))(lambda: blocks([(i, True), (0, False)]))
    pl.when(odd)(lambda: blocks([(i, True), (0, False), (rest, False)]))
    quads = rest // 4

    def body(n, carry):
        run(1 + 4 * n, 4)
        return carry

    lax.fori_loop(0, quads, body, 0)
    pl.when(rest % 4 >= 2)(lambda: run(1 + 4 * quads, 2))

    lam = (jnp.exp(jnp.sum(lq1_ref[...] * lk1_ref[...], axis=-1, keepdims=True))
           - jnp.exp(jnp.sum(lq2_ref[...] * lk2_ref[...], axis=-1, keepdims=True))
           + LAMBDA_INIT)
    nv = DIFF_VDIM
    for st in range(t // ts):
        ot = (acc1[st, 0:nv, :] / acc1[st, nv:nv + 1, :]
              - lam * (acc2[st, 0:nv, :] / acc2[st, nv:nv + 1, :]))
        mean_sq = jnp.mean(ot * ot, axis=0, keepdims=True)
        on = ot * lax.rsqrt(mean_sq + EPS)
        o_ref[st * ts:(st + 1) * ts, :] = (on.T * hg_ref[...] * (1.0 - LAMBDA_INIT)).astype(BF16)


def _attn(qk, vt, lq1, lk1, lq2, lk2, hg, *, t=1024, ts=256):
    _, s, w = qk.shape
    nh = w // DIFF_VDIM
    assert s % t == 0 and t % ts == 0 and ts % MXU_DIM == 0
    vec = lambda n: pl.BlockSpec((1, n), lambda h, i: (0, 0))
    return pl.pallas_call(
        functools.partial(_attn_kernel, t=t, ts=ts),
        grid=(nh, s // t),
        in_specs=[
            pl.BlockSpec((None, t, DIFF_VDIM), lambda h, i: (0, i, h)),
            pl.BlockSpec((None, s, DIFF_VDIM), lambda h, i: (1, 0, h)),
            pl.BlockSpec((DIFF_VDIM, s), lambda h, i: (h, 0)),
            vec(DIFF_QKDIM), vec(DIFF_QKDIM), vec(DIFF_QKDIM), vec(DIFF_QKDIM),
            vec(DIFF_VDIM),
        ],
        out_specs=pl.BlockSpec((t, DIFF_VDIM), lambda h, i: (i, h)),
        out_shape=jax.ShapeDtypeStruct((s, w), BF16),
        scratch_shapes=[
            pltpu.VMEM((t // ts, DIFF_VDIM + _ONES_ROWS, ts), F32),
            pltpu.VMEM((t // ts, DIFF_VDIM + _ONES_ROWS, ts), F32),
            pltpu.VMEM((1, t), F32), pltpu.VMEM((1, t), F32),
            pltpu.VMEM((2 * (t // ts), t, ts), F32),
        ],
        compiler_params=pltpu.CompilerParams(
            dimension_semantics=("arbitrary", "arbitrary"), vmem_limit_bytes=VMEM_LIMIT),
        name="attn",
    )(qk, qk, vt, lq1, lk1, lq2, lk2, hg)


_HALO = 32
_SUB = 8
_ROWS = 32


def _conv_kernel(cur_ref, halo_ref, w_ref, b_ref, y_ref, e_ref, *, tc):
    i = pl.program_id(0)
    e_ref[0, 0:_HALO, :] = jnp.where(i > 0, halo_ref[...], 0.0)
    e_ref[0, _HALO:_HALO + tc, :] = cur_ref[...]
    span = tc + _HALO - _SUB
    for r in range(1, _SUB):
        e_ref[r, 0:span, :] = e_ref[0, r:r + span, :]

    first = _HALO - (CONV_K - 1)
    n_sub = _ROWS // _SUB

    width = cur_ref.shape[1]
    n_lane_blocks = 2

    def chunk(c, carry):
        t0 = pl.multiple_of(c * _ROWS, _ROWS)
        for lb in range(n_lane_blocks):
            lanes = slice(lb * width // n_lane_blocks, (lb + 1) * width // n_lane_blocks)
            acc = [b_ref[:, lanes] for _ in range(n_sub)]
            for r in range(_SUB):
                taps_r = [(k, (first + k) // _SUB) for k in range(CONV_K) if (first + k) % _SUB == r]
                weights = {k: w_ref[k, :, lanes] for k, _ in taps_r}
                for idx in range(max(a for _, a in taps_r) + n_sub):
                    uses = [(k, idx - a) for k, a in taps_r if 0 <= idx - a < n_sub]
                    if not uses:
                        continue
                    tile = e_ref[r, pl.ds(t0 + idx * _SUB, _SUB), lanes]
                    for k, q in uses:
                        acc[q] = acc[q] + weights[k] * tile
            for q in range(n_sub):
                y_ref[pl.ds(t0 + q * _SUB, _SUB), lanes] = acc[q]
        return carry

    lax.fori_loop(0, tc // _ROWS, chunk, 0, unroll=2)


def _conv(u, taps, bias, *, tc=512):
    s, c = u.shape
    assert s % tc == 0 and tc % _HALO == 0 and tc % (2 * _ROWS) == 0 and _HALO >= CONV_K - 1
    halo_blocks = tc // _HALO
    return pl.pallas_call(
        functools.partial(_conv_kernel, tc=tc),
        grid=(s // tc,),
        in_specs=[
            pl.BlockSpec((tc, c), lambda i: (i, 0)),
            pl.BlockSpec((_HALO, c), lambda i: (jnp.maximum(i * halo_blocks - 1, 0), 0)),
            pl.BlockSpec((CONV_K, _SUB, c), lambda i: (0, 0, 0)),
            pl.BlockSpec((_SUB, c), lambda i: (0, 0)),
        ],
        out_specs=pl.BlockSpec((tc, c), lambda i: (i, 0)),
        out_shape=jax.ShapeDtypeStruct((s, c), F32),
        scratch_shapes=[pltpu.VMEM((_SUB, tc + _HALO, c), F32)],
        compiler_params=pltpu.CompilerParams(
            dimension_semantics=("arbitrary",), vmem_limit_bytes=VMEM_LIMIT),
        name="conv",
    )(u, u, taps, bias)


def _memkv_kernel(mem_ref, g_ref, w_ref, kg_ref, k_ref, v_ref):
    h = _rms_rows(mem_ref[...], g_ref[...]).astype(BF16)
    kv = jnp.dot(h, w_ref[...], preferred_element_type=F32)
    width = v_ref.shape[1]
    for hd in range(XATTN_HEADS):
        sl = slice(hd * XATTN_HDIM, (hd + 1) * XATTN_HDIM)
        k_ref[sl, :] = _rms_rows(kv[:, sl], kg_ref[...]).T.astype(BF16)
    v_ref[...] = kv[:, width:].astype(BF16)


def _memkv(mem, g, w_kv, kg):
    n, _ = mem.shape
    width = w_kv.shape[1] // 2
    return pl.pallas_call(
        _memkv_kernel,
        out_shape=[jax.ShapeDtypeStruct((width, n), BF16), jax.ShapeDtypeStruct((n, width), BF16)],
        compiler_params=pltpu.CompilerParams(vmem_limit_bytes=VMEM_LIMIT),
        name="memkv",
    )(mem, g, w_kv, kg)


def _post_kernel(x_ref, od_ref, y_ref, lng_ref, lnb_ref, wo_ref, g_ref, wq_ref, qg_ref, kmt_ref, vm_ref,
                 wxo_ref, o_ref):
    half = od_ref.shape[1]
    x2 = x_ref[...] + jnp.dot(od_ref[...], wo_ref[0:half, :], preferred_element_type=F32)
    y = y_ref[...]
    yc = y - jnp.mean(y, axis=-1, keepdims=True)
    var = jnp.mean(yc * yc, axis=-1, keepdims=True)
    yn = yc * lax.rsqrt(var + EPS) * lng_ref[...] + lnb_ref[...]
    oc = (yn * jax.nn.sigmoid(yn)).astype(BF16)
    x2 = x2 + jnp.dot(oc, wo_ref[half:, :], preferred_element_type=F32)
    h = _rms_rows(x2, g_ref[...]).astype(BF16)
    q = jnp.dot(h, wq_ref[...], preferred_element_type=F32)
    scale = 1.0 / math.sqrt(XATTN_HDIM)
    heads = [slice(hd * XATTN_HDIM, (hd + 1) * XATTN_HDIM) for hd in range(XATTN_HEADS)]
    scores = [jnp.dot(_rms_rows(q[:, sl], qg_ref[...]).astype(BF16), kmt_ref[sl, :],
                      preferred_element_type=F32) * scale for sl in heads]
    probs = []
    for sc in scores:
        p = jnp.exp(sc - jnp.max(sc, axis=-1, keepdims=True))
        probs.append((p / jnp.sum(p, axis=-1, keepdims=True)).astype(BF16))
    outs = [jnp.dot(p, vm_ref[:, sl], preferred_element_type=F32) for p, sl in zip(probs, heads)]
    o = jnp.concatenate(outs, axis=1).astype(BF16)
    o_ref[...] = x2 + jnp.dot(o, wxo_ref[...], preferred_element_type=F32)


def _post(x, od, y_conv, ln_g, ln_b, w_out, g, wq, qg, kmt, vm, wxo, *, tm=512):
    s, d = x.shape
    half = od.shape[1]
    xw = wq.shape[1]
    n_mem = vm.shape[0]
    const = lambda shape: pl.BlockSpec(shape, lambda i: (0, 0))
    return pl.pallas_call(
        _post_kernel,
        grid=(s // tm,),
        in_specs=[
            pl.BlockSpec((tm, d), lambda i: (i, 0)),
            pl.BlockSpec((tm, half), lambda i: (i, 0)),
            pl.BlockSpec((tm, half), lambda i: (i, 0)),
            const((1, half)), const((1, half)),
            const((2 * half, d)), const((1, d)), const((d, xw)), const((1, XATTN_HDIM)),
            const((xw, n_mem)), const((n_mem, xw)), const((xw, d)),
        ],
        out_specs=pl.BlockSpec((tm, d), lambda i: (i, 0)),
        out_shape=jax.ShapeDtypeStruct((s, d), F32),
        compiler_params=pltpu.CompilerParams(
            dimension_semantics=("arbitrary",), vmem_limit_bytes=VMEM_LIMIT),
        name="post",
    )(x, od, y_conv, ln_g, ln_b, w_out, g, wq, qg, kmt, vm, wxo)


def _rope_tables(positions):
    half = ROT_DIM // 2
    inv_freq = ROPE_THETA ** (-jnp.arange(0, ROT_DIM, 2, dtype=F32) / ROT_DIM)
    ang = positions.astype(F32)[:, None] * inv_freq
    cos, sin = jnp.cos(ang), jnp.sin(ang)
    n = positions.shape[0]
    rest = DIFF_QKDIM - ROT_DIM
    rc = jnp.concatenate([cos, cos, jnp.ones((n, rest), F32)], axis=1)
    ra = jnp.concatenate([-sin, jnp.zeros((n, half + rest), F32)], axis=1)
    rb = jnp.concatenate([jnp.zeros((n, half), F32), sin, jnp.zeros((n, rest), F32)], axis=1)
    rep = LANES // DIFF_QKDIM
    return jnp.tile(rc, (1, rep)), jnp.tile(ra, (1, rep)), jnp.tile(rb, (1, rep))


def kernel(x, mem, positions, ffn1_norm_g, ffn1_w_gate, ffn1_w_up, ffn1_w_down, mix_norm_g, w_in, q_norm_g, k_norm_g, lambda_q1, lambda_k1, lambda_q2, lambda_k2, diff_head_norm_g, conv_w, conv_b, conv_ln_g, conv_ln_b, w_out, xattn_norm_g, mem_norm_g, xattn_w_q, xattn_w_kv, xattn_q_norm_g, xattn_k_norm_g, xattn_w_o, ffn2_norm_g, ffn2_w_gate, ffn2_w_up, ffn2_w_down):
    b, s, d = x.shape
    assert b == 1 and ffn1_norm_g.shape[0] == 1, "one sequence, one layer"
    assert s % FFN_TM == 0 and s % PROJ_TM == 0 and ffn1_w_gate.shape[-1] % FFN_TF == 0
    bf = lambda w: w[0].astype(BF16)
    row = lambda v: v[0][None, :]

    rc, ra, rb = _rope_tables(positions[0])
    width = DIFF_HEADS * DIFF_VDIM
    qg = jnp.tile(q_norm_g[0], width // DIFF_QKDIM)[None, :]
    kg = jnp.tile(k_norm_g[0], width // DIFF_QKDIM)[None, :]
    blk = jnp.arange(MXU_DIM) // DIFF_QKDIM
    bd = (blk[:, None] == blk[None, :]).astype(BF16)

    n_i, n_j = s // FFN_TM, ffn1_w_gate.shape[-1] // FFN_TF

    def by_rows(w, scale=1.0):
        r, c = w.shape
        assert r % (n_i * _BF16_ROWS) == 0 and c % FFN_TF == 0 and c // FFN_TF <= n_j, w.shape
        last = c // FFN_TF - 1
        return _HostedCast(w, (r // n_i, FFN_TF), lambda i, j: (i, jnp.minimum(j, last)), scale)

    def by_cols(w, scale=1.0):
        r, c = w.shape
        assert r % (n_j * _BF16_ROWS) == 0 and c % (n_i * LANES) == 0, w.shape
        return _HostedCast(w, (r // n_j, c // n_i), lambda i, j: (j, i), scale)

    x1, wg2, wu2, wd2_half, w_in_bf = _ffn(
        x[0], row(ffn1_norm_g), bf(ffn1_w_gate), bf(ffn1_w_up), (0.5 * ffn1_w_down[0]).astype(BF16),
        [by_rows(ffn2_w_gate[0]), by_rows(ffn2_w_up[0]), by_cols(ffn2_w_down[0], 0.5), by_rows(w_in[0])])

    n_p = s // PROJ_TM

    def row_block(w):
        assert w.shape[0] % (n_p * _BF16_ROWS) == 0, w.shape
        return _HostedCast(w, (w.shape[0] // n_p, w.shape[1]), lambda i: (i, 0))

    qk, vt, u, w_out_bf, wq_bf, wkv_bf, wo_bf = _proj(
        x1, row(mix_norm_g), w_in_bf, qg, kg, bd, rc, ra, rb,
        [row_block(w_out[0]), row_block(xattn_w_q[0]), row_block(xattn_w_kv[0]), row_block(xattn_w_o[0])])
    o_diff = _attn(qk, vt, row(lambda_q1), row(lambda_k1), row(lambda_q2), row(lambda_k2),
                   row(diff_head_norm_g))
    conv_taps = jnp.broadcast_to(conv_w[0][:, None, :], (CONV_K, _SUB, conv_w.shape[-1]))
    conv_bias = jnp.broadcast_to(conv_b[0][None, :], (_SUB, conv_b.shape[-1]))
    y_conv = _conv(u, conv_taps, conv_bias)

    km, vm = _memkv(mem[0], row(mem_norm_g), wkv_bf, row(xattn_k_norm_g))
    x3 = _post(x1, o_diff, y_conv, row(conv_ln_g), row(conv_ln_b), w_out_bf, row(xattn_norm_g), wq_bf,
               row(xattn_q_norm_g), km, vm, wo_bf)

    out = _ffn(x3, row(ffn2_norm_g), wg2, wu2, wd2_half)[0]
    return out[None]
```

```python
import functools
import math
from typing import Callable, NamedTuple

import jax
import jax.numpy as jnp
from jax import lax
from jax.experimental import pallas as pl
from jax.experimental.pallas import tpu as pltpu

F32 = jnp.float32
BF16 = jnp.bfloat16

EPS = 1e-6
ROPE_THETA = 500000.0
DIFF_HEADS = 8
DIFF_VDIM = 128
DIFF_QKDIM = 64
ROT_DIM = 16
CONV_K = 31
XATTN_HEADS = 4
XATTN_HDIM = 128
LAMBDA_INIT = 0.8 - 0.6 * math.exp(0.0)

LANES = 128
_BF16_ROWS = 16
MXU_DIM = 256
VMEM_LIMIT = 56 * 1024 * 1024

NEG_BIG = -1e30


def _rms_rows(x, g):
    ms = jnp.mean(x * x, axis=-1, keepdims=True)
    return x * lax.rsqrt(ms + EPS) * g


class _HostedCast(NamedTuple):
    weight: jax.Array
    block: tuple
    index_map: Callable
    scale: float = 1.0


def _cast_blocks(srcs, dsts, scales):
    for src, dst, scale in zip(srcs, dsts, scales):
        v = src[...]
        dst[...] = (v if scale == 1.0 else v * scale).astype(BF16)


FFN_TM = 1024
FFN_TF = 512
_FFN_ROWS = 256


def _ffn_kernel(*refs, n_casts, cast_scales):
    x_ref, g_ref, wg_ref, wu_ref, wd_ref = refs[:5]
    cast_in = refs[5:5 + n_casts]
    o_ref = refs[5 + n_casts]
    cast_out = refs[6 + n_casts:6 + 2 * n_casts]
    h_ref = refs[6 + 2 * n_casts]
    j = pl.program_id(1)

    n_chunks = o_ref.shape[0] // _FFN_ROWS

    def body(first):
        def gate_up(r):
            rows = slice(r * _FFN_ROWS, (r + 1) * _FFN_ROWS)
            if first:
                h = _rms_rows(x_ref[rows, :], g_ref[...]).astype(BF16)
                h_ref[rows, :] = h
            else:
                h = h_ref[rows, :]
            a = jnp.dot(h, wg_ref[...], preferred_element_type=F32)
            b = jnp.dot(h, wu_ref[...], preferred_element_type=F32)
            return a, b

        ab = gate_up(0)
        for r in range(n_chunks):
            a, b = ab
            if r + 1 < n_chunks:
                ab = gate_up(r + 1)
            act = (a * jax.nn.sigmoid(a) * b).astype(BF16)
            rows = slice(r * _FFN_ROWS, (r + 1) * _FFN_ROWS)
            base = x_ref[rows, :] if first else o_ref[rows, :]
            o_ref[rows, :] = base + jnp.dot(act, wd_ref[...], preferred_element_type=F32)
        _cast_blocks(cast_in, cast_out, cast_scales)

    pl.when(j == 0)(lambda: body(True))
    pl.when(j > 0)(lambda: body(False))


def _ffn(x, g, wg, wu, wd_half, casts=(), *, tm=FFN_TM, tf=FFN_TF):
    s, d = x.shape
    f = wg.shape[1]
    cast_specs = [pl.BlockSpec(c.block, c.index_map) for c in casts]
    outs = pl.pallas_call(
        functools.partial(_ffn_kernel, n_casts=len(casts), cast_scales=tuple(c.scale for c in casts)),
        grid=(s // tm, f // tf),
        in_specs=[
            pl.BlockSpec((tm, d), lambda i, j: (i, 0)),
            pl.BlockSpec((1, d), lambda i, j: (0, 0)),
            pl.BlockSpec((d, tf), lambda i, j: (0, j)),
            pl.BlockSpec((d, tf), lambda i, j: (0, j)),
            pl.BlockSpec((tf, d), lambda i, j: (j, 0)),
        ] + cast_specs,
        out_specs=[pl.BlockSpec((tm, d), lambda i, j: (i, 0))] + cast_specs,
        out_shape=[jax.ShapeDtypeStruct((s, d), F32)]
        + [jax.ShapeDtypeStruct(c.weight.shape, BF16) for c in casts],
        scratch_shapes=[pltpu.VMEM((tm, d), BF16)],
        compiler_params=pltpu.CompilerParams(
            dimension_semantics=("arbitrary", "arbitrary"), vmem_limit_bytes=VMEM_LIMIT),
        name="ffn",
    )(x, g, wg, wu, wd_half, *[c.weight for c in casts])
    return outs


def _group_sumsq(p, bd_ref):
    p2 = (p * p).astype(BF16)
    outs = [jnp.dot(p2[:, c * MXU_DIM:(c + 1) * MXU_DIM], bd_ref[...], preferred_element_type=F32)
            for c in range(p.shape[1] // MXU_DIM)]
    return jnp.concatenate(outs, axis=1)


def _qk_norm_rope(p, gain, bd_ref, rc_ref, ra_ref, rb_ref, scale):
    ss = _group_sumsq(p, bd_ref)
    xn = p * lax.rsqrt(ss * (1.0 / DIFF_QKDIM) + EPS) * gain
    rc, ra, rb = rc_ref[...], ra_ref[...], rb_ref[...]
    outs = []
    for c in range(p.shape[1] // LANES):
        xc = xn[:, c * LANES:(c + 1) * LANES]
        up = pltpu.roll(xc, LANES - ROT_DIM // 2, axis=1)
        dn = pltpu.roll(xc, ROT_DIM // 2, axis=1)
        outs.append(xc * rc + up * ra + dn * rb)
    out = jnp.concatenate(outs, axis=1)
    if scale != 1.0:
        out = out * scale
    return out.astype(BF16)


PROJ_TM = 512
_PROJ_ROWS = 256


def _proj_kernel(*refs, n_casts, cast_scales):
    x_ref, g_ref, w_ref, qg_ref, kg_ref, bd_ref, rc_ref, ra_ref, rb_ref = refs[:9]
    cast_in = refs[9:9 + n_casts]
    qk_ref, vt_ref, u_ref = refs[9 + n_casts:12 + n_casts]
    cast_out = refs[12 + n_casts:]
    _cast_blocks(cast_in, cast_out, cast_scales)
    w = u_ref.shape[1]
    q_scale = math.log2(math.e) / math.sqrt(DIFF_QKDIM)
    for r in range(x_ref.shape[0] // _PROJ_ROWS):
        rows = slice(r * _PROJ_ROWS, (r + 1) * _PROJ_ROWS)
        h = _rms_rows(x_ref[rows, :], g_ref[...]).astype(BF16)
        rope = (rc_ref.at[rows, :], ra_ref.at[rows, :], rb_ref.at[rows, :])
        col = lambda n: jnp.dot(h, w_ref[:, n * w:(n + 1) * w], preferred_element_type=F32)
        qk_ref[0, rows, :] = _qk_norm_rope(col(0), qg_ref[...], bd_ref, *rope, q_scale)
        qk_ref[1, rows, :] = _qk_norm_rope(col(1), kg_ref[...], bd_ref, *rope, 1.0)
        vt_ref[:, rows] = col(2).T.astype(BF16)
        u_ref[rows, :] = col(3) * jax.nn.sigmoid(col(4))


def _proj(x, g, w_in, qg, kg, bd, rc, ra, rb, casts=(), *, tm=PROJ_TM):
    s, d = x.shape
    n_cols = w_in.shape[1]
    w = n_cols // 5
    once = pl.Buffered(1)
    cast_specs = [pl.BlockSpec(c.block, c.index_map) for c in casts]
    return pl.pallas_call(
        functools.partial(_proj_kernel, n_casts=len(casts), cast_scales=tuple(c.scale for c in casts)),
        grid=(s // tm,),
        in_specs=[
            pl.BlockSpec((tm, d), lambda i: (i, 0)),
            pl.BlockSpec((1, d), lambda i: (0, 0)),
            pl.BlockSpec((d, n_cols), lambda i: (0, 0), pipeline_mode=once),
            pl.BlockSpec((1, w), lambda i: (0, 0)),
            pl.BlockSpec((1, w), lambda i: (0, 0)),
            pl.BlockSpec((MXU_DIM, MXU_DIM), lambda i: (0, 0)),
            pl.BlockSpec((tm, LANES), lambda i: (i, 0)),
            pl.BlockSpec((tm, LANES), lambda i: (i, 0)),
            pl.BlockSpec((tm, LANES), lambda i: (i, 0)),
        ] + cast_specs,
        out_specs=[
            pl.BlockSpec((2, tm, w), lambda i: (0, i, 0)),
            pl.BlockSpec((w, tm), lambda i: (0, i)),
            pl.BlockSpec((tm, w), lambda i: (i, 0)),
        ] + cast_specs,
        out_shape=[
            jax.ShapeDtypeStruct((2, s, w), BF16),
            jax.ShapeDtypeStruct((w, s), BF16),
            jax.ShapeDtypeStruct((s, w), F32),
        ] + [jax.ShapeDtypeStruct(c.weight.shape, BF16) for c in casts],
        compiler_params=pltpu.CompilerParams(
            dimension_semantics=("arbitrary",), vmem_limit_bytes=VMEM_LIMIT),
        name="proj",
    )(x, g, w_in, qg, kg, bd, rc, ra, rb, *[c.weight for c in casts])


_ONES_ROWS = 16
_CHAIN_GROUP = 2


def _attn_kernel(q_ref, k_ref, vt_ref, lq1_ref, lk1_ref, lq2_ref, lk2_ref, hg_ref, o_ref,
                 acc1, acc2, m1, m2, s_ref, *, t, ts):
    i = pl.program_id(1)
    qt = q_ref[...].astype(F32).T
    row = lax.broadcasted_iota(jnp.int32, qt.shape, 0)
    qz = (jnp.where(row < DIFF_QKDIM, qt, 0.0).astype(BF16), jnp.where(row >= DIFF_QKDIM, qt, 0.0).astype(BF16))
    accs, ms = (acc1, acc2), (m1, m2)
    ones = jnp.ones((_ONES_ROWS, t), BF16)
    chains = [(st, c) for st in range(t // ts) for c in range(2)]

    def score(j, chain, diagonal):
        st, c = chain
        start = pl.multiple_of(j * t, t)
        nk = (st + 1) * ts if diagonal else t
        kb = k_ref[pl.ds(start, nk), :]
        s = jnp.dot(kb, qz[c][:, st * ts:(st + 1) * ts], preferred_element_type=F32)
        if diagonal:
            krow = lax.broadcasted_iota(jnp.int32, s.shape, 0)
            qcol = lax.broadcasted_iota(jnp.int32, s.shape, 1) + st * ts
            s = jnp.where(krow <= qcol, s, NEG_BIG)
        s_ref[2 * st + c, 0:nk, :] = s
        return jnp.max(s, axis=0, keepdims=True), nk

    def softmax(s_max, chain, diagonal):
        st, c = chain
        cols = slice(st * ts, (st + 1) * ts)
        m_blk, nk = s_max
        if diagonal:
            m_new, alpha = m_blk, None
        else:
            m_old = ms[c][:, cols]
            m_new = jnp.maximum(m_old, m_blk)
            alpha = jnp.exp2(m_old - m_new)
        ms[c][:, cols] = m_new
        return jnp.exp2(s_ref[2 * st + c, 0:nk, :] - m_new).astype(BF16), alpha

    def pv(j, chain, prob):
        st, c = chain
        p, alpha = prob
        nk = p.shape[0]
        start = pl.multiple_of(j * t, t)
        cols = slice(st * ts, (st + 1) * ts)
        vb = jnp.concatenate([vt_ref[:, pl.ds(start, nk)], ones[:, :nk]], axis=0)
        acc_new = jnp.dot(vb, p, preferred_element_type=F32)
        accs[c][st] = acc_new if alpha is None else alpha * accs[c][st] + acc_new

    def blocks(specs):
        groups = [(j, diag, chains[g:g + _CHAIN_GROUP])
                  for j, diag in specs for g in range(0, len(chains), _CHAIN_GROUP)]
        j0, diag0, first = groups[0]
        maxima = [score(j0, chain, diag0) for chain in first]
        for g, (j, diag, group) in enumerate(groups):
            nxt = []
            if g + 1 < len(groups):
                j_nxt, diag_nxt, group_nxt = groups[g + 1]
                nxt = [score(j_nxt, chain, diag_nxt) for chain in group_nxt]
            probs = [softmax(mx, chain, diag) for chain, mx in zip(group, maxima)]
            for chain, prob in zip(group, probs):
                pv(j, chain, prob)
            maxima = nxt

    def run(first, count):
        blocks([(first + n, False) for n in range(count)])

    rest = jnp.maximum(i - 1, 0)
    odd = rest % 2 == 1
    pl.when(i == 0)(lambda: blocks([(i, True)]))
    pl.when(jnp.logical_and(i > 0, jnp.logical_not(odd)))(lambda: blocks([(i, True), (0, False)]))
    pl.when(odd)(lambda: blocks([(i, True), (0, False), (rest, False)]))
    quads = rest // 4

    def body(n, carry):
        run(1 + 4 * n, 4)
        return carry

    lax.fori_loop(0, quads, body, 0)
    pl.when(rest % 4 >= 2)(lambda: run(1 + 4 * quads, 2))

    lam = (jnp.exp(jnp.sum(lq1_ref[...] * lk1_ref[...], axis=-1, keepdims=True))
           - jnp.exp(jnp.sum(lq2_ref[...] * lk2_ref[...], axis=-1, keepdims=True))
           + LAMBDA_INIT)
    nv = DIFF_VDIM
    for st in range(t // ts):
        ot = (acc1[st, 0:nv, :] / acc1[st, nv:nv + 1, :]
              - lam * (acc2[st, 0:nv, :] / acc2[st, nv:nv + 1, :]))
        mean_sq = jnp.mean(ot * ot, axis=0, keepdims=True)
        on = ot * lax.rsqrt(mean_sq + EPS)
        o_ref[st * ts:(st + 1) * ts, :] = (on.T * hg_ref[...] * (1.0 - LAMBDA_INIT)).astype(BF16)


def _attn(qk, vt, lq1, lk1, lq2, lk2, hg, *, t=1024, ts=256):
    _, s, w = qk.shape
    nh = w // DIFF_VDIM
    assert s % t == 0 and t % ts == 0 and ts % MXU_DIM == 0
    vec = lambda n: pl.BlockSpec((1, n), lambda h, i: (0, 0))
    return pl.pallas_call(
        functools.partial(_attn_kernel, t=t, ts=ts),
        grid=(nh, s // t),
        in_specs=[
            pl.BlockSpec((None, t, DIFF_VDIM), lambda h, i: (0, i, h)),
            pl.BlockSpec((None, s, DIFF_VDIM), lambda h, i: (1, 0, h)),
            pl.BlockSpec((DIFF_VDIM, s), lambda h, i: (h, 0)),
            vec(DIFF_QKDIM), vec(DIFF_QKDIM), vec(DIFF_QKDIM), vec(DIFF_QKDIM),
            vec(DIFF_VDIM),
        ],
        out_specs=pl.BlockSpec((t, DIFF_VDIM), lambda h, i: (i, h)),
        out_shape=jax.ShapeDtypeStruct((s, w), BF16),
        scratch_shapes=[
            pltpu.VMEM((t // ts, DIFF_VDIM + _ONES_ROWS, ts), F32),
            pltpu.VMEM((t // ts, DIFF_VDIM + _ONES_ROWS, ts), F32),
            pltpu.VMEM((1, t), F32), pltpu.VMEM((1, t), F32),
            pltpu.VMEM((2 * (t // ts), t, ts), F32),
        ],
        compiler_params=pltpu.CompilerParams(
            dimension_semantics=("arbitrary", "arbitrary"), vmem_limit_bytes=VMEM_LIMIT),
        name="attn",
    )(qk, qk, vt, lq1, lk1, lq2, lk2, hg)


_HALO = 32
_SUB = 8
_ROWS = 32


def _conv_kernel(cur_ref, halo_ref, w_ref, b_ref, y_ref, e_ref, *, tc):
    i = pl.program_id(0)
    e_ref[0, 0:_HALO, :] = jnp.where(i > 0, halo_ref[...], 0.0)
    e_ref[0, _HALO:_HALO + tc, :] = cur_ref[...]
    span = tc + _HALO - _SUB
    for r in range(1, _SUB):
        e_ref[r, 0:span, :] = e_ref[0, r:r + span, :]

    first = _HALO - (CONV_K - 1)
    n_sub = _ROWS // _SUB

    width = cur_ref.shape[1]
    n_lane_blocks = 2

    def chunk(c, carry):
        t0 = pl.multiple_of(c * _ROWS, _ROWS)
        for lb in range(n_lane_blocks):
            lanes = slice(lb * width // n_lane_blocks, (lb + 1) * width // n_lane_blocks)
            acc = [b_ref[:, lanes] for _ in range(n_sub)]
            for r in range(_SUB):
                taps_r = [(k, (first + k) // _SUB) for k in range(CONV_K) if (first + k) % _SUB == r]
                weights = {k: w_ref[k, :, lanes] for k, _ in taps_r}
                for idx in range(max(a for _, a in taps_r) + n_sub):
                    uses = [(k, idx - a) for k, a in taps_r if 0 <= idx - a < n_sub]
                    if not uses:
                        continue
                    tile = e_ref[r, pl.ds(t0 + idx * _SUB, _SUB), lanes]
                    for k, q in uses:
                        acc[q] = acc[q] + weights[k] * tile
            for q in range(n_sub):
                y_ref[pl.ds(t0 + q * _SUB, _SUB), lanes] = acc[q]
        return carry

    lax.fori_loop(0, tc // _ROWS, chunk, 0, unroll=2)


def _conv(u, taps, bias, *, tc=512):
    s, c = u.shape
    assert s % tc == 0 and tc % _HALO == 0 and tc % (2 * _ROWS) == 0 and _HALO >= CONV_K - 1
    halo_blocks = tc // _HALO
    return pl.pallas_call(
        functools.partial(_conv_kernel, tc=tc),
        grid=(s // tc,),
        in_specs=[
            pl.BlockSpec((tc, c), lambda i: (i, 0)),
            pl.BlockSpec((_HALO, c), lambda i: (jnp.maximum(i * halo_blocks - 1, 0), 0)),
            pl.BlockSpec((CONV_K, _SUB, c), lambda i: (0, 0, 0)),
            pl.BlockSpec((_SUB, c), lambda i: (0, 0)),
        ],
        out_specs=pl.BlockSpec((tc, c), lambda i: (i, 0)),
        out_shape=jax.ShapeDtypeStruct((s, c), F32),
        scratch_shapes=[pltpu.VMEM((_SUB, tc + _HALO, c), F32)],
        compiler_params=pltpu.CompilerParams(
            dimension_semantics=("arbitrary",), vmem_limit_bytes=VMEM_LIMIT),
        name="conv",
    )(u, u, taps, bias)


def _memkv_kernel(mem_ref, g_ref, w_ref, kg_ref, k_ref, v_ref):
    h = _rms_rows(mem_ref[...], g_ref[...]).astype(BF16)
    kv = jnp.dot(h, w_ref[...], preferred_element_type=F32)
    width = v_ref.shape[1]
    for hd in range(XATTN_HEADS):
        sl = slice(hd * XATTN_HDIM, (hd + 1) * XATTN_HDIM)
        k_ref[sl, :] = _rms_rows(kv[:, sl], kg_ref[...]).T.astype(BF16)
    v_ref[...] = kv[:, width:].astype(BF16)


def _memkv(mem, g, w_kv, kg):
    n, _ = mem.shape
    width = w_kv.shape[1] // 2
    return pl.pallas_call(
        _memkv_kernel,
        out_shape=[jax.ShapeDtypeStruct((width, n), BF16), jax.ShapeDtypeStruct((n, width), BF16)],
        compiler_params=pltpu.CompilerParams(vmem_limit_bytes=VMEM_LIMIT),
        name="memkv",
    )(mem, g, w_kv, kg)


def _post_kernel(x_ref, od_ref, y_ref, lng_ref, lnb_ref, wo_ref, g_ref, wq_ref, qg_ref, mem_ref, mg_ref,
                 wkv_ref, kg_ref, wxo_ref, o_ref, kmt_ref, vm_ref):
    half = od_ref.shape[1]

    pl.when(pl.program_id(0) == 0)(lambda: _memkv_kernel(mem_ref, mg_ref, wkv_ref, kg_ref, kmt_ref, vm_ref))

    x2 = x_ref[...] + jnp.dot(od_ref[...], wo_ref[0:half, :], preferred_element_type=F32)
    y = y_ref[...]
    yc = y - jnp.mean(y, axis=-1, keepdims=True)
    var = jnp.mean(yc * yc, axis=-1, keepdims=True)
    yn = yc * lax.rsqrt(var + EPS) * lng_ref[...] + lnb_ref[...]
    oc = (yn * jax.nn.sigmoid(yn)).astype(BF16)
    x2 = x2 + jnp.dot(oc, wo_ref[half:, :], preferred_element_type=F32)
    h = _rms_rows(x2, g_ref[...]).astype(BF16)
    q = jnp.dot(h, wq_ref[...], preferred_element_type=F32)
    scale = 1.0 / math.sqrt(XATTN_HDIM)
    heads = [slice(hd * XATTN_HDIM, (hd + 1) * XATTN_HDIM) for hd in range(XATTN_HEADS)]
    scores = [jnp.dot(_rms_rows(q[:, sl], qg_ref[...]).astype(BF16), kmt_ref[sl, :],
                      preferred_element_type=F32) * scale for sl in heads]
    probs = []
    for sc in scores:
        p = jnp.exp(sc - jnp.max(sc, axis=-1, keepdims=True))
        probs.append((p / jnp.sum(p, axis=-1, keepdims=True)).astype(BF16))
    outs = [jnp.dot(p, vm_ref[:, sl], preferred_element_type=F32) for p, sl in zip(probs, heads)]
    o = jnp.concatenate(outs, axis=1).astype(BF16)
    o_ref[...] = x2 + jnp.dot(o, wxo_ref[...], preferred_element_type=F32)


def _post(x, od, y_conv, ln_g, ln_b, w_out, g, wq, qg, mem, mem_g, w_kv, k_g, wxo, *, tm=512):
    s, d = x.shape
    half = od.shape[1]
    xw = wq.shape[1]
    n_mem = mem.shape[0]
    const = lambda shape: pl.BlockSpec(shape, lambda i: (0, 0))
    return pl.pallas_call(
        _post_kernel,
        grid=(s // tm,),
        in_specs=[
            pl.BlockSpec((tm, d), lambda i: (i, 0)),
            pl.BlockSpec((tm, half), lambda i: (i, 0)),
            pl.BlockSpec((tm, half), lambda i: (i, 0)),
            const((1, half)), const((1, half)),
            const((2 * half, d)), const((1, d)), const((d, xw)), const((1, XATTN_HDIM)),
            const((n_mem, d)), const((1, d)), const((d, 2 * xw)), const((1, XATTN_HDIM)), const((xw, d)),
        ],
        out_specs=pl.BlockSpec((tm, d), lambda i: (i, 0)),
        out_shape=jax.ShapeDtypeStruct((s, d), F32),
        scratch_shapes=[pltpu.VMEM((xw, n_mem), BF16), pltpu.VMEM((n_mem, xw), BF16)],
        compiler_params=pltpu.CompilerParams(
            dimension_semantics=("arbitrary",), vmem_limit_bytes=VMEM_LIMIT),
        name="post",
    )(x, od, y_conv, ln_g, ln_b, w_out, g, wq, qg, mem, mem_g, w_kv, k_g, wxo)


def _rope_tables(positions):
    half = ROT_DIM // 2
    inv_freq = ROPE_THETA ** (-jnp.arange(0, ROT_DIM, 2, dtype=F32) / ROT_DIM)
    ang = positions.astype(F32)[:, None] * inv_freq
    cos, sin = jnp.cos(ang), jnp.sin(ang)
    n = positions.shape[0]
    rest = DIFF_QKDIM - ROT_DIM
    rc = jnp.concatenate([cos, cos, jnp.ones((n, rest), F32)], axis=1)
    ra = jnp.concatenate([-sin, jnp.zeros((n, half + rest), F32)], axis=1)
    rb = jnp.concatenate([jnp.zeros((n, half), F32), sin, jnp.zeros((n, rest), F32)], axis=1)
    rep = LANES // DIFF_QKDIM
    return jnp.tile(rc, (1, rep)), jnp.tile(ra, (1, rep)), jnp.tile(rb, (1, rep))


def kernel(x, mem, positions, ffn1_norm_g, ffn1_w_gate, ffn1_w_up, ffn1_w_down, mix_norm_g, w_in, q_norm_g, k_norm_g, lambda_q1, lambda_k1, lambda_q2, lambda_k2, diff_head_norm_g, conv_w, conv_b, conv_ln_g, conv_ln_b, w_out, xattn_norm_g, mem_norm_g, xattn_w_q, xattn_w_kv, xattn_q_norm_g, xattn_k_norm_g, xattn_w_o, ffn2_norm_g, ffn2_w_gate, ffn2_w_up, ffn2_w_down):
    b, s, d = x.shape
    assert b == 1 and ffn1_norm_g.shape[0] == 1, "one sequence, one layer"
    assert s % FFN_TM == 0 and s % PROJ_TM == 0 and ffn1_w_gate.shape[-1] % FFN_TF == 0
    bf = lambda w: w[0].astype(BF16)
    row = lambda v: v[0][None, :]

    rc, ra, rb = _rope_tables(positions[0])
    width = DIFF_HEADS * DIFF_VDIM
    qg = jnp.tile(q_norm_g[0], width // DIFF_QKDIM)[None, :]
    kg = jnp.tile(k_norm_g[0], width // DIFF_QKDIM)[None, :]
    blk = jnp.arange(MXU_DIM) // DIFF_QKDIM
    bd = (blk[:, None] == blk[None, :]).astype(BF16)

    n_i, n_j = s // FFN_TM, ffn1_w_gate.shape[-1] // FFN_TF

    def by_rows(w, scale=1.0):
        r, c = w.shape
        assert r % (n_i * _BF16_ROWS) == 0 and c % FFN_TF == 0 and c // FFN_TF <= n_j, w.shape
        last = c // FFN_TF - 1
        return _HostedCast(w, (r // n_i, FFN_TF), lambda i, j: (i, jnp.minimum(j, last)), scale)

    def by_cols(w, scale=1.0):
        r, c = w.shape
        assert r % (n_j * _BF16_ROWS) == 0 and c % (n_i * LANES) == 0, w.shape
        return _HostedCast(w, (r // n_j, c // n_i), lambda i, j: (j, i), scale)

    x1, wg2, wu2, wd2_half, w_in_bf = _ffn(
        x[0], row(ffn1_norm_g), bf(ffn1_w_gate), bf(ffn1_w_up), (0.5 * ffn1_w_down[0]).astype(BF16),
        [by_rows(ffn2_w_gate[0]), by_rows(ffn2_w_up[0]), by_cols(ffn2_w_down[0], 0.5), by_rows(w_in[0])])

    n_p = s // PROJ_TM

    def row_block(w):
        assert w.shape[0] % (n_p * _BF16_ROWS) == 0, w.shape
        return _HostedCast(w, (w.shape[0] // n_p, w.shape[1]), lambda i: (i, 0))

    qk, vt, u, w_out_bf, wq_bf, wkv_bf, wo_bf = _proj(
        x1, row(mix_norm_g), w_in_bf, qg, kg, bd, rc, ra, rb,
        [row_block(w_out[0]), row_block(xattn_w_q[0]), row_block(xattn_w_kv[0]), row_block(xattn_w_o[0])])
    o_diff = _attn(qk, vt, row(lambda_q1), row(lambda_k1), row(lambda_q2), row(lambda_k2),
                   row(diff_head_norm_g))
    conv_taps = jnp.broadcast_to(conv_w[0][:, None, :], (CONV_K, _SUB, conv_w.shape[-1]))
    conv_bias = jnp.broadcast_to(conv_b[0][None, :], (_SUB, conv_b.shape[-1]))
    y_conv = _conv(u, conv_taps, conv_bias)

    x3 = _post(x1, o_diff, y_conv, row(conv_ln_g), row(conv_ln_b), w_out_bf, row(xattn_norm_g), wq_bf,
               row(xattn_q_norm_g), mem[0], row(mem_norm_g), wkv_bf, row(xattn_k_norm_g), wo_bf)

    out = _ffn(x3, row(ffn2_norm_g), wg2, wu2, wd2_half)[0]
    return out[None]
```
